```python
import jax, jax.numpy as jnp
from jax import lax
import numpy as np

D_MODEL = 1024
BATCH = 8
SEQ = 2048
DEPTH = 1
DEC_BATCH = 128
DEC_SEQ = 8
PAST_LEN = 16384
PAGE_SIZE = 128

MIX_WIDTH = D_MODEL
A_WIDTH = MIX_WIDTH // 2
B_WIDTH = MIX_WIDTH - A_WIDTH
H_A = 4
DK_A = A_WIDTH // H_A
DV_A = A_WIDTH // H_A
H_B = 4
DK_B = B_WIDTH // H_B
DV_B = B_WIDTH // H_B
IN_WIDTH = 4 * A_WIDTH + 4 * B_WIDTH
D_FF = 2816
CONV_W = 3
CHUNK = 64
ROPE_BASE = 10000.0
EPS = 1e-6

kernel_name = "hgrn2_retention_convffn_hybrid_step"

F32 = jnp.float32


def _rmsnorm(x, w):
    xf = x.astype(F32)
    return xf * lax.rsqrt(jnp.mean(xf * xf, axis=-1, keepdims=True) + EPS) * w.astype(F32)


def _groupnorm(x, w):
    xf = x.astype(F32)
    mu = jnp.mean(xf, axis=-1, keepdims=True)
    xc = xf - mu
    return xc * lax.rsqrt(jnp.mean(xc * xc, axis=-1, keepdims=True) + EPS) * w.astype(F32)


def _heads(a, h):
    return a.reshape(a.shape[0], a.shape[1], h, -1)


def _rope(x, pos):
    half = x.shape[-1] // 2
    inv = 1.0 / (ROPE_BASE ** (jnp.arange(half, dtype=F32) / half))
    ang = pos.astype(F32)[:, None] * inv[None, :]
    cos = jnp.cos(ang)[None, :, None, :]
    sin = jnp.sin(ang)[None, :, None, :]
    x1, x2 = x[..., :half], x[..., half:]
    return jnp.concatenate([x1 * cos - x2 * sin, x1 * sin + x2 * cos], axis=-1)


def _chunk_len(t):
    return CHUNK if t % CHUNK == 0 else t


def _to_chunks(a, c):
    nb, nt, h, d = a.shape
    return a.reshape(nb, nt // c, c, h, d).transpose(1, 0, 3, 2, 4)


def _from_chunks(a):
    n, nb, h, c, d = a.shape
    return a.transpose(1, 0, 3, 2, 4).reshape(nb, n * c, h, d)


def _hgrn2_chunked(q, k, log_f, v, s0):
    c = _chunk_len(q.shape[1])
    mask = jnp.tril(jnp.ones((c, c), dtype=bool))[:, :, None]

    def step(S, inp):
        qq, kk, gg, vv = inp
        b = jnp.cumsum(gg, axis=2)
        rel = b[:, :, :, None, :] - b[:, :, None, :, :]
        dec = jnp.where(mask, jnp.exp(jnp.where(mask, rel, 0.0)), 0.0)
        scores = jnp.einsum('bhtd,bhsd,bhtsd->bhts', qq, kk, dec)
        o = (jnp.einsum('bhts,bhsv->bhtv', scores, vv)
             + jnp.einsum('bhtd,bhdv->bhtv', qq * jnp.exp(b), S))
        b_last = b[:, :, -1:, :]
        S_new = (jnp.exp(b_last[:, :, 0, :])[..., None] * S
                 + jnp.einsum('bhsd,bhsv->bhdv', kk * jnp.exp(b_last - b), vv))
        return S_new, o

    S, o = lax.scan(step, s0, (_to_chunks(q, c), _to_chunks(k, c), _to_chunks(log_f, c), _to_chunks(v, c)))
    return _from_chunks(o), S


def _retention_chunked(q, k, v, s0, log_gamma):
    c = _chunk_len(q.shape[1])
    idx = jnp.arange(c, dtype=F32)
    rel = idx[:, None] - idx[None, :]
    causal = rel >= 0
    lg = log_gamma[:, None, None]
    dec = jnp.where(causal[None], jnp.exp(jnp.where(causal, rel, 0.0)[None] * lg), 0.0)
    inner = jnp.exp((idx + 1.0)[None, :] * log_gamma[:, None])[..., None]
    sdec = jnp.exp((c - 1.0 - idx)[None, :] * log_gamma[:, None])[..., None]
    cdec = jnp.exp(c * log_gamma)[:, None, None]

    def step(S, inp):
        qq, kk, vv = inp
        scores = jnp.einsum('bhtd,bhsd->bhts', qq, kk) * dec
        o = jnp.einsum('bhts,bhsv->bhtv', scores, vv) + jnp.einsum('bhtd,bhdv->bhtv', qq, S) * inner
        S_new = cdec * S + jnp.einsum('bhsd,bhsv->bhdv', kk * sdec, vv)
        return S_new, o

    S, o = lax.scan(step, s0, (_to_chunks(q, c), _to_chunks(k, c), _to_chunks(v, c)))
    return _from_chunks(o), S


def _token_mix(h, pos, s_a, s_b, w_in, lb, norm_a, norm_b, w_out):
    nb, nt, _ = h.shape
    proj = h @ w_in
    A, Bw = A_WIDTH, B_WIDTH
    qa, fa, ia, ga, qb, kb, vb, gb = jnp.split(
        proj, [A, 2 * A, 3 * A, 4 * A, 4 * A + Bw, 4 * A + 2 * Bw, 4 * A + 3 * Bw], axis=-1)
    lbh = lb.reshape(H_A, DK_A)
    f = lbh + (1.0 - lbh) * jax.nn.sigmoid(_heads(fa.astype(F32), H_A))
    oa, s_a_new = _hgrn2_chunked(_heads(qa.astype(F32), H_A), 1.0 - f, jnp.log(f),
                                 _heads(ia.astype(F32), H_A), s_a.astype(F32))
    oa = _rmsnorm(oa, norm_a) * jax.nn.silu(_heads(ga.astype(F32), H_A))
    log_gamma = jnp.log1p(-jnp.exp2(-5.0 - jnp.arange(H_B, dtype=F32)))
    qr = _rope(_heads(qb.astype(F32), H_B), pos)
    kr = _rope(_heads(kb.astype(F32), H_B), pos) * (DK_B ** -0.5)
    ob, s_b_new = _retention_chunked(qr, kr, _heads(vb.astype(F32), H_B), s_b.astype(F32), log_gamma)
    ob = _groupnorm(ob, norm_b) * jax.nn.silu(_heads(gb.astype(F32), H_B))
    o = jnp.concatenate([oa.reshape(nb, nt, A), ob.reshape(nb, nt, Bw)], axis=-1).astype(h.dtype)
    return o @ w_out, s_a_new, s_b_new


def _conv_ffn(h, buf, w_up, conv_w, conv_b, w_down):
    nt = h.shape[1]
    up = h @ w_up
    ext = jnp.concatenate([buf.astype(up.dtype), up], axis=1)
    c = conv_b + sum(ext[:, j:j + nt] * conv_w[j] for j in range(CONV_W))
    u, g = jnp.split(c, 2, axis=-1)
    return (jax.nn.silu(g) * u) @ w_down, ext[:, nt:]


def _trunk(x, pos, s_a, s_b, s_c, w_norm1, w_in, hgrn_lb, hgrn_norm_w, ret_norm_w, w_out,
           w_norm2, w_ffn_in, conv_w, conv_b, w_ffn_out, w_norm_f):
    lb_all = jnp.cumsum(jax.nn.softmax(hgrn_lb.astype(F32), axis=0), axis=0)
    na, nbs, nc = [], [], []
    for l in range(DEPTH):
        h = _rmsnorm(x, w_norm1[l]).astype(x.dtype)
        mix, sa, sb = _token_mix(h, pos, s_a[l], s_b[l], w_in[l], lb_all[l], hgrn_norm_w[l],
                                 ret_norm_w[l], w_out[l])
        x = x + mix
        h = _rmsnorm(x, w_norm2[l]).astype(x.dtype)
        ff, sc = _conv_ffn(h, s_c[l], w_ffn_in[l], conv_w[l], conv_b[l], w_ffn_out[l])
        x = x + ff
        na.append(sa); nbs.append(sb); nc.append(sc)
    y = _rmsnorm(x, w_norm_f).astype(x.dtype)
    return (y, jnp.stack(na).astype(x.dtype), jnp.stack(nbs).astype(x.dtype),
            jnp.stack(nc).astype(x.dtype))


def setup_inputs(seed: int = 0) -> dict:
    key = jax.random.key(seed)
    ks = jax.random.split(key, 20)
    nrm = jax.random.normal
    return {
        "x_prompt": nrm(ks[0], (BATCH, SEQ, D_MODEL), F32),
        "x_sample": nrm(ks[1], (DEC_BATCH, DEC_SEQ, D_MODEL), F32),
        "state_hgrn": 0.3 * nrm(ks[2], (DEPTH, DEC_BATCH, H_A, DK_A, DV_A), F32),
        "state_ret": 0.3 * nrm(ks[3], (DEPTH, DEC_BATCH, H_B, DK_B, DV_B), F32),
        "state_conv": nrm(ks[4], (DEPTH, DEC_BATCH, CONV_W - 1, 2 * D_FF), F32),
        "w_norm1": 1.0 + 0.02 * nrm(ks[5], (DEPTH, D_MODEL), F32),
        "w_in": nrm(ks[6], (DEPTH, D_MODEL, IN_WIDTH), F32) * D_MODEL ** -0.5,
        "hgrn_lb": 0.1 * nrm(ks[7], (DEPTH + 1, A_WIDTH), F32),
        "hgrn_norm_w": 1.0 + 0.02 * nrm(ks[8], (DEPTH, DV_A), F32),
        "ret_norm_w": 1.0 + 0.02 * nrm(ks[9], (DEPTH, DV_B), F32),
        "w_out": nrm(ks[10], (DEPTH, MIX_WIDTH, D_MODEL), F32) * MIX_WIDTH ** -0.5,
        "w_norm2": 1.0 + 0.02 * nrm(ks[11], (DEPTH, D_MODEL), F32),
        "w_ffn_in": nrm(ks[12], (DEPTH, D_MODEL, 2 * D_FF), F32) * D_MODEL ** -0.5,
        "conv_w": nrm(ks[13], (DEPTH, CONV_W, 2 * D_FF), F32) * CONV_W ** -0.5,
        "conv_b": 0.02 * nrm(ks[14], (DEPTH, 2 * D_FF), F32),
        "w_ffn_out": nrm(ks[15], (DEPTH, D_FF, D_MODEL), F32) * D_FF ** -0.5,
        "w_norm_f": 1.0 + 0.02 * nrm(ks[16], (D_MODEL,), F32),
    }


def reference(x_prompt, x_sample, state_hgrn, state_ret, state_conv, w_norm1, w_in, hgrn_lb,
              hgrn_norm_w, ret_norm_w, w_out, w_norm2, w_ffn_in, conv_w, conv_b, w_ffn_out,
              w_norm_f):
    nbp, ntp, _ = x_prompt.shape
    nts = x_sample.shape[1]
    pos_p = jnp.arange(ntp, dtype=jnp.int32)
    pos_s = PAST_LEN + jnp.arange(nts, dtype=jnp.int32)
    z_a = jnp.zeros((DEPTH, nbp, H_A, DK_A, DV_A), F32)
    z_b = jnp.zeros((DEPTH, nbp, H_B, DK_B, DV_B), F32)
    z_c = jnp.zeros((DEPTH, nbp, CONV_W - 1, 2 * D_FF), x_prompt.dtype)
    y_prompt, ha_p, rb_p, cv_p = _trunk(
        x_prompt, pos_p, z_a, z_b, z_c, w_norm1, w_in, hgrn_lb, hgrn_norm_w, ret_norm_w, w_out,
        w_norm2, w_ffn_in, conv_w, conv_b, w_ffn_out, w_norm_f)
    y_sample, ha_s, rb_s, cv_s = _trunk(
        x_sample, pos_s, state_hgrn, state_ret, state_conv, w_norm1, w_in, hgrn_lb, hgrn_norm_w,
        ret_norm_w, w_out, w_norm2, w_ffn_in, conv_w, conv_b, w_ffn_out, w_norm_f)
    return (y_prompt, y_sample, ha_p, rb_p, cv_p, ha_s, rb_s, cv_s)
```

```python
import functools

import numpy as np
import jax
import jax.numpy as jnp
from jax import lax
from jax.experimental import pallas as pl
from jax.experimental.pallas import tpu as pltpu

F32 = jnp.float32
BF16 = jnp.bfloat16

D_MODEL = 1024
N_HEADS = 4
D_HEAD = 128
GROUP_W = N_HEADS * D_HEAD
IN_WIDTH = 8 * GROUP_W
D_FF = 2816
FF2 = 2 * D_FF
CONV_W = 3
PAST_LEN = 16384
ROPE_BASE = 10000.0
EPS = 1e-6

CHUNK = 64
SUBLANES = 8
PROJ_COLS = 512
FF_COLS = 256
VMEM_LIMIT = 56 * 1024 * 1024


def _mm(a, b):
    return jnp.dot(a, b, preferred_element_type=F32)


def _mm_nt(a, b):
    return lax.dot_general(a, b, (((1,), (1,)), ((), ())), preferred_element_type=F32)


def _mm_tn(a, b):
    return lax.dot_general(a, b, (((0,), (0,)), ((), ())), preferred_element_type=F32)


def _sigmoid(x):
    return 1.0 / (1.0 + jnp.exp(-x))


def _rmsnorm(x, w):
    return x * lax.rsqrt(jnp.mean(x * x, axis=-1, keepdims=True) + EPS) * w


def _groupnorm(x, w):
    xc = x - jnp.mean(x, axis=-1, keepdims=True)
    return xc * lax.rsqrt(jnp.mean(xc * xc, axis=-1, keepdims=True) + EPS) * w


def _chunk_consts(chunk, seq_len):
    nlev = int(np.log2(seq_len))
    assert 1 << nlev == seq_len and chunk % seq_len == 0
    r = np.arange(chunk)
    rr, cc = r[:, None], r[None, :]
    same_seq = (rr // seq_len) == (cc // seq_len)
    mats = []
    for lev in range(1, nlev + 1):
        m = 1 << lev
        mid = (r // m) * m + m // 2 - 1
        upper = ((r % m) >= m // 2)[:, None]
        mats.append((upper & (cc > mid[:, None]) & (cc <= rr)) | (~upper & (cc > rr) & (cc <= mid[:, None])))
    mats.append(same_seq & (cc <= rr))
    mats.append(same_seq & (cc > rr))
    mstack = np.concatenate(mats, axis=0).astype(np.float32)
    x = rr ^ cc
    bit_len = np.where(x > 0, np.floor(np.log2(np.maximum(x, 1))).astype(np.int64) + 1, 0)
    level = np.where(same_seq & (cc <= rr), bit_len, -1).astype(np.int32)

    pos = r % seq_len
    log_gamma = np.log1p(-np.exp2(-5.0 - np.arange(N_HEADS, dtype=np.float64)))[:, None, None]
    rel = (pos[:, None] - pos[None, :]).astype(np.float64)[None]
    causal = (same_seq & (cc <= rr))[None]
    dec = np.where(causal, np.exp(np.where(causal, rel, 0.0) * log_gamma), 0.0)
    ones = np.ones((1, 1, D_HEAD))
    inner = np.exp((pos + 1.0)[None, :, None] * log_gamma) * ones
    sdec = np.exp((seq_len - 1.0 - pos)[None, :, None] * log_gamma) * ones
    cdec = tuple(float(v) for v in np.exp(seq_len * log_gamma[:, 0, 0]))
    consts = (jnp.asarray(mstack, BF16), jnp.asarray(level), jnp.asarray(dec, F32), jnp.asarray(inner, F32),
              jnp.asarray(sdec, F32))
    return nlev, cdec, consts


def _rope_tables(pos):
    half = D_HEAD // 2
    inv = 1.0 / (ROPE_BASE ** (jnp.arange(half, dtype=F32) / half))
    ang = pos.astype(F32)[:, None] * inv[None, :]
    cos, sin = jnp.cos(ang), jnp.sin(ang)
    return jnp.concatenate([cos, cos], axis=-1), jnp.concatenate([-sin, sin], axis=-1)


def _mix_kernel(*refs, sample, n_chunks, nlev, cdec):
    (x_ref, cos_ref, sin_ref, w1_ref, win_ref, lbp_ref, na_ref, nb_ref, wout_ref,
     mstack_ref, level_ref, dec_ref, inner_ref, sdec_ref) = refs[:14]
    refs = refs[14:]
    if sample:
        sa_in_ref, sb_in_ref = refs[:2]
        refs = refs[2:]
    x1_ref, sa_out_ref, sb_out_ref, h_scr, proj_scr, o_scr, d_scr = refs[:7]
    if sample:
        qe_scr, kh_scr, eb_scr, qi_scr, ks_scr = refs[7:]
    else:
        sa_scr, sb_scr = refs[7:]
        step = pl.program_id(1)

        @pl.when(step == 0)
        def _():
            sa_scr[...] = jnp.zeros_like(sa_scr)
            sb_scr[...] = jnp.zeros_like(sb_scr)

    C = CHUNK
    h_scr[...] = _rmsnorm(x_ref[...], w1_ref[...]).astype(BF16)
    for n in range(0, IN_WIDTH, PROJ_COLS):
        proj_scr[:, n:n + PROJ_COLS] = _mm(h_scr[...], win_ref[:, n:n + PROJ_COLS])

    lb0, lb1 = lbp_ref[0:1, :], lbp_ref[1:2, :]
    lb_max = jnp.maximum(lb0, lb1)
    e0, e1 = jnp.exp(lb0 - lb_max), jnp.exp(lb1 - lb_max)
    lb = e0 / (e0 + e1)

    row_id = lax.broadcasted_iota(jnp.int32, (C, D_HEAD), 0)
    level = level_ref[...]

    def col(group, h):
        return slice(group * GROUP_W + h * D_HEAD, group * GROUP_W + (h + 1) * D_HEAD)

    def chunk_body(c, carry):
        rows = pl.ds(pl.multiple_of(c * C, C), C)

        f = lb + (1.0 - lb) * _sigmoid(proj_scr[rows, GROUP_W:2 * GROUP_W])
        g = jnp.log(f)
        g_hi = g.astype(BF16)
        g_lo = (g - g_hi.astype(F32)).astype(BF16)
        d_scr[...] = _mm(mstack_ref[...], g_hi) + _mm(mstack_ref[...], g_lo)

        for h in range(N_HEADS):
            hs = slice(h * D_HEAD, (h + 1) * D_HEAD)
            q = proj_scr[rows, col(0, h)]
            k = 1.0 - f[:, hs]
            v = proj_scr[rows, col(2, h)].astype(BF16)
            a = jnp.where(level == 0, _mm_nt(q.astype(BF16), k.astype(BF16)), 0.0)
            for lev in range(1, nlev + 1):
                upper = (row_id & (1 << (lev - 1))) != 0
                z = (jnp.where(upper, q, k) * jnp.exp(d_scr[(lev - 1) * C:lev * C, hs])).astype(BF16)
                a = jnp.where(level == lev, _mm_nt(z, z), a)
            o = _mm(a.astype(BF16), v)
            eb = jnp.exp(d_scr[nlev * C:(nlev + 1) * C, hs])
            qe = q * eb
            kh = k * jnp.exp(d_scr[(nlev + 1) * C:(nlev + 2) * C, hs])
            if sample:
                o_scr[rows, col(0, h)] = o
                qe_scr[:, hs] = qe
                kh_scr[:, hs] = kh
                eb_scr[:, hs] = eb
            else:
                st = sa_scr[h]
                o = o + _mm_nt(qe.astype(BF16), st.astype(BF16))
                sa_scr[h] = st * eb[C - 1:C, :] + _mm_tn(v, kh.astype(BF16))
                gate = proj_scr[rows, col(3, h)]
                o_scr[rows, col(0, h)] = _rmsnorm(o, na_ref[...]) * (gate * _sigmoid(gate))

        cos, sin = cos_ref[rows, :], sin_ref[rows, :]
        for h in range(N_HEADS):
            hs = slice(h * D_HEAD, (h + 1) * D_HEAD)
            q = proj_scr[rows, col(4, h)]
            k = proj_scr[rows, col(5, h)]
            v = proj_scr[rows, col(6, h)].astype(BF16)
            qr = q * cos + pltpu.roll(q, D_HEAD // 2, 1) * sin
            kr = (k * cos + pltpu.roll(k, D_HEAD // 2, 1) * sin) * (D_HEAD ** -0.5)
            a = _mm_nt(qr.astype(BF16), kr.astype(BF16)) * dec_ref[h]
            o = _mm(a.astype(BF16), v)
            qi = qr * inner_ref[h]
            ks = kr * sdec_ref[h]
            if sample:
                o_scr[rows, col(1, h)] = o
                qi_scr[:, hs] = qi
                ks_scr[:, hs] = ks
            else:
                st = sb_scr[h]
                o = o + _mm(qi.astype(BF16), st.astype(BF16))
                sb_scr[h] = cdec[h] * st + _mm_tn(ks.astype(BF16), v)
                gate = proj_scr[rows, col(7, h)]
                o_scr[rows, col(1, h)] = _groupnorm(o, nb_ref[...]) * (gate * _sigmoid(gate))

        if sample:
            seq_per_chunk = C // SUBLANES

            def seq_body(j, carry2):
                sidx = c * seq_per_chunk + j
                crow = pl.ds(pl.multiple_of(j * SUBLANES, SUBLANES), SUBLANES)
                trow = pl.ds(pl.multiple_of(c * C + j * SUBLANES, SUBLANES), SUBLANES)
                for h in range(N_HEADS):
                    hs = slice(h * D_HEAD, (h + 1) * D_HEAD)
                    st = sa_in_ref[sidx, h].T
                    o_scr[trow, col(0, h)] += _mm_nt(qe_scr[crow, hs].astype(BF16), st.astype(BF16))
                    v = proj_scr[trow, col(2, h)].astype(BF16)
                    decay = eb_scr[crow, hs][SUBLANES - 1:SUBLANES, :]
                    st = st * decay + _mm_tn(v, kh_scr[crow, hs].astype(BF16))
                    sa_out_ref[sidx, h] = st.T
                for h in range(N_HEADS):
                    hs = slice(h * D_HEAD, (h + 1) * D_HEAD)
                    st = sb_in_ref[sidx, h]
                    o_scr[trow, col(1, h)] += _mm(qi_scr[crow, hs].astype(BF16), st.astype(BF16))
                    v = proj_scr[trow, col(6, h)].astype(BF16)
                    sb_out_ref[sidx, h] = cdec[h] * st + _mm_tn(ks_scr[crow, hs].astype(BF16), v)
                return carry2

            lax.fori_loop(0, seq_per_chunk, seq_body, 0)
            for h in range(N_HEADS):
                gate = proj_scr[rows, col(3, h)]
                o_scr[rows, col(0, h)] = _rmsnorm(o_scr[rows, col(0, h)], na_ref[...]) * (gate * _sigmoid(gate))
                gate = proj_scr[rows, col(7, h)]
                o_scr[rows, col(1, h)] = _groupnorm(o_scr[rows, col(1, h)], nb_ref[...]) * (gate * _sigmoid(gate))
        return carry

    lax.fori_loop(0, n_chunks, chunk_body, 0)

    x1_ref[...] = x_ref[...] + _mm(o_scr[...].astype(BF16), wout_ref[...])

    if not sample:
        @pl.when(step == pl.num_programs(1) - 1)
        def _():
            for h in range(N_HEADS):
                sa_out_ref[0, h] = sa_scr[h].T
                sb_out_ref[0, h] = sb_scr[h]


def _full(shape):
    return pl.BlockSpec(shape, lambda *_: (0,) * len(shape))


def _mix_call(x2, cos, sin, w1, w_in, lbp, na, nb, w_out, states, *, n_seq, seq_len, tile):
    sample = states is not None
    n_rows = n_seq * seq_len
    n_chunks = tile // CHUNK
    assert tile % CHUNK == 0 and n_rows % tile == 0
    nlev, cdec, consts = _chunk_consts(CHUNK, min(seq_len, CHUNK))
    state_shape = jax.ShapeDtypeStruct((n_seq, N_HEADS, D_HEAD, D_HEAD), F32)
    if sample:
        assert seq_len == SUBLANES
        seq_per_tile = tile // seq_len
        grid = (n_rows // tile,)
        row_map = lambda n: (n, 0)
        pos_map = lambda n: (0, 0)
        state_spec = pl.BlockSpec((seq_per_tile, N_HEADS, D_HEAD, D_HEAD), lambda n: (n, 0, 0, 0))
        state_in = [state_spec, state_spec]
        scratch = [pltpu.VMEM((CHUNK, GROUP_W), F32)] * 5
    else:
        assert seq_len % tile == 0
        steps = seq_len // tile
        grid = (n_seq, steps)
        row_map = lambda b, i: (b * steps + i, 0)
        pos_map = lambda b, i: (i, 0)
        state_spec = pl.BlockSpec((1, N_HEADS, D_HEAD, D_HEAD), lambda b, i: (b, 0, 0, 0))
        state_in = []
        scratch = [pltpu.VMEM((N_HEADS, D_HEAD, D_HEAD), F32)] * 2
    const_specs = [_full(c.shape) for c in consts]
    in_specs = [pl.BlockSpec((tile, D_MODEL), row_map), pl.BlockSpec((tile, D_HEAD), pos_map),
                pl.BlockSpec((tile, D_HEAD), pos_map), _full(w1.shape), _full(w_in.shape), _full(lbp.shape),
                _full(na.shape), _full(nb.shape), _full(w_out.shape)] + const_specs + state_in
    args = [x2, cos, sin, w1, w_in, lbp, na, nb, w_out, *consts] + (list(states) if sample else [])
    return pl.pallas_call(
        functools.partial(_mix_kernel, sample=sample, n_chunks=n_chunks, nlev=nlev, cdec=cdec),
        grid=grid,
        in_specs=in_specs,
        out_specs=[pl.BlockSpec((tile, D_MODEL), row_map), state_spec, state_spec],
        out_shape=[jax.ShapeDtypeStruct((n_rows, D_MODEL), F32), state_shape, state_shape],
        scratch_shapes=[pltpu.VMEM((tile, D_MODEL), BF16), pltpu.VMEM((tile, IN_WIDTH), F32),
                        pltpu.VMEM((tile, 2 * GROUP_W), F32), pltpu.VMEM(((nlev + 2) * CHUNK, GROUP_W), F32)] + scratch,
        compiler_params=pltpu.CompilerParams(dimension_semantics=("arbitrary",) * len(grid),
                                             vmem_limit_bytes=VMEM_LIMIT),
        name="mix_sample" if sample else "mix_prompt",
    )(*args)


def _ffn_kernel(*refs, sample, groups, rows):
    x_ref, w2_ref, wup_ref, cw_ref, cb_ref, wdown_ref, wf_ref = refs[:7]
    refs = refs[7:]
    if sample:
        hist_ref = refs[0]
        refs = refs[1:]
    y_ref, hist_out_ref, h_scr, ext_scr, act_scr = refs
    G, L, P = groups, rows, SUBLANES
    if sample:
        ext_scr[:, P - 2:P, :] = hist_ref[...]
    else:
        step = pl.program_id(1)

        @pl.when(step == 0)
        def _():
            ext_scr[:, P - 2:P, :] = jnp.zeros((G, 2, FF2), F32)

        @pl.when(step > 0)
        def _():
            ext_scr[:, P - 2:P, :] = ext_scr[:, P + L - 2:P + L, :]

    h_scr[...] = _rmsnorm(x_ref[...], w2_ref[...]).astype(BF16)
    for n in range(0, D_FF, FF_COLS):
        conv = []
        for cols in (slice(n, n + FF_COLS), slice(D_FF + n, D_FF + n + FF_COLS)):
            up = _mm(h_scr[...], wup_ref[:, cols]).reshape(G, L, FF_COLS)
            ext_scr[:, P:P + L, cols] = up
            conv.append(cb_ref[:, cols] + cw_ref[0:1, cols] * ext_scr[:, P - 2:P + L - 2, cols]
                        + cw_ref[1:2, cols] * ext_scr[:, P - 1:P + L - 1, cols] + cw_ref[2:3, cols] * up)
        u, g = conv
        act_scr[:, n:n + FF_COLS] = ((g * _sigmoid(g)) * u).reshape(G * L, FF_COLS).astype(BF16)

    x2 = x_ref[...] + _mm(act_scr[...], wdown_ref[...])
    y_ref[...] = _rmsnorm(x2, wf_ref[...])

    if sample:
        hist_out_ref[...] = ext_scr[:, P + L - 2:P + L, :]
    else:
        @pl.when(step == pl.num_programs(1) - 1)
        def _():
            hist_out_ref[...] = ext_scr[:, P + L - 2:P + L, :]


def _ffn_call(x1, w2, w_up, cw, cb, w_down, wf, hist, *, n_seq, seq_len, tile):
    sample = hist is not None
    n_rows = n_seq * seq_len
    hist_shape = jax.ShapeDtypeStruct((n_seq, CONV_W - 1, FF2), F32)
    if sample:
        groups, rows = tile // seq_len, seq_len
        grid = (n_rows // tile,)
        row_map = lambda n: (n, 0)
        hist_spec = pl.BlockSpec((groups, CONV_W - 1, FF2), lambda n: (n, 0, 0))
        hist_in = [hist_spec]
    else:
        groups, rows = 1, tile
        steps = seq_len // tile
        grid = (n_seq, steps)
        row_map = lambda b, i: (b * steps + i, 0)
        hist_spec = pl.BlockSpec((1, CONV_W - 1, FF2), lambda b, i: (b, 0, 0))
        hist_in = []
    in_specs = [pl.BlockSpec((tile, D_MODEL), row_map), _full(w2.shape), _full(w_up.shape), _full(cw.shape),
                _full(cb.shape), _full(w_down.shape), _full(wf.shape)] + hist_in
    args = [x1, w2, w_up, cw, cb, w_down, wf] + ([hist] if sample else [])
    return pl.pallas_call(
        functools.partial(_ffn_kernel, sample=sample, groups=groups, rows=rows),
        grid=grid,
        in_specs=in_specs,
        out_specs=[pl.BlockSpec((tile, D_MODEL), row_map), hist_spec],
        out_shape=[jax.ShapeDtypeStruct((n_rows, D_MODEL), F32), hist_shape],
        scratch_shapes=[pltpu.VMEM((tile, D_MODEL), BF16), pltpu.VMEM((groups, SUBLANES + rows, FF2), F32),
                        pltpu.VMEM((tile, D_FF), BF16)],
        compiler_params=pltpu.CompilerParams(dimension_semantics=("arbitrary",) * len(grid),
                                             vmem_limit_bytes=VMEM_LIMIT),
        name="ffn_sample" if sample else "ffn_prompt",
    )(*args)


def _trunk(x, pos, states, hist, params, *, mix_tile, ffn_tile):
    w1, w_in, lbp, na, nb, w_out, w2, w_up, cw, cb, w_down, wf = params
    n_seq, seq_len, _ = x.shape
    cos, sin = _rope_tables(pos)
    if states is not None:
        cos, sin = (jnp.tile(t, (mix_tile // seq_len, 1)) for t in (cos, sin))
    x2 = x.reshape(n_seq * seq_len, D_MODEL)
    x1, s_a, s_b = _mix_call(x2, cos, sin, w1, w_in, lbp, na, nb, w_out, states,
                             n_seq=n_seq, seq_len=seq_len, tile=mix_tile)
    y, s_c = _ffn_call(x1, w2, w_up, cw, cb, w_down, wf, hist, n_seq=n_seq, seq_len=seq_len, tile=ffn_tile)
    return y.reshape(x.shape), s_a[None], s_b[None], s_c[None]


def kernel(x_prompt, x_sample, state_hgrn, state_ret, state_conv, w_norm1, w_in, hgrn_lb, hgrn_norm_w, ret_norm_w,
           w_out, w_norm2, w_ffn_in, conv_w, conv_b, w_ffn_out, w_norm_f):
    assert w_in.shape == (1, D_MODEL, IN_WIDTH) and hgrn_lb.shape == (2, GROUP_W)
    params = (w_norm1, w_in[0].astype(BF16), hgrn_lb, hgrn_norm_w, ret_norm_w, w_out[0].astype(BF16),
              w_norm2, w_ffn_in[0].astype(BF16), conv_w[0], conv_b, w_ffn_out[0].astype(BF16),
              w_norm_f.reshape(1, D_MODEL))
    pos_p = jnp.arange(x_prompt.shape[1], dtype=jnp.int32)
    pos_s = PAST_LEN + jnp.arange(x_sample.shape[1], dtype=jnp.int32)
    y_p, ha_p, rb_p, cv_p = _trunk(x_prompt, pos_p, None, None, params, mix_tile=256, ffn_tile=256)
    y_s, ha_s, rb_s, cv_s = _trunk(x_sample, pos_s, (state_hgrn[0], state_ret[0]), state_conv[0], params,
                                   mix_tile=64, ffn_tile=128)
    return (y_p, y_s, ha_p, rb_p, cv_p, ha_s, rb_s, cv_s)
```

```python
import functools

import numpy as np
import jax
import jax.numpy as jnp
from jax import lax
from jax.experimental import pallas as pl
from jax.experimental.pallas import tpu as pltpu

F32 = jnp.float32
BF16 = jnp.bfloat16

D_MODEL = 1024
N_HEADS = 4
D_HEAD = 128
GROUP_W = N_HEADS * D_HEAD
IN_WIDTH = 8 * GROUP_W
D_FF = 2816
FF2 = 2 * D_FF
CONV_W = 3
PAST_LEN = 16384
ROPE_BASE = 10000.0
EPS = 1e-6

SUBLANES = 8
PROJ_COLS = 512
FF_COLS = 256
VMEM_LIMIT = 56 * 1024 * 1024


def _mm(a, b):
    return jnp.dot(a, b, preferred_element_type=F32)


def _mm_nt(a, b):
    return lax.dot_general(a, b, (((1,), (1,)), ((), ())), preferred_element_type=F32)


def _mm_tn(a, b):
    return lax.dot_general(a, b, (((0,), (0,)), ((), ())), preferred_element_type=F32)


def _sigmoid(x):
    return 1.0 / (1.0 + jnp.exp(-x))


def _rmsnorm(x, w):
    return x * lax.rsqrt(jnp.mean(x * x, axis=-1, keepdims=True) + EPS) * w


def _groupnorm(x, w):
    xc = x - jnp.mean(x, axis=-1, keepdims=True)
    return xc * lax.rsqrt(jnp.mean(xc * xc, axis=-1, keepdims=True) + EPS) * w


def _chunk_consts(chunk, seq_len):
    nlev = int(np.log2(seq_len))
    assert 1 << nlev == seq_len and chunk % seq_len == 0
    r = np.arange(chunk)
    rr, cc = r[:, None], r[None, :]
    same_seq = (rr // seq_len) == (cc // seq_len)
    mats = []
    for lev in range(1, nlev + 1):
        m = 1 << lev
        mid = (r // m) * m + m // 2 - 1
        upper = ((r % m) >= m // 2)[:, None]
        mats.append((upper & (cc > mid[:, None]) & (cc <= rr)) | (~upper & (cc > rr) & (cc <= mid[:, None])))
    mats.append(same_seq & (cc <= rr))
    mats.append(same_seq & (cc > rr))
    mstack = np.concatenate(mats, axis=0).astype(np.float32)
    x = rr ^ cc
    bit_len = np.where(x > 0, np.floor(np.log2(np.maximum(x, 1))).astype(np.int64) + 1, 0)
    level = np.where(same_seq & (cc <= rr), bit_len, -1).astype(np.int32)

    pos = r % seq_len
    log_gamma = np.log1p(-np.exp2(-5.0 - np.arange(N_HEADS, dtype=np.float64)))[:, None, None]
    rel = (pos[:, None] - pos[None, :]).astype(np.float64)[None]
    causal = (same_seq & (cc <= rr))[None]
    dec = np.where(causal, np.exp(np.where(causal, rel, 0.0) * log_gamma), 0.0)
    ones = np.ones((1, 1, D_HEAD))
    inner = np.exp((pos + 1.0)[None, :, None] * log_gamma) * ones
    sdec = np.exp((seq_len - 1.0 - pos)[None, :, None] * log_gamma) * ones
    cdec = tuple(float(v) for v in np.exp(seq_len * log_gamma[:, 0, 0]))
    consts = (jnp.asarray(mstack, BF16), jnp.asarray(level), jnp.asarray(dec, F32), jnp.asarray(inner, F32),
              jnp.asarray(sdec, F32))
    return nlev, cdec, consts


def _rope_tables(pos):
    half = D_HEAD // 2
    inv = 1.0 / (ROPE_BASE ** (jnp.arange(half, dtype=F32) / half))
    ang = pos.astype(F32)[:, None] * inv[None, :]
    cos, sin = jnp.cos(ang), jnp.sin(ang)
    return jnp.concatenate([cos, cos], axis=-1), jnp.concatenate([-sin, sin], axis=-1)


def _mix_kernel(*refs, sample, tile, nlev, cdec):
    (x_ref, cos_ref, sin_ref, w1_ref, win_ref, lbp_ref, na_ref, nb_ref, wout_ref,
     mstack_ref, level_ref, dec_ref, inner_ref, sdec_ref) = refs[:14]
    refs = refs[14:]
    if sample:
        sa_in_ref, sb_in_ref = refs[:2]
        refs = refs[2:]
    x1_ref, sa_out_ref, sb_out_ref, h_scr, proj_scr, o_scr, d_scr, k_scr = refs[:8]
    if sample:
        qe_scr, kh_scr, eb_scr, qi_scr, ks_scr = refs[8:]
    else:
        sa_scr, sb_scr = refs[8:]
        step = pl.program_id(1)

        @pl.when(step == 0)
        def _():
            sa_scr[...] = jnp.zeros_like(sa_scr)
            sb_scr[...] = jnp.zeros_like(sb_scr)

    C = tile
    h_scr[...] = _rmsnorm(x_ref[...], w1_ref[...]).astype(BF16)
    for n in range(0, IN_WIDTH, PROJ_COLS):
        proj_scr[:, n:n + PROJ_COLS] = _mm(h_scr[...], win_ref[:, n:n + PROJ_COLS])

    lb0, lb1 = lbp_ref[0:1, :], lbp_ref[1:2, :]
    lb_max = jnp.maximum(lb0, lb1)
    e0, e1 = jnp.exp(lb0 - lb_max), jnp.exp(lb1 - lb_max)
    lb = e0 / (e0 + e1)

    row_id = lax.broadcasted_iota(jnp.int32, (C, D_HEAD), 0)

    def col(group, h):
        return slice(group * GROUP_W + h * D_HEAD, group * GROUP_W + (h + 1) * D_HEAD)

    f = lb + (1.0 - lb) * _sigmoid(proj_scr[:, GROUP_W:2 * GROUP_W])
    k_scr[...] = 1.0 - f
    g = jnp.log(f)
    g_hi = g.astype(BF16)
    g_lo = (g - g_hi.astype(F32)).astype(BF16)
    d_scr[...] = _mm(mstack_ref[...], g_hi) + _mm(mstack_ref[...], g_lo)

    for h in range(N_HEADS):
        hs = slice(h * D_HEAD, (h + 1) * D_HEAD)
        q = proj_scr[:, col(0, h)]
        k = k_scr[:, hs]
        v = proj_scr[:, col(2, h)].astype(BF16)
        a = jnp.where(level_ref[...] == 0, _mm_nt(q.astype(BF16), k.astype(BF16)), 0.0)
        for lev in range(1, nlev + 1):
            upper = (row_id & (1 << (lev - 1))) != 0
            z = (jnp.where(upper, q, k) * jnp.exp(d_scr[(lev - 1) * C:lev * C, hs])).astype(BF16)
            a = jnp.where(level_ref[...] == lev, _mm_nt(z, z), a)
        o = _mm(a.astype(BF16), v)
        eb = jnp.exp(d_scr[nlev * C:(nlev + 1) * C, hs])
        qe = q * eb
        kh = k * jnp.exp(d_scr[(nlev + 1) * C:(nlev + 2) * C, hs])
        if sample:
            o_scr[:, col(0, h)] = o
            qe_scr[:, hs] = qe
            kh_scr[:, hs] = kh
            eb_scr[:, hs] = eb
        else:
            st = sa_scr[h]
            o = o + _mm_nt(qe.astype(BF16), st.astype(BF16))
            sa_scr[h] = st * eb[C - 1:C, :] + _mm_tn(v, kh.astype(BF16))
            gate = proj_scr[:, col(3, h)]
            o_scr[:, col(0, h)] = _rmsnorm(o, na_ref[...]) * (gate * _sigmoid(gate))

    cos, sin = cos_ref[...], sin_ref[...]
    for h in range(N_HEADS):
        hs = slice(h * D_HEAD, (h + 1) * D_HEAD)
        q = proj_scr[:, col(4, h)]
        k = proj_scr[:, col(5, h)]
        v = proj_scr[:, col(6, h)].astype(BF16)
        qr = q * cos + pltpu.roll(q, D_HEAD // 2, 1) * sin
        kr = (k * cos + pltpu.roll(k, D_HEAD // 2, 1) * sin) * (D_HEAD ** -0.5)
        a = _mm_nt(qr.astype(BF16), kr.astype(BF16)) * dec_ref[h]
        o = _mm(a.astype(BF16), v)
        qi = qr * inner_ref[h]
        ks = kr * sdec_ref[h]
        if sample:
            o_scr[:, col(1, h)] = o
            qi_scr[:, hs] = qi
            ks_scr[:, hs] = ks
        else:
            st = sb_scr[h]
            o = o + _mm(qi.astype(BF16), st.astype(BF16))
            sb_scr[h] = cdec[h] * st + _mm_tn(ks.astype(BF16), v)
            gate = proj_scr[:, col(7, h)]
            o_scr[:, col(1, h)] = _groupnorm(o, nb_ref[...]) * (gate * _sigmoid(gate))

    if sample:
        def seq_body(j, carry):
            rows = pl.ds(pl.multiple_of(j * SUBLANES, SUBLANES), SUBLANES)
            for h in range(N_HEADS):
                hs = slice(h * D_HEAD, (h + 1) * D_HEAD)
                st = sa_in_ref[j, h].T
                o_scr[rows, col(0, h)] += _mm_nt(qe_scr[rows, hs].astype(BF16), st.astype(BF16))
                v = proj_scr[rows, col(2, h)].astype(BF16)
                decay = eb_scr[rows, hs][SUBLANES - 1:SUBLANES, :]
                st = st * decay + _mm_tn(v, kh_scr[rows, hs].astype(BF16))
                sa_out_ref[j, h] = st.T
            for h in range(N_HEADS):
                hs = slice(h * D_HEAD, (h + 1) * D_HEAD)
                st = sb_in_ref[j, h]
                o_scr[rows, col(1, h)] += _mm(qi_scr[rows, hs].astype(BF16), st.astype(BF16))
                v = proj_scr[rows, col(6, h)].astype(BF16)
                sb_out_ref[j, h] = cdec[h] * st + _mm_tn(ks_scr[rows, hs].astype(BF16), v)
            return carry

        lax.fori_loop(0, C // SUBLANES, seq_body, 0)
        for h in range(N_HEADS):
            gate = proj_scr[:, col(3, h)]
            o_scr[:, col(0, h)] = _rmsnorm(o_scr[:, col(0, h)], na_ref[...]) * (gate * _sigmoid(gate))
            gate = proj_scr[:, col(7, h)]
            o_scr[:, col(1, h)] = _groupnorm(o_scr[:, col(1, h)], nb_ref[...]) * (gate * _sigmoid(gate))

    x1_ref[...] = x_ref[...] + _mm(o_scr[...].astype(BF16), wout_ref[...])

    if not sample:
        @pl.when(step == pl.num_programs(1) - 1)
        def _():
            for h in range(N_HEADS):
                sa_out_ref[0, h] = sa_scr[h].T
                sb_out_ref[0, h] = sb_scr[h]


def _full(shape):
    return pl.BlockSpec(shape, lambda *_: (0,) * len(shape))


def _mix_call(x2, cos, sin, w1, w_in, lbp, na, nb, w_out, states, *, n_seq, seq_len, tile):
    sample = states is not None
    n_rows = n_seq * seq_len
    assert n_rows % tile == 0
    nlev, cdec, consts = _chunk_consts(tile, min(seq_len, tile))
    state_shape = jax.ShapeDtypeStruct((n_seq, N_HEADS, D_HEAD, D_HEAD), F32)
    if sample:
        assert seq_len == SUBLANES
        seq_per_tile = tile // seq_len
        grid = (n_rows // tile,)
        row_map = lambda n: (n, 0)
        pos_map = lambda n: (0, 0)
        state_spec = pl.BlockSpec((seq_per_tile, N_HEADS, D_HEAD, D_HEAD), lambda n: (n, 0, 0, 0))
        state_in = [state_spec, state_spec]
        scratch = [pltpu.VMEM((tile, GROUP_W), F32)] * 5
    else:
        assert seq_len % tile == 0
        steps = seq_len // tile
        grid = (n_seq, steps)
        row_map = lambda b, i: (b * steps + i, 0)
        pos_map = lambda b, i: (i, 0)
        state_spec = pl.BlockSpec((1, N_HEADS, D_HEAD, D_HEAD), lambda b, i: (b, 0, 0, 0))
        state_in = []
        scratch = [pltpu.VMEM((N_HEADS, D_HEAD, D_HEAD), F32)] * 2
    const_specs = [_full(c.shape) for c in consts]
    in_specs = [pl.BlockSpec((tile, D_MODEL), row_map), pl.BlockSpec((tile, D_HEAD), pos_map),
                pl.BlockSpec((tile, D_HEAD), pos_map), _full(w1.shape), _full(w_in.shape), _full(lbp.shape),
                _full(na.shape), _full(nb.shape), _full(w_out.shape)] + const_specs + state_in
    args = [x2, cos, sin, w1, w_in, lbp, na, nb, w_out, *consts] + (list(states) if sample else [])
    return pl.pallas_call(
        functools.partial(_mix_kernel, sample=sample, tile=tile, nlev=nlev, cdec=cdec),
        grid=grid,
        in_specs=in_specs,
        out_specs=[pl.BlockSpec((tile, D_MODEL), row_map), state_spec, state_spec],
        out_shape=[jax.ShapeDtypeStruct((n_rows, D_MODEL), F32), state_shape, state_shape],
        scratch_shapes=[pltpu.VMEM((tile, D_MODEL), BF16), pltpu.VMEM((tile, IN_WIDTH), F32),
                        pltpu.VMEM((tile, 2 * GROUP_W), F32), pltpu.VMEM(((nlev + 2) * tile, GROUP_W), F32),
                        pltpu.VMEM((tile, GROUP_W), F32)] + scratch,
        compiler_params=pltpu.CompilerParams(dimension_semantics=("arbitrary",) * len(grid),
                                             vmem_limit_bytes=VMEM_LIMIT),
        name="mix_sample" if sample else "mix_prompt",
    )(*args)


def _ffn_kernel(*refs, sample, groups, rows):
    x_ref, w2_ref, wup_ref, cw_ref, cb_ref, wdown_ref, wf_ref = refs[:7]
    refs = refs[7:]
    if sample:
        hist_ref = refs[0]
        refs = refs[1:]
    y_ref, hist_out_ref, h_scr, ext_scr, act_scr = refs
    G, L, P = groups, rows, SUBLANES
    if sample:
        ext_scr[:, P - 2:P, :] = hist_ref[...]
    else:
        step = pl.program_id(1)

        @pl.when(step == 0)
        def _():
            ext_scr[:, P - 2:P, :] = jnp.zeros((G, 2, FF2), F32)

        @pl.when(step > 0)
        def _():
            ext_scr[:, P - 2:P, :] = ext_scr[:, P + L - 2:P + L, :]

    h_scr[...] = _rmsnorm(x_ref[...], w2_ref[...]).astype(BF16)
    for n in range(0, D_FF, FF_COLS):
        conv = []
        for cols in (slice(n, n + FF_COLS), slice(D_FF + n, D_FF + n + FF_COLS)):
            up = _mm(h_scr[...], wup_ref[:, cols]).reshape(G, L, FF_COLS)
            ext_scr[:, P:P + L, cols] = up
            conv.append(cb_ref[:, cols] + cw_ref[0:1, cols] * ext_scr[:, P - 2:P + L - 2, cols]
                        + cw_ref[1:2, cols] * ext_scr[:, P - 1:P + L - 1, cols] + cw_ref[2:3, cols] * up)
        u, g = conv
        act_scr[:, n:n + FF_COLS] = ((g * _sigmoid(g)) * u).reshape(G * L, FF_COLS).astype(BF16)

    x2 = x_ref[...] + _mm(act_scr[...], wdown_ref[...])
    y_ref[...] = _rmsnorm(x2, wf_ref[...])

    if sample:
        hist_out_ref[...] = ext_scr[:, P + L - 2:P + L, :]
    else:
        @pl.when(step == pl.num_programs(1) - 1)
        def _():
            hist_out_ref[...] = ext_scr[:, P + L - 2:P + L, :]


def _ffn_call(x1, w2, w_up, cw, cb, w_down, wf, hist, *, n_seq, seq_len, tile):
    sample = hist is not None
    n_rows = n_seq * seq_len
    hist_shape = jax.ShapeDtypeStruct((n_seq, CONV_W - 1, FF2), F32)
    if sample:
        groups, rows = tile // seq_len, seq_len
        grid = (n_rows // tile,)
        row_map = lambda n: (n, 0)
        hist_spec = pl.BlockSpec((groups, CONV_W - 1, FF2), lambda n: (n, 0, 0))
        hist_in = [hist_spec]
    else:
        groups, rows = 1, tile
        steps = seq_len // tile
        grid = (n_seq, steps)
        row_map = lambda b, i: (b * steps + i, 0)
        hist_spec = pl.BlockSpec((1, CONV_W - 1, FF2), lambda b, i: (b, 0, 0))
        hist_in = []
    in_specs = [pl.BlockSpec((tile, D_MODEL), row_map), _full(w2.shape), _full(w_up.shape), _full(cw.shape),
                _full(cb.shape), _full(w_down.shape), _full(wf.shape)] + hist_in
    args = [x1, w2, w_up, cw, cb, w_down, wf] + ([hist] if sample else [])
    return pl.pallas_call(
        functools.partial(_ffn_kernel, sample=sample, groups=groups, rows=rows),
        grid=grid,
        in_specs=in_specs,
        out_specs=[pl.BlockSpec((tile, D_MODEL), row_map), hist_spec],
        out_shape=[jax.ShapeDtypeStruct((n_rows, D_MODEL), F32), hist_shape],
        scratch_shapes=[pltpu.VMEM((tile, D_MODEL), BF16), pltpu.VMEM((groups, SUBLANES + rows, FF2), F32),
                        pltpu.VMEM((tile, D_FF), BF16)],
        compiler_params=pltpu.CompilerParams(dimension_semantics=("arbitrary",) * len(grid),
                                             vmem_limit_bytes=VMEM_LIMIT),
        name="ffn_sample" if sample else "ffn_prompt",
    )(*args)


def _trunk(x, pos, states, hist, params, *, mix_tile, ffn_tile):
    w1, w_in, lbp, na, nb, w_out, w2, w_up, cw, cb, w_down, wf = params
    n_seq, seq_len, _ = x.shape
    cos, sin = _rope_tables(pos)
    if states is not None:
        cos, sin = (jnp.tile(t, (mix_tile // seq_len, 1)) for t in (cos, sin))
    x2 = x.reshape(n_seq * seq_len, D_MODEL)
    x1, s_a, s_b = _mix_call(x2, cos, sin, w1, w_in, lbp, na, nb, w_out, states,
                             n_seq=n_seq, seq_len=seq_len, tile=mix_tile)
    y, s_c = _ffn_call(x1, w2, w_up, cw, cb, w_down, wf, hist, n_seq=n_seq, seq_len=seq_len, tile=ffn_tile)
    return y.reshape(x.shape), s_a[None], s_b[None], s_c[None]


def kernel(x_prompt, x_sample, state_hgrn, state_ret, state_conv, w_norm1, w_in, hgrn_lb, hgrn_norm_w, ret_norm_w,
           w_out, w_norm2, w_ffn_in, conv_w, conv_b, w_ffn_out, w_norm_f):
    assert w_in.shape == (1, D_MODEL, IN_WIDTH) and hgrn_lb.shape == (2, GROUP_W)
    params = (w_norm1, w_in[0].astype(BF16), hgrn_lb, hgrn_norm_w, ret_norm_w, w_out[0].astype(BF16),
              w_norm2, w_ffn_in[0].astype(BF16), conv_w[0], conv_b, w_ffn_out[0].astype(BF16),
              w_norm_f.reshape(1, D_MODEL))
    pos_p = jnp.arange(x_prompt.shape[1], dtype=jnp.int32)
    pos_s = PAST_LEN + jnp.arange(x_sample.shape[1], dtype=jnp.int32)
    y_p, ha_p, rb_p, cv_p = _trunk(x_prompt, pos_p, None, None, params, mix_tile=256, ffn_tile=256)
    y_s, ha_s, rb_s, cv_s = _trunk(x_sample, pos_s, (state_hgrn[0], state_ret[0]), state_conv[0], params,
                                   mix_tile=64, ffn_tile=128)
    return (y_p, y_s, ha_p, rb_p, cv_p, ha_s, rb_s, cv_s)
```

```python
import functools

import numpy as np
import jax
import jax.numpy as jnp
from jax import lax
from jax.experimental import pallas as pl
from jax.experimental.pallas import tpu as pltpu

F32 = jnp.float32
BF16 = jnp.bfloat16

D_MODEL = 1024
N_HEADS = 4
D_HEAD = 128
GROUP_W = N_HEADS * D_HEAD
IN_WIDTH = 8 * GROUP_W
D_FF = 2816
FF2 = 2 * D_FF
CONV_W = 3
PAST_LEN = 16384
ROPE_BASE = 10000.0
EPS = 1e-6
LOG2E = 1.4426950408889634

SUBLANES = 8
PROJ_COLS = 512
FF_COLS = 256
VMEM_LIMIT = 56 * 1024 * 1024


def _mm(a, b):
    return jnp.dot(a, b, preferred_element_type=F32)


def _mm_nt(a, b):
    return lax.dot_general(a, b, (((1,), (1,)), ((), ())), preferred_element_type=F32)


def _mm_tn(a, b):
    return lax.dot_general(a, b, (((0,), (0,)), ((), ())), preferred_element_type=F32)


def _sigmoid(x):
    return 1.0 / (1.0 + jnp.exp(-x))


def _rmsnorm(x, w):
    return x * lax.rsqrt(jnp.mean(x * x, axis=-1, keepdims=True) + EPS) * w


def _groupnorm(x, w):
    xc = x - jnp.mean(x, axis=-1, keepdims=True)
    return xc * lax.rsqrt(jnp.mean(xc * xc, axis=-1, keepdims=True) + EPS) * w


def _chunk_consts(chunk, seq_len):
    nlev = int(np.log2(seq_len))
    assert 1 << nlev == seq_len and chunk % seq_len == 0
    r = np.arange(chunk)
    rr, cc = r[:, None], r[None, :]
    same_seq = (rr // seq_len) == (cc // seq_len)
    m = 4
    mid = (r // m) * m + m // 2 - 1
    upper = ((r % m) >= m // 2)[:, None]
    lev2 = (upper & (cc > mid[:, None]) & (cc <= rr)) | (~upper & (cc > rr) & (cc <= mid[:, None]))
    cum = same_seq & (cc <= rr)
    mstack = np.concatenate([lev2, cum], axis=0).astype(np.float32)
    x = rr ^ cc
    bit_len = np.where(x > 0, np.floor(np.log2(np.maximum(x, 1))).astype(np.int64) + 1, 0)
    level = np.where(same_seq & (cc <= rr), bit_len, -1).astype(np.int32)

    pos = r % seq_len
    log_gamma = np.log1p(-np.exp2(-5.0 - np.arange(N_HEADS, dtype=np.float64)))[:, None, None]
    rel = (pos[:, None] - pos[None, :]).astype(np.float64)[None]
    causal = (same_seq & (cc <= rr))[None]
    dec = np.where(causal, np.exp(np.where(causal, rel, 0.0) * log_gamma), 0.0)
    ones = np.ones((1, 1, D_HEAD))
    inner = np.exp((pos + 1.0)[None, :, None] * log_gamma) * ones
    sdec = np.exp((seq_len - 1.0 - pos)[None, :, None] * log_gamma) * ones
    cdec = tuple(float(v) for v in np.exp(seq_len * log_gamma[:, 0, 0]))
    consts = (jnp.asarray(mstack, BF16), jnp.asarray(level), jnp.asarray(dec, F32), jnp.asarray(inner, F32),
              jnp.asarray(sdec, F32))
    return nlev, cdec, consts


def _rope_tables(pos):
    half = D_HEAD // 2
    inv = 1.0 / (ROPE_BASE ** (jnp.arange(half, dtype=F32) / half))
    ang = pos.astype(F32)[:, None] * inv[None, :]
    cos, sin = jnp.cos(ang), jnp.sin(ang)
    return jnp.concatenate([cos, cos], axis=-1), jnp.concatenate([-sin, sin], axis=-1)


def _mix_kernel(*refs, sample, tile, seq_rows, nlev, cdec):
    (x_ref, cos_ref, sin_ref, w1_ref, win_ref, lbp_ref, na_ref, nb_ref, wout_ref,
     mstack_ref, level_ref, dec_ref, inner_ref, sdec_ref) = refs[:14]
    refs = refs[14:]
    if sample:
        sa_in_ref, sb_in_ref = refs[:2]
        refs = refs[2:]
    x1_ref, sa_out_ref, sb_out_ref, h_scr, proj_scr, o_scr, d_scr, k_scr = refs[:8]
    if sample:
        qe_scr, kh_scr, eb_scr, qi_scr, ks_scr = refs[8:]
    else:
        sa_scr, sb_scr = refs[8:]
        step = pl.program_id(1)

        @pl.when(step == 0)
        def _():
            sa_scr[...] = jnp.zeros_like(sa_scr)
            sb_scr[...] = jnp.zeros_like(sb_scr)

    C = tile
    h_scr[...] = _rmsnorm(x_ref[...], w1_ref[...]).astype(BF16)
    for n in range(0, IN_WIDTH, PROJ_COLS):
        proj_scr[:, n:n + PROJ_COLS] = _mm(h_scr[...], win_ref[:, n:n + PROJ_COLS])

    lb0, lb1 = lbp_ref[0:1, :], lbp_ref[1:2, :]
    lb_max = jnp.maximum(lb0, lb1)
    e0, e1 = jnp.exp(lb0 - lb_max), jnp.exp(lb1 - lb_max)
    lb = e0 / (e0 + e1)

    row_id = lax.broadcasted_iota(jnp.int32, (C, D_HEAD), 0)

    def col(group, h):
        return slice(group * GROUP_W + h * D_HEAD, group * GROUP_W + (h + 1) * D_HEAD)

    f = lb + (1.0 - lb) * _sigmoid(proj_scr[:, GROUP_W:2 * GROUP_W])
    k_scr[...] = 1.0 - f
    g = jnp.log(f)
    g_hi = g.astype(BF16)
    g_mid = (g - g_hi.astype(F32)).astype(BF16)
    g_lo = (g - g_hi.astype(F32) - g_mid.astype(F32)).astype(BF16)
    d_scr[...] = _mm(mstack_ref[...], g_hi) + _mm(mstack_ref[...], g_mid) + _mm(mstack_ref[...], g_lo)

    def block_rows(x, m, row):
        x3 = x.reshape(C // m, m, D_HEAD)
        return jnp.broadcast_to(x3[:, row:row + 1, :], x3.shape).reshape(C, D_HEAD)

    signs = [jnp.where((row_id & (1 << (lev - 1))) != 0, LOG2E, -LOG2E) for lev in range(3, nlev + 1)]

    for h in range(N_HEADS):
        hs = slice(h * D_HEAD, (h + 1) * D_HEAD)
        q = proj_scr[:, col(0, h)]
        k = k_scr[:, hs]
        v = proj_scr[:, col(2, h)].astype(BF16)
        b = d_scr[C:2 * C, hs]
        a = jnp.where(level_ref[...] == 0, _mm_nt(q.astype(BF16), k.astype(BF16)), 0.0)
        for lev in range(1, nlev + 1):
            upper = (row_id & (1 << (lev - 1))) != 0
            if lev == 1:
                z = jnp.where(upper, q * (1.0 - k), k)
            elif lev == 2:
                z = jnp.where(upper, q, k) * jnp.exp(d_scr[0:C, hs])
            else:
                m = 1 << lev
                z = jnp.where(upper, q, k) * jnp.exp2((b - block_rows(b, m, m // 2 - 1)) * signs[lev - 3])
            z = z.astype(BF16)
            a = jnp.where(level_ref[...] == lev, _mm_nt(z, z), a)
        o = _mm(a.astype(BF16), v)
        eb = jnp.exp(b)
        qe = q * eb
        kh = k * jnp.exp(block_rows(b, seq_rows, seq_rows - 1) - b)
        if sample:
            o_scr[:, col(0, h)] = o
            qe_scr[:, hs] = qe
            kh_scr[:, hs] = kh
            eb_scr[:, hs] = eb
        else:
            st = sa_scr[h]
            o = o + _mm_nt(qe.astype(BF16), st.astype(BF16))
            sa_scr[h] = st * eb[C - 1:C, :] + _mm_tn(v, kh.astype(BF16))
            gate = proj_scr[:, col(3, h)]
            o_scr[:, col(0, h)] = _rmsnorm(o, na_ref[...]) * (gate * _sigmoid(gate))

    cos, sin = cos_ref[...], sin_ref[...]
    for h in range(N_HEADS):
        hs = slice(h * D_HEAD, (h + 1) * D_HEAD)
        q = proj_scr[:, col(4, h)]
        k = proj_scr[:, col(5, h)]
        v = proj_scr[:, col(6, h)].astype(BF16)
        qr = q * cos + pltpu.roll(q, D_HEAD // 2, 1) * sin
        kr = (k * cos + pltpu.roll(k, D_HEAD // 2, 1) * sin) * (D_HEAD ** -0.5)
        a = _mm_nt(qr.astype(BF16), kr.astype(BF16)) * dec_ref[h]
        o = _mm(a.astype(BF16), v)
        qi = qr * inner_ref[h]
        ks = kr * sdec_ref[h]
        if sample:
            o_scr[:, col(1, h)] = o
            qi_scr[:, hs] = qi
            ks_scr[:, hs] = ks
        else:
            st = sb_scr[h]
            o = o + _mm(qi.astype(BF16), st.astype(BF16))
            sb_scr[h] = cdec[h] * st + _mm_tn(ks.astype(BF16), v)
            gate = proj_scr[:, col(7, h)]
            o_scr[:, col(1, h)] = _groupnorm(o, nb_ref[...]) * (gate * _sigmoid(gate))

    if sample:
        def seq_body(j, carry):
            rows = pl.ds(pl.multiple_of(j * SUBLANES, SUBLANES), SUBLANES)
            for h in range(N_HEADS):
                hs = slice(h * D_HEAD, (h + 1) * D_HEAD)
                st = sa_in_ref[j, h].T
                o_scr[rows, col(0, h)] += _mm_nt(qe_scr[rows, hs].astype(BF16), st.astype(BF16))
                v = proj_scr[rows, col(2, h)].astype(BF16)
                decay = eb_scr[rows, hs][SUBLANES - 1:SUBLANES, :]
                st = st * decay + _mm_tn(v, kh_scr[rows, hs].astype(BF16))
                sa_out_ref[j, h] = st.T
            for h in range(N_HEADS):
                hs = slice(h * D_HEAD, (h + 1) * D_HEAD)
                st = sb_in_ref[j, h]
                o_scr[rows, col(1, h)] += _mm(qi_scr[rows, hs].astype(BF16), st.astype(BF16))
                v = proj_scr[rows, col(6, h)].astype(BF16)
                sb_out_ref[j, h] = cdec[h] * st + _mm_tn(ks_scr[rows, hs].astype(BF16), v)
            return carry

        lax.fori_loop(0, C // SUBLANES, seq_body, 0)
        for h in range(N_HEADS):
            gate = proj_scr[:, col(3, h)]
            o_scr[:, col(0, h)] = _rmsnorm(o_scr[:, col(0, h)], na_ref[...]) * (gate * _sigmoid(gate))
            gate = proj_scr[:, col(7, h)]
            o_scr[:, col(1, h)] = _groupnorm(o_scr[:, col(1, h)], nb_ref[...]) * (gate * _sigmoid(gate))

    x1_ref[...] = x_ref[...] + _mm(o_scr[...].astype(BF16), wout_ref[...])

    if not sample:
        @pl.when(step == pl.num_programs(1) - 1)
        def _():
            for h in range(N_HEADS):
                sa_out_ref[0, h] = sa_scr[h].T
                sb_out_ref[0, h] = sb_scr[h]


def _full(shape):
    return pl.BlockSpec(shape, lambda *_: (0,) * len(shape))


def _mix_call(x2, cos, sin, w1, w_in, lbp, na, nb, w_out, states, *, n_seq, seq_len, tile):
    sample = states is not None
    n_rows = n_seq * seq_len
    assert n_rows % tile == 0
    nlev, cdec, consts = _chunk_consts(tile, min(seq_len, tile))
    state_shape = jax.ShapeDtypeStruct((n_seq, N_HEADS, D_HEAD, D_HEAD), F32)
    if sample:
        assert seq_len == SUBLANES
        seq_per_tile = tile // seq_len
        grid = (n_rows // tile,)
        row_map = lambda n: (n, 0)
        pos_map = lambda n: (0, 0)
        state_spec = pl.BlockSpec((seq_per_tile, N_HEADS, D_HEAD, D_HEAD), lambda n: (n, 0, 0, 0))
        state_in = [state_spec, state_spec]
        scratch = [pltpu.VMEM((tile, GROUP_W), F32)] * 5
    else:
        assert seq_len % tile == 0
        steps = seq_len // tile
        grid = (n_seq, steps)
        row_map = lambda b, i: (b * steps + i, 0)
        pos_map = lambda b, i: (i, 0)
        state_spec = pl.BlockSpec((1, N_HEADS, D_HEAD, D_HEAD), lambda b, i: (b, 0, 0, 0))
        state_in = []
        scratch = [pltpu.VMEM((N_HEADS, D_HEAD, D_HEAD), F32)] * 2
    const_specs = [_full(c.shape) for c in consts]
    in_specs = [pl.BlockSpec((tile, D_MODEL), row_map), pl.BlockSpec((tile, D_HEAD), pos_map),
                pl.BlockSpec((tile, D_HEAD), pos_map), _full(w1.shape), _full(w_in.shape), _full(lbp.shape),
                _full(na.shape), _full(nb.shape), _full(w_out.shape)] + const_specs + state_in
    args = [x2, cos, sin, w1, w_in, lbp, na, nb, w_out, *consts] + (list(states) if sample else [])
    return pl.pallas_call(
        functools.partial(_mix_kernel, sample=sample, tile=tile, seq_rows=min(seq_len, tile), nlev=nlev, cdec=cdec),
        grid=grid,
        in_specs=in_specs,
        out_specs=[pl.BlockSpec((tile, D_MODEL), row_map), state_spec, state_spec],
        out_shape=[jax.ShapeDtypeStruct((n_rows, D_MODEL), F32), state_shape, state_shape],
        scratch_shapes=[pltpu.VMEM((tile, D_MODEL), BF16), pltpu.VMEM((tile, IN_WIDTH), F32),
                        pltpu.VMEM((tile, 2 * GROUP_W), F32), pltpu.VMEM((2 * tile, GROUP_W), F32),
                        pltpu.VMEM((tile, GROUP_W), F32)] + scratch,
        compiler_params=pltpu.CompilerParams(dimension_semantics=("arbitrary",) * len(grid),
                                             vmem_limit_bytes=VMEM_LIMIT),
        name="mix_sample" if sample else "mix_prompt",
    )(*args)


def _ffn_kernel(*refs, sample, groups, rows):
    x_ref, w2_ref, wup_ref, cw_ref, cb_ref, wdown_ref, wf_ref = refs[:7]
    refs = refs[7:]
    if sample:
        hist_ref, y_ref, hist_out_ref, h_scr, act_scr = refs
    else:
        y_ref, hist_out_ref, h_scr, act_scr, tail_scr = refs
        step = pl.program_id(1)

        @pl.when(step == 0)
        def _():
            tail_scr[...] = jnp.zeros_like(tail_scr)

    G, L, P = groups, rows, SUBLANES
    row_id = lax.broadcasted_iota(jnp.int32, (G, P, FF_COLS), 1)

    def shifted(up, prev2, prev1):
        r1, r2 = pltpu.roll(up, 1, 1), pltpu.roll(up, 2, 1)
        top1 = jnp.where(row_id == 0, prev1, r1[:, :P])
        top2 = jnp.where(row_id == 0, prev2, jnp.where(row_id == 1, prev1, r2[:, :P]))
        if L == P:
            return top1, top2
        return jnp.concatenate([top1, r1[:, P:]], axis=1), jnp.concatenate([top2, r2[:, P:]], axis=1)

    h_scr[...] = _rmsnorm(x_ref[...], w2_ref[...]).astype(BF16)
    for n in range(0, D_FF, FF_COLS):
        conv = []
        for cols in (slice(n, n + FF_COLS), slice(D_FF + n, D_FF + n + FF_COLS)):
            up = _mm(h_scr[...], wup_ref[:, cols]).reshape(G, L, FF_COLS)
            if sample:
                prev2, prev1 = hist_ref[:, 0:1, cols], hist_ref[:, 1:2, cols]
                hist_out_ref[:, :, cols] = up[:, L - 2:, :]
            else:
                prev2, prev1 = tail_scr[:, P - 2:P - 1, cols], tail_scr[:, P - 1:P, cols]
                tail_scr[:, :, cols] = up[:, L - P:, :]
            sh1, sh2 = shifted(up, prev2, prev1)
            conv.append(cb_ref[:, cols] + cw_ref[0:1, cols] * sh2 + cw_ref[1:2, cols] * sh1 + cw_ref[2:3, cols] * up)
        u, g = conv
        act_scr[:, n:n + FF_COLS] = ((g * _sigmoid(g)) * u).reshape(G * L, FF_COLS).astype(BF16)

    x2 = x_ref[...] + _mm(act_scr[...], wdown_ref[...])
    y_ref[...] = _rmsnorm(x2, wf_ref[...])

    if not sample:
        @pl.when(step == pl.num_programs(1) - 1)
        def _():
            hist_out_ref[...] = tail_scr[:, P - 2:, :]


def _ffn_call(x1, w2, w_up, cw, cb, w_down, wf, hist, *, n_seq, seq_len, tile):
    sample = hist is not None
    n_rows = n_seq * seq_len
    hist_shape = jax.ShapeDtypeStruct((n_seq, CONV_W - 1, FF2), F32)
    if sample:
        groups, rows = tile // seq_len, seq_len
        grid = (n_rows // tile,)
        row_map = lambda n: (n, 0)
        hist_spec = pl.BlockSpec((groups, CONV_W - 1, FF2), lambda n: (n, 0, 0))
        hist_in = [hist_spec]
    else:
        assert seq_len % tile == 0
        groups, rows = 1, tile
        steps = seq_len // tile
        grid = (n_seq, steps)
        row_map = lambda b, i: (b * steps + i, 0)
        hist_spec = pl.BlockSpec((1, CONV_W - 1, FF2), lambda b, i: (b, 0, 0))
        hist_in = []
    in_specs = [pl.BlockSpec((tile, D_MODEL), row_map), _full(w2.shape), _full(w_up.shape), _full(cw.shape),
                _full(cb.shape), _full(w_down.shape), _full(wf.shape)] + hist_in
    args = [x1, w2, w_up, cw, cb, w_down, wf] + ([hist] if sample else [])
    return pl.pallas_call(
        functools.partial(_ffn_kernel, sample=sample, groups=groups, rows=rows),
        grid=grid,
        in_specs=in_specs,
        out_specs=[pl.BlockSpec((tile, D_MODEL), row_map), hist_spec],
        out_shape=[jax.ShapeDtypeStruct((n_rows, D_MODEL), F32), hist_shape],
        scratch_shapes=[pltpu.VMEM((tile, D_MODEL), BF16), pltpu.VMEM((tile, D_FF), BF16)]
        + ([] if sample else [pltpu.VMEM((1, SUBLANES, FF2), F32)]),
        compiler_params=pltpu.CompilerParams(dimension_semantics=("arbitrary",) * len(grid),
                                             vmem_limit_bytes=VMEM_LIMIT),
        name="ffn_sample" if sample else "ffn_prompt",
    )(*args)


def _trunk(x, pos, states, hist, params, *, mix_tile, ffn_tile):
    w1, w_in, lbp, na, nb, w_out, w2, w_up, cw, cb, w_down, wf = params
    n_seq, seq_len, _ = x.shape
    cos, sin = _rope_tables(pos)
    if states is not None:
        cos, sin = (jnp.tile(t, (mix_tile // seq_len, 1)) for t in (cos, sin))
    x2 = x.reshape(n_seq * seq_len, D_MODEL)
    x1, s_a, s_b = _mix_call(x2, cos, sin, w1, w_in, lbp, na, nb, w_out, states,
                             n_seq=n_seq, seq_len=seq_len, tile=mix_tile)
    y, s_c = _ffn_call(x1, w2, w_up, cw, cb, w_down, wf, hist, n_seq=n_seq, seq_len=seq_len, tile=ffn_tile)
    return y.reshape(x.shape), s_a[None], s_b[None], s_c[None]


def kernel(x_prompt, x_sample, state_hgrn, state_ret, state_conv, w_norm1, w_in, hgrn_lb, hgrn_norm_w, ret_norm_w,
           w_out, w_norm2, w_ffn_in, conv_w, conv_b, w_ffn_out, w_norm_f):
    assert w_in.shape == (1, D_MODEL, IN_WIDTH) and hgrn_lb.shape == (2, GROUP_W)
    params = (w_norm1, w_in[0].astype(BF16), hgrn_lb, hgrn_norm_w, ret_norm_w, w_out[0].astype(BF16),
              w_norm2, w_ffn_in[0].astype(BF16), conv_w[0], conv_b, w_ffn_out[0].astype(BF16),
              w_norm_f.reshape(1, D_MODEL))
    pos_p = jnp.arange(x_prompt.shape[1], dtype=jnp.int32)
    pos_s = PAST_LEN + jnp.arange(x_sample.shape[1], dtype=jnp.int32)
    y_p, ha_p, rb_p, cv_p = _trunk(x_prompt, pos_p, None, None, params, mix_tile=256, ffn_tile=512)
    y_s, ha_s, rb_s, cv_s = _trunk(x_sample, pos_s, (state_hgrn[0], state_ret[0]), state_conv[0], params,
                                   mix_tile=64, ffn_tile=128)
    return (y_p, y_s, ha_p, rb_p, cv_p, ha_s, rb_s, cv_s)
```

```python
import functools

import numpy as np
import jax
import jax.numpy as jnp
from jax import lax
from jax.experimental import pallas as pl
from jax.experimental.pallas import tpu as pltpu

F32 = jnp.float32
BF16 = jnp.bfloat16

D_MODEL = 1024
N_HEADS = 4
D_HEAD = 128
GROUP_W = N_HEADS * D_HEAD
IN_WIDTH = 8 * GROUP_W
D_FF = 2816
FF2 = 2 * D_FF
CONV_W = 3
PAST_LEN = 16384
ROPE_BASE = 10000.0
EPS = 1e-6
LOG2E = 1.4426950408889634

SUBLANES = 8
PROJ_COLS = 512
FF_COLS = 256
SAMPLE_SEQ_PER_STEP = 8
SEQ_UNROLL = 4
VMEM_LIMIT = 56 * 1024 * 1024


def _mm(a, b):
    return jnp.dot(a, b, preferred_element_type=F32)


def _mm_nt(a, b):
    return lax.dot_general(a, b, (((1,), (1,)), ((), ())), preferred_element_type=F32)


def _mm_tn(a, b):
    return lax.dot_general(a, b, (((0,), (0,)), ((), ())), preferred_element_type=F32)


def _sigmoid(x):
    return 1.0 / (1.0 + jnp.exp(-x))


def _rmsnorm(x, w):
    return x * lax.rsqrt(jnp.mean(x * x, axis=-1, keepdims=True) + EPS) * w


def _groupnorm(x, w):
    xc = x - jnp.mean(x, axis=-1, keepdims=True)
    return xc * lax.rsqrt(jnp.mean(xc * xc, axis=-1, keepdims=True) + EPS) * w


def _chunk_consts(chunk, seq_len):
    nlev = int(np.log2(seq_len))
    assert 1 << nlev == seq_len and chunk % seq_len == 0
    r = np.arange(chunk)
    rr, cc = r[:, None], r[None, :]
    same_seq = (rr // seq_len) == (cc // seq_len)
    m = 4
    mid = (r // m) * m + m // 2 - 1
    upper = ((r % m) >= m // 2)[:, None]
    lev2 = (upper & (cc > mid[:, None]) & (cc <= rr)) | (~upper & (cc > rr) & (cc <= mid[:, None]))
    cum = same_seq & (cc <= rr)
    mstack = np.concatenate([lev2, cum], axis=0).astype(np.float32)
    x = rr ^ cc
    bit_len = np.where(x > 0, np.floor(np.log2(np.maximum(x, 1))).astype(np.int64) + 1, 0)
    level = np.where(same_seq & (cc <= rr), bit_len, -1).astype(np.int32)

    pos = r % seq_len
    log_gamma = np.log1p(-np.exp2(-5.0 - np.arange(N_HEADS, dtype=np.float64)))[:, None, None]
    rel = (pos[:, None] - pos[None, :]).astype(np.float64)[None]
    causal = (same_seq & (cc <= rr))[None]
    dec = np.where(causal, np.exp(np.where(causal, rel, 0.0) * log_gamma), 0.0)
    ones = np.ones((1, 1, D_HEAD))
    inner = np.exp((pos + 1.0)[None, :, None] * log_gamma) * ones
    sdec = np.exp((seq_len - 1.0 - pos)[None, :, None] * log_gamma) * ones
    cdec = tuple(float(v) for v in np.exp(seq_len * log_gamma[:, 0, 0]))
    consts = (jnp.asarray(mstack, BF16), jnp.asarray(level), jnp.asarray(dec, F32), jnp.asarray(inner, F32),
              jnp.asarray(sdec, F32))
    return nlev, cdec, consts


def _rope_tables(pos):
    half = D_HEAD // 2
    inv = 1.0 / (ROPE_BASE ** (jnp.arange(half, dtype=F32) / half))
    ang = pos.astype(F32)[:, None] * inv[None, :]
    cos, sin = jnp.cos(ang), jnp.sin(ang)
    return jnp.concatenate([cos, cos], axis=-1), jnp.concatenate([-sin, sin], axis=-1)


def _mix_kernel(*refs, sample, tile, seq_rows, nlev, cdec, seq_per_step):
    (x_ref, cos_ref, sin_ref, w1_ref, win_ref, lbp_ref, na_ref, nb_ref, wout_ref,
     mstack_ref, level_ref, dec_ref, inner_ref, sdec_ref) = refs[:14]
    refs = refs[14:]
    if sample:
        sa_in_ref, sb_in_ref = refs[:2]
        refs = refs[2:]
    x1_ref, sa_out_ref, sb_out_ref, h_scr, proj_scr, o_scr, d_scr, k_scr = refs[:8]
    C = tile
    if sample:
        qe_scr, kh_scr, ex_scr, qi_scr, ks_scr = refs[8:]
        steps_per_tile = C // (seq_per_step * SUBLANES)
        sub = pl.program_id(0) % steps_per_tile
    else:
        sa_scr, sb_scr = refs[8:]
        step = pl.program_id(1)

        @pl.when(step == 0)
        def _():
            sa_scr[...] = jnp.zeros_like(sa_scr)
            sb_scr[...] = jnp.zeros_like(sb_scr)

    def col(group, h):
        return slice(group * GROUP_W + h * D_HEAD, group * GROUP_W + (h + 1) * D_HEAD)

    def head(h):
        return slice(h * D_HEAD, (h + 1) * D_HEAD)

    def block_rows(x, m, row):
        x3 = x.reshape(C // m, m, D_HEAD)
        return jnp.broadcast_to(x3[:, row:row + 1, :], x3.shape).reshape(C, D_HEAD)

    def score_tile():
        h_scr[...] = _rmsnorm(x_ref[...], w1_ref[...]).astype(BF16)
        for n in range(0, IN_WIDTH, PROJ_COLS):
            proj_scr[:, n:n + PROJ_COLS] = _mm(h_scr[...], win_ref[:, n:n + PROJ_COLS])

        lb0, lb1 = lbp_ref[0:1, :], lbp_ref[1:2, :]
        lb_max = jnp.maximum(lb0, lb1)
        e0, e1 = jnp.exp(lb0 - lb_max), jnp.exp(lb1 - lb_max)
        lb = e0 / (e0 + e1)

        row_id = lax.broadcasted_iota(jnp.int32, (C, D_HEAD), 0)

        f = lb + (1.0 - lb) * _sigmoid(proj_scr[:, GROUP_W:2 * GROUP_W])
        k_scr[...] = 1.0 - f
        g = jnp.log(f)
        g_hi = g.astype(BF16)
        g_mid = (g - g_hi.astype(F32)).astype(BF16)
        g_lo = (g - g_hi.astype(F32) - g_mid.astype(F32)).astype(BF16)
        d_scr[...] = _mm(mstack_ref[...], g_hi) + _mm(mstack_ref[...], g_mid) + _mm(mstack_ref[...], g_lo)

        signs = [jnp.where((row_id & (1 << (lev - 1))) != 0, LOG2E, -LOG2E) for lev in range(3, nlev + 1)]

        for h in range(N_HEADS):
            hs = head(h)
            q = proj_scr[:, col(0, h)]
            k = k_scr[:, hs]
            v = proj_scr[:, col(2, h)].astype(BF16)
            b = d_scr[C:2 * C, hs]
            a = jnp.where(level_ref[...] == 0, _mm_nt(q.astype(BF16), k.astype(BF16)), 0.0)
            for lev in range(1, nlev + 1):
                upper = (row_id & (1 << (lev - 1))) != 0
                if lev == 1:
                    z = jnp.where(upper, q * (1.0 - k), k)
                elif lev == 2:
                    z = jnp.where(upper, q, k) * jnp.exp(d_scr[0:C, hs])
                else:
                    m = 1 << lev
                    z = jnp.where(upper, q, k) * jnp.exp2((b - block_rows(b, m, m // 2 - 1)) * signs[lev - 3])
                z = z.astype(BF16)
                a = jnp.where(level_ref[...] == lev, _mm_nt(z, z), a)
            o = _mm(a.astype(BF16), v)
            eb = jnp.exp(b)
            qe = q * eb
            kh = k * jnp.exp(block_rows(b, seq_rows, seq_rows - 1) - b)
            if sample:
                o_scr[:, col(0, h)] = o
                qe_scr[:, hs] = qe
                kh_scr[:, hs] = kh
                e_all = block_rows(eb, seq_rows, seq_rows - 1)
                e_hi = e_all.astype(BF16).astype(F32)
                e_mid = (e_all - e_hi).astype(BF16).astype(F32)
                e_lo = e_all - e_hi - e_mid
                pos = row_id & (seq_rows - 1)
                ex_scr[:, hs] = jnp.where(pos == 0, e_hi, jnp.where(pos == 1, e_mid, jnp.where(pos == 2, e_lo, 0.0)))
            else:
                st = sa_scr[h]
                o = o + _mm_nt(qe.astype(BF16), st.astype(BF16))
                sa_scr[h] = st * eb[C - 1:C, :] + _mm_tn(v, kh.astype(BF16))
                gate = proj_scr[:, col(3, h)]
                o_scr[:, col(0, h)] = _rmsnorm(o, na_ref[...]) * (gate * _sigmoid(gate))

        cos, sin = cos_ref[...], sin_ref[...]
        for h in range(N_HEADS):
            hs = head(h)
            q = proj_scr[:, col(4, h)]
            k = proj_scr[:, col(5, h)]
            v = proj_scr[:, col(6, h)].astype(BF16)
            qr = q * cos + pltpu.roll(q, D_HEAD // 2, 1) * sin
            kr = (k * cos + pltpu.roll(k, D_HEAD // 2, 1) * sin) * (D_HEAD ** -0.5)
            a = _mm_nt(qr.astype(BF16), kr.astype(BF16)) * dec_ref[h]
            o = _mm(a.astype(BF16), v)
            qi = qr * inner_ref[h]
            ks = kr * sdec_ref[h]
            if sample:
                o_scr[:, col(1, h)] = o
                qi_scr[:, hs] = qi
                ks_scr[:, hs] = ks
            else:
                st = sb_scr[h]
                o = o + _mm(qi.astype(BF16), st.astype(BF16))
                sb_scr[h] = cdec[h] * st + _mm_tn(ks.astype(BF16), v)
                gate = proj_scr[:, col(7, h)]
                o_scr[:, col(1, h)] = _groupnorm(o, nb_ref[...]) * (gate * _sigmoid(gate))

    def apply_states():
        zeros8 = jnp.zeros((SUBLANES, D_HEAD), F32)
        sel8 = jnp.where(lax.broadcasted_iota(jnp.int32, (SUBLANES, D_HEAD), 0) < 3, 1.0, 0.0)

        def pair_readout(lhs_scr, st_ref, j, rows, group, h):
            lhs = jnp.concatenate([lhs_scr[rows, head(h)], lhs_scr[rows, head(h + 1)]], axis=0).astype(BF16)
            w = jnp.concatenate([st_ref[j, h], st_ref[j, h + 1]], axis=1).astype(BF16)
            oo = _mm(lhs, w)
            o_scr[rows, col(group, h)] += oo[:SUBLANES, :D_HEAD]
            o_scr[rows, col(group, h + 1)] += oo[SUBLANES:, D_HEAD:]

        def seq_body(j, carry):
            rows = pl.ds(pl.multiple_of(sub * (seq_per_step * SUBLANES) + j * SUBLANES, SUBLANES), SUBLANES)
            for h in range(0, N_HEADS, 2):
                pair_readout(qe_scr, sa_in_ref, j, rows, 0, h)
                pair_readout(qi_scr, sb_in_ref, j, rows, 1, h)
            for h in range(N_HEADS):
                hs = head(h)
                lhs = jnp.concatenate([kh_scr[rows, hs], ex_scr[rows, hs]], axis=0).astype(BF16)
                v = proj_scr[rows, col(2, h)]
                rhs = jnp.concatenate([jnp.concatenate([v, zeros8], axis=1),
                                       jnp.concatenate([zeros8, sel8], axis=1)], axis=0).astype(BF16)
                upd = _mm_tn(lhs, rhs)
                sa_out_ref[j, h] = sa_in_ref[j, h] * upd[:, D_HEAD:] + upd[:, :D_HEAD]
                v = proj_scr[rows, col(6, h)].astype(BF16)
                sb_out_ref[j, h] = cdec[h] * sb_in_ref[j, h] + _mm_tn(ks_scr[rows, hs].astype(BF16), v)
            return carry

        lax.fori_loop(0, seq_per_step, seq_body, 0, unroll=SEQ_UNROLL)

    def project_out():
        if sample:
            for h in range(N_HEADS):
                gate = proj_scr[:, col(3, h)]
                o_scr[:, col(0, h)] = _rmsnorm(o_scr[:, col(0, h)], na_ref[...]) * (gate * _sigmoid(gate))
                gate = proj_scr[:, col(7, h)]
                o_scr[:, col(1, h)] = _groupnorm(o_scr[:, col(1, h)], nb_ref[...]) * (gate * _sigmoid(gate))
        x1_ref[...] = x_ref[...] + _mm(o_scr[...].astype(BF16), wout_ref[...])

    if sample:
        pl.when(sub == 0)(score_tile)
        apply_states()
        pl.when(sub == steps_per_tile - 1)(project_out)
    else:
        score_tile()
        project_out()

        @pl.when(step == pl.num_programs(1) - 1)
        def _():
            for h in range(N_HEADS):
                sa_out_ref[0, h] = sa_scr[h].T
                sb_out_ref[0, h] = sb_scr[h]


def _full(shape):
    return pl.BlockSpec(shape, lambda *_: (0,) * len(shape))


def _mix_call(x2, cos, sin, w1, w_in, lbp, na, nb, w_out, states, *, n_seq, seq_len, tile):
    sample = states is not None
    n_rows = n_seq * seq_len
    assert n_rows % tile == 0
    nlev, cdec, consts = _chunk_consts(tile, min(seq_len, tile))
    state_shape = jax.ShapeDtypeStruct((n_seq, N_HEADS, D_HEAD, D_HEAD), F32)
    if sample:
        assert seq_len == SUBLANES and tile % (SAMPLE_SEQ_PER_STEP * seq_len) == 0
        steps_per_tile = tile // (SAMPLE_SEQ_PER_STEP * seq_len)
        grid = (n_seq // SAMPLE_SEQ_PER_STEP,)
        row_map = lambda n: (n // steps_per_tile, 0)
        pos_map = lambda n: (0, 0)
        state_spec = pl.BlockSpec((SAMPLE_SEQ_PER_STEP, N_HEADS, D_HEAD, D_HEAD), lambda n: (n, 0, 0, 0))
        state_in = [state_spec, state_spec]
        scratch = [pltpu.VMEM((tile, GROUP_W), F32)] * 5
    else:
        assert seq_len % tile == 0
        steps = seq_len // tile
        grid = (n_seq, steps)
        row_map = lambda b, i: (b * steps + i, 0)
        pos_map = lambda b, i: (i, 0)
        state_spec = pl.BlockSpec((1, N_HEADS, D_HEAD, D_HEAD), lambda b, i: (b, 0, 0, 0))
        state_in = []
        scratch = [pltpu.VMEM((N_HEADS, D_HEAD, D_HEAD), F32)] * 2
    const_specs = [_full(c.shape) for c in consts]
    in_specs = [pl.BlockSpec((tile, D_MODEL), row_map), pl.BlockSpec((tile, D_HEAD), pos_map),
                pl.BlockSpec((tile, D_HEAD), pos_map), _full(w1.shape), _full(w_in.shape), _full(lbp.shape),
                _full(na.shape), _full(nb.shape), _full(w_out.shape)] + const_specs + state_in
    args = [x2, cos, sin, w1, w_in, lbp, na, nb, w_out, *consts] + (list(states) if sample else [])
    return pl.pallas_call(
        functools.partial(_mix_kernel, sample=sample, tile=tile, seq_rows=min(seq_len, tile), nlev=nlev, cdec=cdec,
                          seq_per_step=SAMPLE_SEQ_PER_STEP),
        grid=grid,
        in_specs=in_specs,
        out_specs=[pl.BlockSpec((tile, D_MODEL), row_map), state_spec, state_spec],
        out_shape=[jax.ShapeDtypeStruct((n_rows, D_MODEL), F32), state_shape, state_shape],
        scratch_shapes=[pltpu.VMEM((tile, D_MODEL), BF16), pltpu.VMEM((tile, IN_WIDTH), F32),
                        pltpu.VMEM((tile, 2 * GROUP_W), F32), pltpu.VMEM((2 * tile, GROUP_W), F32),
                        pltpu.VMEM((tile, GROUP_W), F32)] + scratch,
        compiler_params=pltpu.CompilerParams(dimension_semantics=("arbitrary",) * len(grid),
                                             vmem_limit_bytes=VMEM_LIMIT),
        name="mix_sample" if sample else "mix_prompt",
    )(*args)


def _ffn_kernel(*refs, sample, groups, rows):
    x_ref, w2_ref, wup_ref, cw_ref, cb_ref, wdown_ref, wf_ref = refs[:7]
    refs = refs[7:]
    if sample:
        hist_ref, y_ref, hist_out_ref, h_scr, act_scr = refs
    else:
        y_ref, hist_out_ref, h_scr, act_scr, tail_scr = refs
        step = pl.program_id(1)

        @pl.when(step == 0)
        def _():
            tail_scr[...] = jnp.zeros_like(tail_scr)

    G, L, P = groups, rows, SUBLANES
    row_id = lax.broadcasted_iota(jnp.int32, (G, P, FF_COLS), 1)

    def shifted(up, prev2, prev1):
        r1, r2 = pltpu.roll(up, 1, 1), pltpu.roll(up, 2, 1)
        top1 = jnp.where(row_id == 0, prev1, r1[:, :P])
        top2 = jnp.where(row_id == 0, prev2, jnp.where(row_id == 1, prev1, r2[:, :P]))
        if L == P:
            return top1, top2
        return jnp.concatenate([top1, r1[:, P:]], axis=1), jnp.concatenate([top2, r2[:, P:]], axis=1)

    h_scr[...] = _rmsnorm(x_ref[...], w2_ref[...]).astype(BF16)
    for n in range(0, D_FF, FF_COLS):
        conv = []
        for cols in (slice(n, n + FF_COLS), slice(D_FF + n, D_FF + n + FF_COLS)):
            up = _mm(h_scr[...], wup_ref[:, cols]).reshape(G, L, FF_COLS)
            if sample:
                prev2, prev1 = hist_ref[:, 0:1, cols], hist_ref[:, 1:2, cols]
                hist_out_ref[:, :, cols] = up[:, L - 2:, :]
            else:
                prev2, prev1 = tail_scr[:, P - 2:P - 1, cols], tail_scr[:, P - 1:P, cols]
                tail_scr[:, :, cols] = up[:, L - P:, :]
            sh1, sh2 = shifted(up, prev2, prev1)
            conv.append(cb_ref[:, cols] + cw_ref[0:1, cols] * sh2 + cw_ref[1:2, cols] * sh1 + cw_ref[2:3, cols] * up)
        u, g = conv
        act_scr[:, n:n + FF_COLS] = ((g * _sigmoid(g)) * u).reshape(G * L, FF_COLS).astype(BF16)

    x2 = x_ref[...] + _mm(act_scr[...], wdown_ref[...])
    y_ref[...] = _rmsnorm(x2, wf_ref[...])

    if not sample:
        @pl.when(step == pl.num_programs(1) - 1)
        def _():
            hist_out_ref[...] = tail_scr[:, P - 2:, :]


def _ffn_call(x1, w2, w_up, cw, cb, w_down, wf, hist, *, n_seq, seq_len, tile):
    sample = hist is not None
    n_rows = n_seq * seq_len
    hist_shape = jax.ShapeDtypeStruct((n_seq, CONV_W - 1, FF2), F32)
    if sample:
        groups, rows = tile // seq_len, seq_len
        grid = (n_rows // tile,)
        row_map = lambda n: (n, 0)
        hist_spec = pl.BlockSpec((groups, CONV_W - 1, FF2), lambda n: (n, 0, 0))
        hist_in = [hist_spec]
    else:
        assert seq_len % tile == 0
        groups, rows = 1, tile
        steps = seq_len // tile
        grid = (n_seq, steps)
        row_map = lambda b, i: (b * steps + i, 0)
        hist_spec = pl.BlockSpec((1, CONV_W - 1, FF2), lambda b, i: (b, 0, 0))
        hist_in = []
    in_specs = [pl.BlockSpec((tile, D_MODEL), row_map), _full(w2.shape), _full(w_up.shape), _full(cw.shape),
                _full(cb.shape), _full(w_down.shape), _full(wf.shape)] + hist_in
    args = [x1, w2, w_up, cw, cb, w_down, wf] + ([hist] if sample else [])
    return pl.pallas_call(
        functools.partial(_ffn_kernel, sample=sample, groups=groups, rows=rows),
        grid=grid,
        in_specs=in_specs,
        out_specs=[pl.BlockSpec((tile, D_MODEL), row_map), hist_spec],
        out_shape=[jax.ShapeDtypeStruct((n_rows, D_MODEL), F32), hist_shape],
        scratch_shapes=[pltpu.VMEM((tile, D_MODEL), BF16), pltpu.VMEM((tile, D_FF), BF16)]
        + ([] if sample else [pltpu.VMEM((1, SUBLANES, FF2), F32)]),
        compiler_params=pltpu.CompilerParams(dimension_semantics=("arbitrary",) * len(grid),
                                             vmem_limit_bytes=VMEM_LIMIT),
        name="ffn_sample" if sample else "ffn_prompt",
    )(*args)


def _trunk(x, pos, states, hist, params, *, mix_tile, ffn_tile):
    w1, w_in, lbp, na, nb, w_out, w2, w_up, cw, cb, w_down, wf = params
    n_seq, seq_len, _ = x.shape
    cos, sin = _rope_tables(pos)
    if states is not None:
        cos, sin = (jnp.tile(t, (mix_tile // seq_len, 1)) for t in (cos, sin))
    x2 = x.reshape(n_seq * seq_len, D_MODEL)
    x1, s_a, s_b = _mix_call(x2, cos, sin, w1, w_in, lbp, na, nb, w_out, states,
                             n_seq=n_seq, seq_len=seq_len, tile=mix_tile)
    y, s_c = _ffn_call(x1, w2, w_up, cw, cb, w_down, wf, hist, n_seq=n_seq, seq_len=seq_len, tile=ffn_tile)
    return y.reshape(x.shape), s_a[None], s_b[None], s_c[None]


def kernel(x_prompt, x_sample, state_hgrn, state_ret, state_conv, w_norm1, w_in, hgrn_lb, hgrn_norm_w, ret_norm_w,
           w_out, w_norm2, w_ffn_in, conv_w, conv_b, w_ffn_out, w_norm_f):
    assert w_in.shape == (1, D_MODEL, IN_WIDTH) and hgrn_lb.shape == (2, GROUP_W)
    params = (w_norm1, w_in[0].astype(BF16), hgrn_lb, hgrn_norm_w, ret_norm_w, w_out[0].astype(BF16),
              w_norm2, w_ffn_in[0].astype(BF16), conv_w[0], conv_b, w_ffn_out[0].astype(BF16),
              w_norm_f.reshape(1, D_MODEL))
    pos_p = jnp.arange(x_prompt.shape[1], dtype=jnp.int32)
    pos_s = PAST_LEN + jnp.arange(x_sample.shape[1], dtype=jnp.int32)
    y_p, ha_p, rb_p, cv_p = _trunk(x_prompt, pos_p, None, None, params, mix_tile=256, ffn_tile=512)
    y_s, ha_s, rb_s, cv_s = _trunk(x_sample, pos_s, (state_hgrn[0], state_ret[0]), state_conv[0], params,
                                   mix_tile=256, ffn_tile=256)
    return (y_p, y_s, ha_p, rb_p, cv_p, ha_s, rb_s, cv_s)
```

```python
import functools

import numpy as np
import jax
import jax.numpy as jnp
from jax import lax
from jax.experimental import pallas as pl
from jax.experimental.pallas import tpu as pltpu

F32 = jnp.float32
BF16 = jnp.bfloat16

D_MODEL = 1024
N_HEADS = 4
D_HEAD = 128
GROUP_W = N_HEADS * D_HEAD
IN_WIDTH = 8 * GROUP_W
D_FF = 2816
FF2 = 2 * D_FF
CONV_W = 3
PAST_LEN = 16384
ROPE_BASE = 10000.0
EPS = 1e-6
LOG2E = 1.4426950408889634

SUBLANES = 8
PROJ_COLS = 512
FF_COLS = 256
ROW_PARTS = 2
SAMPLE_SEQ_PER_STEP = 8
SEQ_UNROLL = 4
VMEM_LIMIT = 56 * 1024 * 1024


def _mm(a, b):
    return jnp.dot(a, b, preferred_element_type=F32)


def _mm_nt(a, b):
    return lax.dot_general(a, b, (((1,), (1,)), ((), ())), preferred_element_type=F32)


def _mm_tn(a, b):
    return lax.dot_general(a, b, (((0,), (0,)), ((), ())), preferred_element_type=F32)


def _sigmoid(x):
    return 1.0 / (1.0 + jnp.exp(-x))


def _rmsnorm(x, w):
    return x * lax.rsqrt(jnp.mean(x * x, axis=-1, keepdims=True) + EPS) * w


def _groupnorm(x, w):
    xc = x - jnp.mean(x, axis=-1, keepdims=True)
    return xc * lax.rsqrt(jnp.mean(xc * xc, axis=-1, keepdims=True) + EPS) * w


def _chunk_consts(chunk, seq_len):
    nlev = int(np.log2(seq_len))
    assert 1 << nlev == seq_len and chunk % seq_len == 0
    r = np.arange(chunk)
    rr, cc = r[:, None], r[None, :]
    same_seq = (rr // seq_len) == (cc // seq_len)
    cum = (same_seq & (cc <= rr)).astype(np.float32)
    x = rr ^ cc
    bit_len = np.where(x > 0, np.floor(np.log2(np.maximum(x, 1))).astype(np.int64) + 1, 0)
    level = np.where(same_seq & (cc <= rr), bit_len, -1).astype(np.int32)

    pos = r % seq_len
    log_gamma = np.log1p(-np.exp2(-5.0 - np.arange(N_HEADS, dtype=np.float64)))[:, None, None]
    rel = (pos[:, None] - pos[None, :]).astype(np.float64)[None]
    causal = (same_seq & (cc <= rr))[None]
    dec = np.where(causal, np.exp(np.where(causal, rel, 0.0) * log_gamma), 0.0)
    ones = np.ones((1, 1, D_HEAD))
    inner = np.exp((pos + 1.0)[None, :, None] * log_gamma) * ones
    sdec = np.exp((seq_len - 1.0 - pos)[None, :, None] * log_gamma) * ones
    cdec = tuple(float(v) for v in np.exp(seq_len * log_gamma[:, 0, 0]))
    consts = (jnp.asarray(cum, BF16), jnp.asarray(level), jnp.asarray(dec, F32), jnp.asarray(inner, F32),
              jnp.asarray(sdec, F32))
    return nlev, cdec, consts


def _rope_tables(pos):
    half = D_HEAD // 2
    inv = 1.0 / (ROPE_BASE ** (jnp.arange(half, dtype=F32) / half))
    ang = pos.astype(F32)[:, None] * inv[None, :]
    cos, sin = jnp.cos(ang), jnp.sin(ang)
    return jnp.concatenate([cos, cos], axis=-1), jnp.concatenate([-sin, sin], axis=-1)


def _mix_kernel(*refs, sample, tile, chunks, seq_rows, nlev, cdec, seq_per_step):
    (x_ref, cos_ref, sin_ref, w1_ref, win_ref, lbp_ref, na_ref, nb_ref, wout_ref,
     cum_ref, level_ref, dec_ref, inner_ref, sdec_ref) = refs[:14]
    refs = refs[14:]
    if sample:
        sa_in_ref, sb_in_ref = refs[:2]
        refs = refs[2:]
    x1_ref, sa_out_ref, sb_out_ref, h_scr, proj_scr, o_scr, d_scr, k_scr = refs[:8]
    C = tile // chunks
    if sample:
        qe_scr, kh_scr, ex_scr, qi_scr, ks_scr = refs[8:]
        steps_per_tile = C // (seq_per_step * SUBLANES)
        sub = pl.program_id(0) % steps_per_tile
    else:
        sa_scr, sb_scr = refs[8:]
        step = pl.program_id(1)

        @pl.when(step == 0)
        def _():
            sa_scr[...] = jnp.zeros_like(sa_scr)
            sb_scr[...] = jnp.zeros_like(sb_scr)

    def col(group, h):
        return slice(group * GROUP_W + h * D_HEAD, group * GROUP_W + (h + 1) * D_HEAD)

    def head(h):
        return slice(h * D_HEAD, (h + 1) * D_HEAD)

    def block_rows(x, m, row):
        x3 = x.reshape(C // m, m, D_HEAD)
        return jnp.broadcast_to(x3[:, row:row + 1, :], x3.shape).reshape(C, D_HEAD)

    def upper_rows(x, m):
        return x.reshape(C // m, 2, m // 2, x.shape[-1])[:, 1].reshape(C // 2, x.shape[-1])

    def put_upper_rows(x, xu, m):
        x4 = x.reshape(C // m, 2, m // 2, x.shape[-1])
        xu4 = xu.reshape(C // m, 1, m // 2, x.shape[-1])
        return jnp.concatenate([x4[:, 0:1], xu4], axis=1).reshape(C, x.shape[-1])

    def project(rs):
        h_scr[rs, :] = _rmsnorm(x_ref[rs, :], w1_ref[...]).astype(BF16)
        for n in range(0, IN_WIDTH, PROJ_COLS):
            proj_scr[rs, n:n + PROJ_COLS] = _mm(h_scr[rs, :], win_ref[:, n:n + PROJ_COLS])

    def score(rs):
        lb0, lb1 = lbp_ref[0:1, :], lbp_ref[1:2, :]
        lb_max = jnp.maximum(lb0, lb1)
        e0, e1 = jnp.exp(lb0 - lb_max), jnp.exp(lb1 - lb_max)
        lb = e0 / (e0 + e1)

        row_id = lax.broadcasted_iota(jnp.int32, (C, D_HEAD), 0)

        f = lb + (1.0 - lb) * _sigmoid(proj_scr[rs, GROUP_W:2 * GROUP_W])
        k_scr[rs, :] = 1.0 - f
        g = jnp.log(f)
        g_hi = g.astype(BF16)
        g_mid = (g - g_hi.astype(F32)).astype(BF16)
        g_lo = (g - g_hi.astype(F32) - g_mid.astype(F32)).astype(BF16)
        d_scr[rs, :] = _mm(cum_ref[...], g_hi) + _mm(cum_ref[...], g_mid) + _mm(cum_ref[...], g_lo)

        signs = [jnp.where((row_id & (1 << (lev - 1))) != 0, LOG2E, -LOG2E) for lev in range(3, nlev + 1)]

        for h in range(N_HEADS):
            hs = head(h)
            q = proj_scr[rs, col(0, h)]
            k = k_scr[rs, hs]
            v = proj_scr[rs, col(2, h)].astype(BF16)
            b = d_scr[rs, hs]
            a = jnp.where(level_ref[...] == 0, _mm_nt(q.astype(BF16), k.astype(BF16)), 0.0)
            for lev in range(1, nlev + 1):
                m = 1 << lev
                upper = (row_id & (m // 2)) != 0
                if lev == 1:
                    z = jnp.where(upper, q * (1.0 - k), k)
                elif lev == 2:
                    fh = 1.0 - k
                    pos4 = row_id & 3
                    decay = jnp.where(pos4 == 0, pltpu.roll(fh, C - 1, 0),
                                      jnp.where(pos4 == 1, 1.0, jnp.where(pos4 == 2, fh, fh * pltpu.roll(fh, 1, 0))))
                    z = jnp.where(upper, q, k) * decay
                else:
                    z = jnp.where(upper, q, k) * jnp.exp2((b - block_rows(b, m, m // 2 - 1)) * signs[lev - 3])
                if m < 2 * SUBLANES:
                    z = z.astype(BF16)
                    a = jnp.where(level_ref[...] == lev, _mm_nt(z, z), a)
                else:
                    zq = upper_rows(z, m).astype(BF16)
                    zk = z.astype(BF16)
                    if m > D_HEAD:
                        p = [_mm_nt(zq[i * (m // 2):(i + 1) * (m // 2)], zk[i * m:i * m + m // 2])
                             for i in range(C // m)]
                        width = m // 2
                    else:
                        p = [_mm_nt(zq[i * (D_HEAD // 2):(i + 1) * (D_HEAD // 2)], zk[i * D_HEAD:(i + 1) * D_HEAD])
                             for i in range(C // D_HEAD)]
                        width = D_HEAD
                    full = jnp.concatenate([jnp.concatenate([pi] * (C // width), axis=1) for pi in p], axis=0)
                    lvl_u = upper_rows(level_ref[...], m)
                    a = put_upper_rows(a, jnp.where(lvl_u == lev, full, upper_rows(a, m)), m)
            o = _mm(a.astype(BF16), v)
            eb = jnp.exp(b)
            qe = q * eb
            kh = k * jnp.exp(block_rows(b, seq_rows, seq_rows - 1) - b)
            if sample:
                o_scr[rs, col(0, h)] = o
                qe_scr[rs, hs] = qe
                kh_scr[rs, hs] = kh
                e_all = block_rows(eb, seq_rows, seq_rows - 1)
                e_hi = e_all.astype(BF16).astype(F32)
                e_mid = (e_all - e_hi).astype(BF16).astype(F32)
                e_lo = e_all - e_hi - e_mid
                pos = row_id & (seq_rows - 1)
                ex_scr[rs, hs] = jnp.where(pos == 0, e_hi, jnp.where(pos == 1, e_mid, jnp.where(pos == 2, e_lo, 0.0)))
            else:
                st = sa_scr[h]
                o = o + _mm_nt(qe.astype(BF16), st.astype(BF16))
                sa_scr[h] = st * eb[C - 1:C, :] + _mm_tn(v, kh.astype(BF16))
                gate = proj_scr[rs, col(3, h)]
                o_scr[rs, col(0, h)] = _rmsnorm(o, na_ref[...]) * (gate * _sigmoid(gate))

        cos, sin = cos_ref[rs, :], sin_ref[rs, :]
        for h in range(N_HEADS):
            hs = head(h)
            q = proj_scr[rs, col(4, h)]
            k = proj_scr[rs, col(5, h)]
            v = proj_scr[rs, col(6, h)].astype(BF16)
            qr = q * cos + pltpu.roll(q, D_HEAD // 2, 1) * sin
            kr = (k * cos + pltpu.roll(k, D_HEAD // 2, 1) * sin) * (D_HEAD ** -0.5)
            a = _mm_nt(qr.astype(BF16), kr.astype(BF16)) * dec_ref[h]
            o = _mm(a.astype(BF16), v)
            qi = qr * inner_ref[h]
            ks = kr * sdec_ref[h]
            if sample:
                o_scr[rs, col(1, h)] = o
                qi_scr[rs, hs] = qi
                ks_scr[rs, hs] = ks
            else:
                st = sb_scr[h]
                o = o + _mm(qi.astype(BF16), st.astype(BF16))
                sb_scr[h] = cdec[h] * st + _mm_tn(ks.astype(BF16), v)
                gate = proj_scr[rs, col(7, h)]
                o_scr[rs, col(1, h)] = _groupnorm(o, nb_ref[...]) * (gate * _sigmoid(gate))

    def apply_states():
        zeros8 = jnp.zeros((SUBLANES, D_HEAD), F32)
        sel8 = jnp.where(lax.broadcasted_iota(jnp.int32, (SUBLANES, D_HEAD), 0) < 3, 1.0, 0.0)

        def pair_readout(lhs_scr, st_ref, j, rows, group, h):
            lhs = jnp.concatenate([lhs_scr[rows, head(h)], lhs_scr[rows, head(h + 1)]], axis=0).astype(BF16)
            w = jnp.concatenate([st_ref[j, h], st_ref[j, h + 1]], axis=1).astype(BF16)
            oo = _mm(lhs, w)
            o_scr[rows, col(group, h)] += oo[:SUBLANES, :D_HEAD]
            o_scr[rows, col(group, h + 1)] += oo[SUBLANES:, D_HEAD:]

        def seq_body(j, carry):
            rows = pl.ds(pl.multiple_of(sub * (seq_per_step * SUBLANES) + j * SUBLANES, SUBLANES), SUBLANES)
            for h in range(0, N_HEADS, 2):
                pair_readout(qe_scr, sa_in_ref, j, rows, 0, h)
                pair_readout(qi_scr, sb_in_ref, j, rows, 1, h)
            for h in range(N_HEADS):
                hs = head(h)
                lhs = jnp.concatenate([kh_scr[rows, hs], ex_scr[rows, hs]], axis=0).astype(BF16)
                v = proj_scr[rows, col(2, h)]
                rhs = jnp.concatenate([jnp.concatenate([v, zeros8], axis=1),
                                       jnp.concatenate([zeros8, sel8], axis=1)], axis=0).astype(BF16)
                upd = _mm_tn(lhs, rhs)
                sa_out_ref[j, h] = sa_in_ref[j, h] * upd[:, D_HEAD:] + upd[:, :D_HEAD]
                v = proj_scr[rows, col(6, h)].astype(BF16)
                sb_out_ref[j, h] = cdec[h] * sb_in_ref[j, h] + _mm_tn(ks_scr[rows, hs].astype(BF16), v)
            return carry

        lax.fori_loop(0, seq_per_step, seq_body, 0, unroll=SEQ_UNROLL)

    def project_out():
        if sample:
            for h in range(N_HEADS):
                gate = proj_scr[:, col(3, h)]
                o_scr[:, col(0, h)] = _rmsnorm(o_scr[:, col(0, h)], na_ref[...]) * (gate * _sigmoid(gate))
                gate = proj_scr[:, col(7, h)]
                o_scr[:, col(1, h)] = _groupnorm(o_scr[:, col(1, h)], nb_ref[...]) * (gate * _sigmoid(gate))
        x1_ref[...] = x_ref[...] + _mm(o_scr[...].astype(BF16), wout_ref[...])

    chunk_rows = [slice(c * C, (c + 1) * C) for c in range(chunks)]
    if sample:
        @pl.when(sub == 0)
        def _():
            project(chunk_rows[0])
            score(chunk_rows[0])

        apply_states()
        pl.when(sub == steps_per_tile - 1)(project_out)
    else:
        for rs in chunk_rows:
            project(rs)
        for rs in chunk_rows:
            score(rs)
        project_out()

        @pl.when(step == pl.num_programs(1) - 1)
        def _():
            for h in range(N_HEADS):
                sa_out_ref[0, h] = sa_scr[h].T
                sb_out_ref[0, h] = sb_scr[h]


def _full(shape):
    return pl.BlockSpec(shape, lambda *_: (0,) * len(shape))


def _mix_call(x2, cos, sin, w1, w_in, lbp, na, nb, w_out, states, *, n_seq, seq_len, tile, chunks):
    sample = states is not None
    n_rows = n_seq * seq_len
    assert n_rows % tile == 0 and tile % chunks == 0
    chunk = tile // chunks
    nlev, cdec, consts = _chunk_consts(chunk, min(seq_len, chunk))
    state_shape = jax.ShapeDtypeStruct((n_seq, N_HEADS, D_HEAD, D_HEAD), F32)
    if sample:
        assert seq_len == SUBLANES and chunks == 1 and tile % (SAMPLE_SEQ_PER_STEP * seq_len) == 0
        steps_per_tile = tile // (SAMPLE_SEQ_PER_STEP * seq_len)
        grid = (n_seq // SAMPLE_SEQ_PER_STEP,)
        row_map = lambda n: (n // steps_per_tile, 0)
        pos_map = lambda n: (0, 0)
        state_spec = pl.BlockSpec((SAMPLE_SEQ_PER_STEP, N_HEADS, D_HEAD, D_HEAD), lambda n: (n, 0, 0, 0))
        state_in = [state_spec, state_spec]
        scratch = [pltpu.VMEM((tile, GROUP_W), F32)] * 5
    else:
        assert seq_len % tile == 0
        steps = seq_len // tile
        grid = (n_seq, steps)
        row_map = lambda b, i: (b * steps + i, 0)
        pos_map = lambda b, i: (i, 0)
        state_spec = pl.BlockSpec((1, N_HEADS, D_HEAD, D_HEAD), lambda b, i: (b, 0, 0, 0))
        state_in = []
        scratch = [pltpu.VMEM((N_HEADS, D_HEAD, D_HEAD), F32)] * 2
    const_specs = [_full(c.shape) for c in consts]
    in_specs = [pl.BlockSpec((tile, D_MODEL), row_map), pl.BlockSpec((tile, D_HEAD), pos_map),
                pl.BlockSpec((tile, D_HEAD), pos_map), _full(w1.shape), _full(w_in.shape), _full(lbp.shape),
                _full(na.shape), _full(nb.shape), _full(w_out.shape)] + const_specs + state_in
    args = [x2, cos, sin, w1, w_in, lbp, na, nb, w_out, *consts] + (list(states) if sample else [])
    return pl.pallas_call(
        functools.partial(_mix_kernel, sample=sample, tile=tile, chunks=chunks, seq_rows=min(seq_len, chunk),
                          nlev=nlev, cdec=cdec, seq_per_step=SAMPLE_SEQ_PER_STEP),
        grid=grid,
        in_specs=in_specs,
        out_specs=[pl.BlockSpec((tile, D_MODEL), row_map), state_spec, state_spec],
        out_shape=[jax.ShapeDtypeStruct((n_rows, D_MODEL), F32), state_shape, state_shape],
        scratch_shapes=[pltpu.VMEM((tile, D_MODEL), BF16), pltpu.VMEM((tile, IN_WIDTH), F32),
                        pltpu.VMEM((tile, 2 * GROUP_W), F32), pltpu.VMEM((tile, GROUP_W), F32),
                        pltpu.VMEM((tile, GROUP_W), F32)] + scratch,
        compiler_params=pltpu.CompilerParams(dimension_semantics=("arbitrary",) * len(grid),
                                             vmem_limit_bytes=VMEM_LIMIT),
        name="mix_sample" if sample else "mix_prompt",
    )(*args)


def _ffn_kernel(*refs, sample, groups, rows):
    x_ref, w2_ref, wup_ref, cw_ref, cb_ref, wdown_ref, wf_ref = refs[:7]
    refs = refs[7:]
    if sample:
        hist_ref, y_ref, hist_out_ref, h_scr, act_scr = refs
    else:
        y_ref, hist_out_ref, h_scr, act_scr, tail_scr = refs
        step = pl.program_id(1)

        @pl.when(step == 0)
        def _():
            tail_scr[...] = jnp.zeros_like(tail_scr)

    G, L, P = groups, rows, SUBLANES
    parts = 1 if sample else ROW_PARTS
    LP = L // parts
    row_id = lax.broadcasted_iota(jnp.int32, (G, P, FF_COLS), 1)

    def shifted(up, prev2, prev1):
        r1, r2 = pltpu.roll(up, 1, 1), pltpu.roll(up, 2, 1)
        top1 = jnp.where(row_id == 0, prev1, r1[:, :P])
        top2 = jnp.where(row_id == 0, prev2, jnp.where(row_id == 1, prev1, r2[:, :P]))
        if LP == P:
            return top1, top2
        return jnp.concatenate([top1, r1[:, P:]], axis=1), jnp.concatenate([top2, r2[:, P:]], axis=1)

    tails = {}
    for part in range(parts):
        rows_p = slice(part * G * LP, (part + 1) * G * LP)
        h_scr[rows_p, :] = _rmsnorm(x_ref[rows_p, :], w2_ref[...]).astype(BF16)
        for n in range(0, D_FF, FF_COLS):
            conv = []
            for cols in (slice(n, n + FF_COLS), slice(D_FF + n, D_FF + n + FF_COLS)):
                up = _mm(h_scr[rows_p, :], wup_ref[:, cols]).reshape(G, LP, FF_COLS)
                if sample:
                    prev2, prev1 = hist_ref[:, 0:1, cols], hist_ref[:, 1:2, cols]
                    hist_out_ref[:, :, cols] = up[:, LP - 2:, :]
                else:
                    if part == 0:
                        prev2, prev1 = tail_scr[:, P - 2:P - 1, cols], tail_scr[:, P - 1:P, cols]
                    else:
                        prev = tails[cols.start]
                        prev2, prev1 = prev[:, P - 2:P - 1, :], prev[:, P - 1:P, :]
                    tails[cols.start] = up[:, LP - P:, :]
                    if part == parts - 1:
                        tail_scr[:, :, cols] = up[:, LP - P:, :]
                sh1, sh2 = shifted(up, prev2, prev1)
                conv.append(cb_ref[:, cols] + cw_ref[0:1, cols] * sh2 + cw_ref[1:2, cols] * sh1 + cw_ref[2:3, cols] * up)
            u, g = (c.astype(BF16) for c in conv)
            one = jnp.ones((), BF16)
            act_scr[rows_p, n:n + FF_COLS] = ((g * (one / (one + jnp.exp(-g)))) * u).reshape(G * LP, FF_COLS)

        x2 = x_ref[rows_p, :] + _mm(act_scr[rows_p, :], wdown_ref[...])
        y_ref[rows_p, :] = _rmsnorm(x2, wf_ref[...])

    if not sample:
        @pl.when(step == pl.num_programs(1) - 1)
        def _():
            hist_out_ref[...] = tail_scr[:, P - 2:, :]


def _ffn_call(x1, w2, w_up, cw, cb, w_down, wf, hist, *, n_seq, seq_len, tile):
    sample = hist is not None
    n_rows = n_seq * seq_len
    hist_shape = jax.ShapeDtypeStruct((n_seq, CONV_W - 1, FF2), F32)
    if sample:
        groups, rows = tile // seq_len, seq_len
        grid = (n_rows // tile,)
        row_map = lambda n: (n, 0)
        hist_spec = pl.BlockSpec((groups, CONV_W - 1, FF2), lambda n: (n, 0, 0))
        hist_in = [hist_spec]
    else:
        assert seq_len % tile == 0
        groups, rows = 1, tile
        steps = seq_len // tile
        grid = (n_seq, steps)
        row_map = lambda b, i: (b * steps + i, 0)
        hist_spec = pl.BlockSpec((1, CONV_W - 1, FF2), lambda b, i: (b, 0, 0))
        hist_in = []
    in_specs = [pl.BlockSpec((tile, D_MODEL), row_map), _full(w2.shape), _full(w_up.shape), _full(cw.shape),
                _full(cb.shape), _full(w_down.shape), _full(wf.shape)] + hist_in
    args = [x1, w2, w_up, cw, cb, w_down, wf] + ([hist] if sample else [])
    return pl.pallas_call(
        functools.partial(_ffn_kernel, sample=sample, groups=groups, rows=rows),
        grid=grid,
        in_specs=in_specs,
        out_specs=[pl.BlockSpec((tile, D_MODEL), row_map), hist_spec],
        out_shape=[jax.ShapeDtypeStruct((n_rows, D_MODEL), F32), hist_shape],
        scratch_shapes=[pltpu.VMEM((tile, D_MODEL), BF16), pltpu.VMEM((tile, D_FF), BF16)]
        + ([] if sample else [pltpu.VMEM((1, SUBLANES, FF2), F32)]),
        compiler_params=pltpu.CompilerParams(dimension_semantics=("arbitrary",) * len(grid),
                                             vmem_limit_bytes=VMEM_LIMIT),
        name="ffn_sample" if sample else "ffn_prompt",
    )(*args)


def _trunk(x, pos, states, hist, params, *, mix_tile, mix_chunks, ffn_tile):
    w1, w_in, lbp, na, nb, w_out, w2, w_up, cw, cb, w_down, wf = params
    n_seq, seq_len, _ = x.shape
    cos, sin = _rope_tables(pos)
    if states is not None:
        cos, sin = (jnp.tile(t, (mix_tile // seq_len, 1)) for t in (cos, sin))
    x2 = x.reshape(n_seq * seq_len, D_MODEL)
    x1, s_a, s_b = _mix_call(x2, cos, sin, w1, w_in, lbp, na, nb, w_out, states,
                             n_seq=n_seq, seq_len=seq_len, tile=mix_tile, chunks=mix_chunks)
    y, s_c = _ffn_call(x1, w2, w_up, cw, cb, w_down, wf, hist, n_seq=n_seq, seq_len=seq_len, tile=ffn_tile)
    return y.reshape(x.shape), s_a[None], s_b[None], s_c[None]


def kernel(x_prompt, x_sample, state_hgrn, state_ret, state_conv, w_norm1, w_in, hgrn_lb, hgrn_norm_w, ret_norm_w,
           w_out, w_norm2, w_ffn_in, conv_w, conv_b, w_ffn_out, w_norm_f):
    assert w_in.shape == (1, D_MODEL, IN_WIDTH) and hgrn_lb.shape == (2, GROUP_W)
    params = (w_norm1, w_in[0].astype(BF16), hgrn_lb, hgrn_norm_w, ret_norm_w, w_out[0].astype(BF16),
              w_norm2, w_ffn_in[0].astype(BF16), conv_w[0], conv_b, w_ffn_out[0].astype(BF16),
              w_norm_f.reshape(1, D_MODEL))
    pos_p = jnp.arange(x_prompt.shape[1], dtype=jnp.int32)
    pos_s = PAST_LEN + jnp.arange(x_sample.shape[1], dtype=jnp.int32)
    y_p, ha_p, rb_p, cv_p = _trunk(x_prompt, pos_p, None, None, params, mix_tile=512, mix_chunks=2, ffn_tile=512)
    y_s, ha_s, rb_s, cv_s = _trunk(x_sample, pos_s, (state_hgrn[0], state_ret[0]), state_conv[0], params,
                                   mix_tile=256, mix_chunks=1, ffn_tile=256)
    return (y_p, y_s, ha_p, rb_p, cv_p, ha_s, rb_s, cv_s)
```

```python
import functools

import numpy as np
import jax
import jax.numpy as jnp
from jax import lax
from jax.experimental import pallas as pl
from jax.experimental.pallas import tpu as pltpu

F32 = jnp.float32
BF16 = jnp.bfloat16

D_MODEL = 1024
N_HEADS = 4
D_HEAD = 128
GROUP_W = N_HEADS * D_HEAD
IN_WIDTH = 8 * GROUP_W
D_FF = 2816
FF2 = 2 * D_FF
CONV_W = 3
PAST_LEN = 16384
ROPE_BASE = 10000.0
EPS = 1e-6
LOG2E = 1.4426950408889634

SUBLANES = 8
BF16_ROWS = 16
PROJ_COLS = 512
FF_COLS = 256
ROW_PARTS = 2
SAMPLE_SEQ_PER_STEP = 8
SEQ_UNROLL = 4
VMEM_LIMIT = 56 * 1024 * 1024


def _mm(a, b):
    return jnp.dot(a, b, preferred_element_type=F32)


def _mm_nt(a, b):
    return lax.dot_general(a, b, (((1,), (1,)), ((), ())), preferred_element_type=F32)


def _mm_tn(a, b):
    return lax.dot_general(a, b, (((0,), (0,)), ((), ())), preferred_element_type=F32)


def _sigmoid(x):
    return 1.0 / (1.0 + jnp.exp(-x))


def _rmsnorm(x, w):
    return x * lax.rsqrt(jnp.mean(x * x, axis=-1, keepdims=True) + EPS) * w


def _groupnorm(x, w):
    xc = x - jnp.mean(x, axis=-1, keepdims=True)
    return xc * lax.rsqrt(jnp.mean(xc * xc, axis=-1, keepdims=True) + EPS) * w


def _chunk_consts(chunk, seq_len):
    nlev = int(np.log2(seq_len))
    assert 1 << nlev == seq_len and chunk % seq_len == 0
    r = np.arange(chunk)
    rr, cc = r[:, None], r[None, :]
    same_seq = (rr // seq_len) == (cc // seq_len)
    cum = (same_seq & (cc <= rr)).astype(np.float32)
    x = rr ^ cc
    bit_len = np.where(x > 0, np.floor(np.log2(np.maximum(x, 1))).astype(np.int64) + 1, 0)
    level = np.where(same_seq & (cc <= rr), bit_len, -1).astype(np.int32)

    pos = r % seq_len
    log_gamma = np.log1p(-np.exp2(-5.0 - np.arange(N_HEADS, dtype=np.float64)))[:, None, None]
    rel = (pos[:, None] - pos[None, :]).astype(np.float64)[None]
    causal = (same_seq & (cc <= rr))[None]
    dec = np.where(causal, np.exp(np.where(causal, rel, 0.0) * log_gamma), 0.0)
    ones = np.ones((1, 1, D_HEAD))
    inner = np.exp((pos + 1.0)[None, :, None] * log_gamma) * ones
    sdec = np.exp((seq_len - 1.0 - pos)[None, :, None] * log_gamma) * ones
    cdec = tuple(float(v) for v in np.exp(seq_len * log_gamma[:, 0, 0]))
    consts = (jnp.asarray(cum, BF16), jnp.asarray(level), jnp.asarray(dec, F32), jnp.asarray(inner, F32),
              jnp.asarray(sdec, F32))
    return nlev, cdec, consts


def _rope_tables(pos):
    half = D_HEAD // 2
    inv = 1.0 / (ROPE_BASE ** (np.arange(half, dtype=np.float64) / half))
    ang = np.asarray(pos, np.float64)[:, None] * inv[None, :]
    cos, sin = np.cos(ang), np.sin(ang)
    return (jnp.asarray(np.concatenate([cos, cos], axis=-1), F32),
            jnp.asarray(np.concatenate([-sin, sin], axis=-1), F32))


def _mix_kernel(*refs, sample, tile, chunks, seq_rows, nlev, cdec, seq_per_step):
    (x_ref, cos_ref, sin_ref, w1_ref, win_ref, lbp_ref, na_ref, nb_ref, wout_ref,
     cum_ref, level_ref, dec_ref, inner_ref, sdec_ref) = refs[:14]
    refs = refs[14:]
    if sample:
        sa_in_ref, sb_in_ref = refs[:2]
        x1_ref, sa_out_ref, sb_out_ref = refs[2:5]
    else:
        wide_refs = refs[:2]
        x1_ref, sa_out_ref, sb_out_ref = refs[2:5]
        narrow_refs = refs[5:7]
        refs = refs[2:]
    refs = refs[5:]
    h_scr, proj_scr, o_scr, d_scr, k_scr = refs[:5]
    refs = refs[5:]
    C = tile // chunks
    if sample:
        qe_scr, kh_scr, ex_scr, qi_scr, ks_scr = refs
        steps_per_tile = C // (seq_per_step * SUBLANES)
        sub = pl.program_id(0) % steps_per_tile
    else:
        sa_scr, sb_scr = refs
        step = pl.program_id(1)

        @pl.when(step == 0)
        def _():
            sa_scr[...] = jnp.zeros_like(sa_scr)
            sb_scr[...] = jnp.zeros_like(sb_scr)

    def col(group, h):
        return slice(group * GROUP_W + h * D_HEAD, group * GROUP_W + (h + 1) * D_HEAD)

    def head(h):
        return slice(h * D_HEAD, (h + 1) * D_HEAD)

    def block_rows(x, m, row):
        x3 = x.reshape(C // m, m, D_HEAD)
        return jnp.broadcast_to(x3[:, row:row + 1, :], x3.shape).reshape(C, D_HEAD)

    def upper_rows(x, m):
        return x.reshape(C // m, 2, m // 2, x.shape[-1])[:, 1].reshape(C // 2, x.shape[-1])

    def put_upper_rows(x, xu, m):
        x4 = x.reshape(C // m, 2, m // 2, x.shape[-1])
        xu4 = xu.reshape(C // m, 1, m // 2, x.shape[-1])
        return jnp.concatenate([x4[:, 0:1], xu4], axis=1).reshape(C, x.shape[-1])

    def project(rs):
        h_scr[rs, :] = _rmsnorm(x_ref[rs, :], w1_ref[...]).astype(BF16)
        for n in range(0, IN_WIDTH, PROJ_COLS):
            proj_scr[rs, n:n + PROJ_COLS] = _mm(h_scr[rs, :], win_ref[:, n:n + PROJ_COLS])

    def score(rs):
        lb0, lb1 = lbp_ref[0:1, :], lbp_ref[1:2, :]
        lb_max = jnp.maximum(lb0, lb1)
        e0, e1 = jnp.exp(lb0 - lb_max), jnp.exp(lb1 - lb_max)
        lb = e0 / (e0 + e1)

        row_id = lax.broadcasted_iota(jnp.int32, (C, D_HEAD), 0)

        f = lb + (1.0 - lb) * _sigmoid(proj_scr[rs, GROUP_W:2 * GROUP_W])
        k_scr[rs, :] = 1.0 - f
        g = jnp.log(f)
        g_hi = g.astype(BF16)
        g_mid = (g - g_hi.astype(F32)).astype(BF16)
        g_lo = (g - g_hi.astype(F32) - g_mid.astype(F32)).astype(BF16)
        d_scr[rs, :] = _mm(cum_ref[...], g_hi) + _mm(cum_ref[...], g_mid) + _mm(cum_ref[...], g_lo)

        signs = [jnp.where((row_id & (1 << (lev - 1))) != 0, LOG2E, -LOG2E) for lev in range(3, nlev + 1)]

        for h in range(N_HEADS):
            hs = head(h)
            q = proj_scr[rs, col(0, h)]
            k = k_scr[rs, hs]
            v = proj_scr[rs, col(2, h)].astype(BF16)
            b = d_scr[rs, hs]
            a = jnp.where(level_ref[...] == 0, _mm_nt(q.astype(BF16), k.astype(BF16)), 0.0)
            for lev in range(1, nlev + 1):
                m = 1 << lev
                upper = (row_id & (m // 2)) != 0
                if lev == 1:
                    z = jnp.where(upper, q * (1.0 - k), k)
                elif lev == 2:
                    fh = 1.0 - k
                    pos4 = row_id & 3
                    decay = jnp.where(pos4 == 0, pltpu.roll(fh, C - 1, 0),
                                      jnp.where(pos4 == 1, 1.0, jnp.where(pos4 == 2, fh, fh * pltpu.roll(fh, 1, 0))))
                    z = jnp.where(upper, q, k) * decay
                else:
                    z = jnp.where(upper, q, k) * jnp.exp2((b - block_rows(b, m, m // 2 - 1)) * signs[lev - 3])
                if m < 2 * SUBLANES:
                    z = z.astype(BF16)
                    a = jnp.where(level_ref[...] == lev, _mm_nt(z, z), a)
                else:
                    zq = upper_rows(z, m).astype(BF16)
                    zk = z.astype(BF16)
                    if m > D_HEAD:
                        p = [_mm_nt(zq[i * (m // 2):(i + 1) * (m // 2)], zk[i * m:i * m + m // 2])
                             for i in range(C // m)]
                        width = m // 2
                    else:
                        p = [_mm_nt(zq[i * (D_HEAD // 2):(i + 1) * (D_HEAD // 2)], zk[i * D_HEAD:(i + 1) * D_HEAD])
                             for i in range(C // D_HEAD)]
                        width = D_HEAD
                    full = jnp.concatenate([jnp.concatenate([pi] * (C // width), axis=1) for pi in p], axis=0)
                    lvl_u = upper_rows(level_ref[...], m)
                    a = put_upper_rows(a, jnp.where(lvl_u == lev, full, upper_rows(a, m)), m)
            o = _mm(a.astype(BF16), v)
            eb = jnp.exp(b)
            qe = q * eb
            kh = k * jnp.exp(block_rows(b, seq_rows, seq_rows - 1) - b)
            if sample:
                o_scr[rs, col(0, h)] = o
                qe_scr[rs, hs] = qe
                kh_scr[rs, hs] = kh
                e_all = block_rows(eb, seq_rows, seq_rows - 1)
                e_hi = e_all.astype(BF16).astype(F32)
                e_mid = (e_all - e_hi).astype(BF16).astype(F32)
                e_lo = e_all - e_hi - e_mid
                pos = row_id & (seq_rows - 1)
                ex_scr[rs, hs] = jnp.where(pos == 0, e_hi, jnp.where(pos == 1, e_mid, jnp.where(pos == 2, e_lo, 0.0)))
            else:
                st = sa_scr[h]
                o = o + _mm_nt(qe.astype(BF16), st.astype(BF16))
                sa_scr[h] = st * eb[C - 1:C, :] + _mm_tn(v, kh.astype(BF16))
                gate = proj_scr[rs, col(3, h)]
                o_scr[rs, col(0, h)] = _rmsnorm(o, na_ref[...]) * (gate * _sigmoid(gate))

        cos, sin = cos_ref[rs, :], sin_ref[rs, :]
        for h in range(N_HEADS):
            hs = head(h)
            q = proj_scr[rs, col(4, h)]
            k = proj_scr[rs, col(5, h)]
            v = proj_scr[rs, col(6, h)].astype(BF16)
            qr = q * cos + pltpu.roll(q, D_HEAD // 2, 1) * sin
            kr = (k * cos + pltpu.roll(k, D_HEAD // 2, 1) * sin) * (D_HEAD ** -0.5)
            a = _mm_nt(qr.astype(BF16), kr.astype(BF16)) * dec_ref[h]
            o = _mm(a.astype(BF16), v)
            qi = qr * inner_ref[h]
            ks = kr * sdec_ref[h]
            if sample:
                o_scr[rs, col(1, h)] = o
                qi_scr[rs, hs] = qi
                ks_scr[rs, hs] = ks
            else:
                st = sb_scr[h]
                o = o + _mm(qi.astype(BF16), st.astype(BF16))
                sb_scr[h] = cdec[h] * st + _mm_tn(ks.astype(BF16), v)
                gate = proj_scr[rs, col(7, h)]
                o_scr[rs, col(1, h)] = _groupnorm(o, nb_ref[...]) * (gate * _sigmoid(gate))

    def apply_states():
        zeros8 = jnp.zeros((SUBLANES, D_HEAD), F32)
        sel8 = jnp.where(lax.broadcasted_iota(jnp.int32, (SUBLANES, D_HEAD), 0) < 3, 1.0, 0.0)

        def pair_readout(lhs_scr, st_ref, j, rows, group, h):
            lhs = jnp.concatenate([lhs_scr[rows, head(h)], lhs_scr[rows, head(h + 1)]], axis=0).astype(BF16)
            w = jnp.concatenate([st_ref[j, h], st_ref[j, h + 1]], axis=1).astype(BF16)
            oo = _mm(lhs, w)
            o_scr[rows, col(group, h)] += oo[:SUBLANES, :D_HEAD]
            o_scr[rows, col(group, h + 1)] += oo[SUBLANES:, D_HEAD:]

        def seq_body(j, carry):
            rows = pl.ds(pl.multiple_of(sub * (seq_per_step * SUBLANES) + j * SUBLANES, SUBLANES), SUBLANES)
            for h in range(0, N_HEADS, 2):
                pair_readout(qe_scr, sa_in_ref, j, rows, 0, h)
                pair_readout(qi_scr, sb_in_ref, j, rows, 1, h)
            for h in range(N_HEADS):
                hs = head(h)
                lhs = jnp.concatenate([kh_scr[rows, hs], ex_scr[rows, hs]], axis=0).astype(BF16)
                v = proj_scr[rows, col(2, h)]
                rhs = jnp.concatenate([jnp.concatenate([v, zeros8], axis=1),
                                       jnp.concatenate([zeros8, sel8], axis=1)], axis=0).astype(BF16)
                upd = _mm_tn(lhs, rhs)
                sa_out_ref[j, h] = sa_in_ref[j, h] * upd[:, D_HEAD:] + upd[:, :D_HEAD]
                v = proj_scr[rows, col(6, h)].astype(BF16)
                sb_out_ref[j, h] = cdec[h] * sb_in_ref[j, h] + _mm_tn(ks_scr[rows, hs].astype(BF16), v)
            return carry

        lax.fori_loop(0, seq_per_step, seq_body, 0, unroll=SEQ_UNROLL)

    def project_out():
        if sample:
            for h in range(N_HEADS):
                gate = proj_scr[:, col(3, h)]
                o_scr[:, col(0, h)] = _rmsnorm(o_scr[:, col(0, h)], na_ref[...]) * (gate * _sigmoid(gate))
                gate = proj_scr[:, col(7, h)]
                o_scr[:, col(1, h)] = _groupnorm(o_scr[:, col(1, h)], nb_ref[...]) * (gate * _sigmoid(gate))
        x1_ref[...] = x_ref[...] + _mm(o_scr[...].astype(BF16), wout_ref[...])

    chunk_rows = [slice(c * C, (c + 1) * C) for c in range(chunks)]
    if sample:
        @pl.when(sub == 0)
        def _():
            project(chunk_rows[0])
            score(chunk_rows[0])

        apply_states()
        pl.when(sub == steps_per_tile - 1)(project_out)
    else:
        for rs in chunk_rows:
            project(rs)
        for rs in chunk_rows:
            score(rs)
        project_out()
        for wide, narrow in zip(wide_refs, narrow_refs):
            narrow[...] = wide[...].astype(BF16)

        @pl.when(step == pl.num_programs(1) - 1)
        def _():
            for h in range(N_HEADS):
                sa_out_ref[0, h] = sa_scr[h].T
                sb_out_ref[0, h] = sb_scr[h]


def _full(shape):
    return pl.BlockSpec(shape, lambda *_: (0,) * len(shape))


def _mix_call(x2, cos, sin, w1, w_in, lbp, na, nb, w_out, states, to_narrow, *, n_seq, seq_len, tile, chunks):
    sample = states is not None
    n_rows = n_seq * seq_len
    assert n_rows % tile == 0 and tile % chunks == 0
    chunk = tile // chunks
    nlev, cdec, consts = _chunk_consts(chunk, min(seq_len, chunk))
    state_shape = jax.ShapeDtypeStruct((n_seq, N_HEADS, D_HEAD, D_HEAD), F32)
    if sample:
        assert seq_len == SUBLANES and chunks == 1 and tile % (SAMPLE_SEQ_PER_STEP * seq_len) == 0
        steps_per_tile = tile // (SAMPLE_SEQ_PER_STEP * seq_len)
        grid = (n_seq // SAMPLE_SEQ_PER_STEP,)
        row_map = lambda n: (n // steps_per_tile, 0)
        pos_map = lambda n: (0, 0)
        state_spec = pl.BlockSpec((SAMPLE_SEQ_PER_STEP, N_HEADS, D_HEAD, D_HEAD), lambda n: (n, 0, 0, 0))
        extra_in, extra_args, extra_out, extra_shapes = [state_spec, state_spec], list(states), [], []
        scratch = [pltpu.VMEM((tile, GROUP_W), F32)] * 5
    else:
        assert seq_len % tile == 0
        steps = seq_len // tile
        grid = (n_seq, steps)
        row_map = lambda b, i: (b * steps + i, 0)
        pos_map = lambda b, i: (i, 0)
        state_spec = pl.BlockSpec((1, N_HEADS, D_HEAD, D_HEAD), lambda b, i: (b, 0, 0, 0))
        n_steps = n_seq * steps
        extra_in = [pl.BlockSpec((w.shape[0] // n_steps, w.shape[1]), row_map) for w in to_narrow]
        assert all(w.shape[0] % (n_steps * BF16_ROWS) == 0 for w in to_narrow)
        extra_args, extra_out = list(to_narrow), extra_in
        extra_shapes = [jax.ShapeDtypeStruct(w.shape, BF16) for w in to_narrow]
        scratch = [pltpu.VMEM((N_HEADS, D_HEAD, D_HEAD), F32)] * 2
    const_specs = [_full(c.shape) for c in consts]
    in_specs = [pl.BlockSpec((tile, D_MODEL), row_map), pl.BlockSpec((tile, D_HEAD), pos_map),
                pl.BlockSpec((tile, D_HEAD), pos_map), _full(w1.shape), _full(w_in.shape), _full(lbp.shape),
                _full(na.shape), _full(nb.shape), _full(w_out.shape)] + const_specs + extra_in
    args = [x2, cos, sin, w1, w_in, lbp, na, nb, w_out, *consts] + extra_args
    return pl.pallas_call(
        functools.partial(_mix_kernel, sample=sample, tile=tile, chunks=chunks, seq_rows=min(seq_len, chunk),
                          nlev=nlev, cdec=cdec, seq_per_step=SAMPLE_SEQ_PER_STEP),
        grid=grid,
        in_specs=in_specs,
        out_specs=[pl.BlockSpec((tile, D_MODEL), row_map), state_spec, state_spec] + extra_out,
        out_shape=[jax.ShapeDtypeStruct((n_rows, D_MODEL), F32), state_shape, state_shape] + extra_shapes,
        scratch_shapes=[pltpu.VMEM((tile, D_MODEL), BF16), pltpu.VMEM((tile, IN_WIDTH), F32),
                        pltpu.VMEM((tile, 2 * GROUP_W), F32), pltpu.VMEM((tile, GROUP_W), F32),
                        pltpu.VMEM((tile, GROUP_W), F32)] + scratch,
        compiler_params=pltpu.CompilerParams(dimension_semantics=("arbitrary",) * len(grid),
                                             vmem_limit_bytes=VMEM_LIMIT),
        name="mix_sample" if sample else "mix_prompt",
    )(*args)


def _ffn_kernel(*refs, sample, groups, rows):
    x_ref, w2_ref, wup_ref, cw_ref, cb_ref, wdown_ref, wf_ref = refs[:7]
    refs = refs[7:]
    if sample:
        hist_ref, y_ref, hist_out_ref, h_scr, act_scr = refs
    else:
        y_ref, hist_out_ref, h_scr, act_scr, tail_scr = refs
        step = pl.program_id(1)

        @pl.when(step == 0)
        def _():
            tail_scr[...] = jnp.zeros_like(tail_scr)

    G, L, P = groups, rows, SUBLANES
    parts = 1 if sample else ROW_PARTS
    LP = L // parts
    row_id = lax.broadcasted_iota(jnp.int32, (G, P, FF_COLS), 1)

    def shifted(up, prev2, prev1):
        r1, r2 = pltpu.roll(up, 1, 1), pltpu.roll(up, 2, 1)
        top1 = jnp.where(row_id == 0, prev1, r1[:, :P])
        top2 = jnp.where(row_id == 0, prev2, jnp.where(row_id == 1, prev1, r2[:, :P]))
        if LP == P:
            return top1, top2
        return jnp.concatenate([top1, r1[:, P:]], axis=1), jnp.concatenate([top2, r2[:, P:]], axis=1)

    tails = {}
    for part in range(parts):
        rows_p = slice(part * G * LP, (part + 1) * G * LP)
        h_scr[rows_p, :] = _rmsnorm(x_ref[rows_p, :], w2_ref[...]).astype(BF16)
        for n in range(0, D_FF, FF_COLS):
            conv = []
            for cols in (slice(n, n + FF_COLS), slice(D_FF + n, D_FF + n + FF_COLS)):
                up = _mm(h_scr[rows_p, :], wup_ref[:, cols]).reshape(G, LP, FF_COLS)
                if sample:
                    prev2, prev1 = hist_ref[:, 0:1, cols], hist_ref[:, 1:2, cols]
                    hist_out_ref[:, :, cols] = up[:, LP - 2:, :]
                else:
                    if part == 0:
                        prev2, prev1 = tail_scr[:, P - 2:P - 1, cols], tail_scr[:, P - 1:P, cols]
                    else:
                        prev = tails[cols.start]
                        prev2, prev1 = prev[:, P - 2:P - 1, :], prev[:, P - 1:P, :]
                    tails[cols.start] = up[:, LP - P:, :]
                    if part == parts - 1:
                        tail_scr[:, :, cols] = up[:, LP - P:, :]
                sh1, sh2 = shifted(up, prev2, prev1)
                conv.append(cb_ref[:, cols] + cw_ref[0:1, cols] * sh2 + cw_ref[1:2, cols] * sh1 + cw_ref[2:3, cols] * up)
            u, g = (c.astype(BF16) for c in conv)
            one = jnp.ones((), BF16)
            act_scr[rows_p, n:n + FF_COLS] = ((g * (one / (one + jnp.exp(-g)))) * u).reshape(G * LP, FF_COLS)

        x2 = x_ref[rows_p, :] + _mm(act_scr[rows_p, :], wdown_ref[...])
        y_ref[rows_p, :] = _rmsnorm(x2, wf_ref[...])

    if not sample:
        @pl.when(step == pl.num_programs(1) - 1)
        def _():
            hist_out_ref[...] = tail_scr[:, P - 2:, :]


def _ffn_call(x1, w2, w_up, cw, cb, w_down, wf, hist, *, n_seq, seq_len, tile):
    sample = hist is not None
    n_rows = n_seq * seq_len
    hist_shape = jax.ShapeDtypeStruct((n_seq, CONV_W - 1, FF2), F32)
    if sample:
        groups, rows = tile // seq_len, seq_len
        grid = (n_rows // tile,)
        row_map = lambda n: (n, 0)
        hist_spec = pl.BlockSpec((groups, CONV_W - 1, FF2), lambda n: (n, 0, 0))
        hist_in = [hist_spec]
    else:
        assert seq_len % tile == 0
        groups, rows = 1, tile
        steps = seq_len // tile
        grid = (n_seq, steps)
        row_map = lambda b, i: (b * steps + i, 0)
        hist_spec = pl.BlockSpec((1, CONV_W - 1, FF2), lambda b, i: (b, 0, 0))
        hist_in = []
    in_specs = [pl.BlockSpec((tile, D_MODEL), row_map), _full(w2.shape), _full(w_up.shape), _full(cw.shape),
                _full(cb.shape), _full(w_down.shape), _full(wf.shape)] + hist_in
    args = [x1, w2, w_up, cw, cb, w_down, wf] + ([hist] if sample else [])
    return pl.pallas_call(
        functools.partial(_ffn_kernel, sample=sample, groups=groups, rows=rows),
        grid=grid,
        in_specs=in_specs,
        out_specs=[pl.BlockSpec((tile, D_MODEL), row_map), hist_spec],
        out_shape=[jax.ShapeDtypeStruct((n_rows, D_MODEL), F32), hist_shape],
        scratch_shapes=[pltpu.VMEM((tile, D_MODEL), BF16), pltpu.VMEM((tile, D_FF), BF16)]
        + ([] if sample else [pltpu.VMEM((1, SUBLANES, FF2), F32)]),
        compiler_params=pltpu.CompilerParams(dimension_semantics=("arbitrary",) * len(grid),
                                             vmem_limit_bytes=VMEM_LIMIT),
        name="ffn_sample" if sample else "ffn_prompt",
    )(*args)


def kernel(x_prompt, x_sample, state_hgrn, state_ret, state_conv, w_norm1, w_in, hgrn_lb, hgrn_norm_w, ret_norm_w,
           w_out, w_norm2, w_ffn_in, conv_w, conv_b, w_ffn_out, w_norm_f):
    assert w_in.shape == (1, D_MODEL, IN_WIDTH) and hgrn_lb.shape == (2, GROUP_W)
    mix_w = (w_norm1, w_in[0].astype(BF16), hgrn_lb, hgrn_norm_w, ret_norm_w, w_out[0].astype(BF16))
    wf = w_norm_f.reshape(1, D_MODEL)

    n_seq, seq_len, _ = x_prompt.shape
    cos, sin = _rope_tables(np.arange(seq_len))
    x1, ha_p, rb_p, w_up, w_down = _mix_call(
        x_prompt.reshape(n_seq * seq_len, D_MODEL), cos, sin, *mix_w, None,
        (w_ffn_in[0], w_ffn_out[0].reshape(-1, FF2)), n_seq=n_seq, seq_len=seq_len, tile=512, chunks=2)
    ffn_w = (w_norm2, w_up, conv_w[0], conv_b, w_down.reshape(D_FF, D_MODEL), wf)
    y_p, cv_p = _ffn_call(x1, *ffn_w, None, n_seq=n_seq, seq_len=seq_len, tile=512)

    n_smp, smp_len, _ = x_sample.shape
    smp_tile = 256
    cos, sin = _rope_tables(np.tile(PAST_LEN + np.arange(smp_len), smp_tile // smp_len))
    x1, ha_s, rb_s = _mix_call(
        x_sample.reshape(n_smp * smp_len, D_MODEL), cos, sin, *mix_w, (state_hgrn[0], state_ret[0]), (),
        n_seq=n_smp, seq_len=smp_len, tile=smp_tile, chunks=1)
    y_s, cv_s = _ffn_call(x1, *ffn_w, state_conv[0], n_seq=n_smp, seq_len=smp_len, tile=256)
    return (y_p.reshape(x_prompt.shape), y_s.reshape(x_sample.shape), ha_p[None], rb_p[None], cv_p[None],
            ha_s[None], rb_s[None], cv_s[None])
```

```python
import functools

import numpy as np
import jax
import jax.numpy as jnp
from jax import lax
from jax.experimental import pallas as pl
from jax.experimental.pallas import tpu as pltpu

F32 = jnp.float32
BF16 = jnp.bfloat16

D_MODEL = 1024
N_HEADS = 4
D_HEAD = 128
GROUP_W = N_HEADS * D_HEAD
IN_WIDTH = 8 * GROUP_W
D_FF = 2816
FF2 = 2 * D_FF
CONV_W = 3
PAST_LEN = 16384
ROPE_BASE = 10000.0
EPS = 1e-6
LOG2E = 1.4426950408889634

SUBLANES = 8
BF16_ROWS = 16
PROJ_COLS = 512
FF_COLS = 256
ROW_PARTS = 2
SAMPLE_SEQ_PER_STEP = 8
SEQ_UNROLL = 4
VMEM_LIMIT = 56 * 1024 * 1024


def _mm(a, b):
    return jnp.dot(a, b, preferred_element_type=F32)


def _mm_nt(a, b):
    return lax.dot_general(a, b, (((1,), (1,)), ((), ())), preferred_element_type=F32)


def _mm_tn(a, b):
    return lax.dot_general(a, b, (((0,), (0,)), ((), ())), preferred_element_type=F32)


def _sigmoid(x):
    return 1.0 / (1.0 + jnp.exp(-x))


def _rmsnorm(x, w):
    return x * lax.rsqrt(jnp.mean(x * x, axis=-1, keepdims=True) + EPS) * w


def _groupnorm(x, w):
    xc = x - jnp.mean(x, axis=-1, keepdims=True)
    return xc * lax.rsqrt(jnp.mean(xc * xc, axis=-1, keepdims=True) + EPS) * w


def _chunk_consts(chunk, seq_len):
    nlev = int(np.log2(seq_len))
    assert 1 << nlev == seq_len and chunk % seq_len == 0
    r = np.arange(chunk)
    rr, cc = r[:, None], r[None, :]
    same_seq = (rr // seq_len) == (cc // seq_len)
    cum = (same_seq & (cc <= rr)).astype(np.float32)
    x = rr ^ cc
    bit_len = np.where(x > 0, np.floor(np.log2(np.maximum(x, 1))).astype(np.int64) + 1, 0)
    level = np.where(same_seq & (cc <= rr), bit_len, -1).astype(np.int32)

    pos = r % seq_len
    log_gamma = np.log1p(-np.exp2(-5.0 - np.arange(N_HEADS, dtype=np.float64)))[:, None, None]
    rel = (pos[:, None] - pos[None, :]).astype(np.float64)[None]
    causal = (same_seq & (cc <= rr))[None]
    dec = np.where(causal, np.exp(np.where(causal, rel, 0.0) * log_gamma), 0.0)
    ones = np.ones((1, 1, D_HEAD))
    inner = np.exp((pos + 1.0)[None, :, None] * log_gamma) * ones
    sdec = np.exp((seq_len - 1.0 - pos)[None, :, None] * log_gamma) * ones
    cdec = tuple(float(v) for v in np.exp(seq_len * log_gamma[:, 0, 0]))
    consts = (jnp.asarray(cum, BF16), jnp.asarray(level), jnp.asarray(dec, F32), jnp.asarray(inner, F32),
              jnp.asarray(sdec, F32))
    return nlev, cdec, consts


def _rope_tables(pos):
    half = D_HEAD // 2
    inv = 1.0 / (ROPE_BASE ** (np.arange(half, dtype=np.float64) / half))
    ang = np.asarray(pos, np.float64)[:, None] * inv[None, :]
    cos, sin = np.cos(ang), np.sin(ang)
    return (jnp.asarray(np.concatenate([cos, cos], axis=-1), F32),
            jnp.asarray(np.concatenate([-sin, sin], axis=-1), F32))


def _mix_kernel(*refs, sample, tile, chunks, seq_rows, nlev, cdec, seq_per_step):
    (x_ref, cos_ref, sin_ref, w1_ref, win_ref, lbp_ref, na_ref, nb_ref, wout_ref,
     cum_ref, level_ref, dec_ref, inner_ref, sdec_ref) = refs[:14]
    refs = refs[14:]
    if sample:
        sa_in_ref, sb_in_ref = refs[:2]
        x1_ref, sa_out_ref, sb_out_ref = refs[2:5]
    else:
        wide_refs = refs[:2]
        x1_ref, sa_out_ref, sb_out_ref = refs[2:5]
        narrow_refs = refs[5:7]
        refs = refs[2:]
    refs = refs[5:]
    h_scr, proj_scr, o_scr, d_scr, k_scr = refs[:5]
    refs = refs[5:]
    C = tile // chunks
    if sample:
        qe_scr, kh_scr, ex_scr, qi_scr, ks_scr = refs
        steps_per_tile = C // (seq_per_step * SUBLANES)
        sub = pl.program_id(0) % steps_per_tile
    else:
        sa_scr, sb_scr = refs
        step = pl.program_id(1)

        @pl.when(step == 0)
        def _():
            sa_scr[...] = jnp.zeros_like(sa_scr)
            sb_scr[...] = jnp.zeros_like(sb_scr)

    def col(group, h):
        return slice(group * GROUP_W + h * D_HEAD, group * GROUP_W + (h + 1) * D_HEAD)

    def head(h):
        return slice(h * D_HEAD, (h + 1) * D_HEAD)

    def block_rows(x, m, row):
        x3 = x.reshape(C // m, m, D_HEAD)
        return jnp.broadcast_to(x3[:, row:row + 1, :], x3.shape).reshape(C, D_HEAD)

    def upper_rows(x, m):
        return x.reshape(C // m, 2, m // 2, x.shape[-1])[:, 1].reshape(C // 2, x.shape[-1])

    def put_upper_rows(x, xu, m):
        x4 = x.reshape(C // m, 2, m // 2, x.shape[-1])
        xu4 = xu.reshape(C // m, 1, m // 2, x.shape[-1])
        return jnp.concatenate([x4[:, 0:1], xu4], axis=1).reshape(C, x.shape[-1])

    def project(rs):
        h_scr[rs, :] = _rmsnorm(x_ref[rs, :], w1_ref[...]).astype(BF16)
        for n in range(0, IN_WIDTH, PROJ_COLS):
            proj_scr[rs, n:n + PROJ_COLS] = _mm(h_scr[rs, :], win_ref[:, n:n + PROJ_COLS])

    def score(rs):
        lb0, lb1 = lbp_ref[0:1, :], lbp_ref[1:2, :]
        lb_max = jnp.maximum(lb0, lb1)
        e0, e1 = jnp.exp(lb0 - lb_max), jnp.exp(lb1 - lb_max)
        lb = e0 / (e0 + e1)

        row_id = lax.broadcasted_iota(jnp.int32, (C, D_HEAD), 0)

        f = lb + (1.0 - lb) * _sigmoid(proj_scr[rs, GROUP_W:2 * GROUP_W])
        k_scr[rs, :] = 1.0 - f
        g = jnp.log(f)
        g_hi = g.astype(BF16)
        g_mid = (g - g_hi.astype(F32)).astype(BF16)
        g_lo = (g - g_hi.astype(F32) - g_mid.astype(F32)).astype(BF16)
        d_scr[rs, :] = _mm(cum_ref[...], g_hi) + _mm(cum_ref[...], g_mid) + _mm(cum_ref[...], g_lo)

        signs = [jnp.where((row_id & (1 << (lev - 1))) != 0, LOG2E, -LOG2E) for lev in range(3, nlev + 1)]

        for h in range(N_HEADS):
            hs = head(h)
            q = proj_scr[rs, col(0, h)]
            k = k_scr[rs, hs]
            v = proj_scr[rs, col(2, h)].astype(BF16)
            b = d_scr[rs, hs]
            a = jnp.where(level_ref[...] == 0, _mm_nt(q.astype(BF16), k.astype(BF16)), 0.0)
            for lev in range(1, nlev + 1):
                m = 1 << lev
                upper = (row_id & (m // 2)) != 0
                if lev == 1:
                    z = jnp.where(upper, q * (1.0 - k), k)
                elif lev == 2:
                    fh = 1.0 - k
                    pos4 = row_id & 3
                    decay = jnp.where(pos4 == 0, pltpu.roll(fh, C - 1, 0),
                                      jnp.where(pos4 == 1, 1.0, jnp.where(pos4 == 2, fh, fh * pltpu.roll(fh, 1, 0))))
                    z = jnp.where(upper, q, k) * decay
                else:
                    z = jnp.where(upper, q, k) * jnp.exp2((b - block_rows(b, m, m // 2 - 1)) * signs[lev - 3])
                if m < 2 * SUBLANES:
                    z = z.astype(BF16)
                    a = jnp.where(level_ref[...] == lev, _mm_nt(z, z), a)
                else:
                    zq = upper_rows(z, m).astype(BF16)
                    zk = z.astype(BF16)
                    if m > D_HEAD:
                        p = [_mm_nt(zq[i * (m // 2):(i + 1) * (m // 2)], zk[i * m:i * m + m // 2])
                             for i in range(C // m)]
                        width = m // 2
                    else:
                        p = [_mm_nt(zq[i * (D_HEAD // 2):(i + 1) * (D_HEAD // 2)], zk[i * D_HEAD:(i + 1) * D_HEAD])
                             for i in range(C // D_HEAD)]
                        width = D_HEAD
                    full = jnp.concatenate([jnp.concatenate([pi] * (C // width), axis=1) for pi in p], axis=0)
                    lvl_u = upper_rows(level_ref[...], m)
                    a = put_upper_rows(a, jnp.where(lvl_u == lev, full, upper_rows(a, m)), m)
            o = _mm(a.astype(BF16), v)
            eb = jnp.exp(b)
            qe = q * eb
            kh = k * jnp.exp(block_rows(b, seq_rows, seq_rows - 1) - b)
            if sample:
                o_scr[rs, col(0, h)] = o
                qe_scr[rs, hs] = qe
                kh_scr[rs, hs] = kh
                e_all = block_rows(eb, seq_rows, seq_rows - 1)
                e_hi = e_all.astype(BF16).astype(F32)
                e_mid = (e_all - e_hi).astype(BF16).astype(F32)
                e_lo = e_all - e_hi - e_mid
                pos = row_id & (seq_rows - 1)
                ex_scr[rs, hs] = jnp.where(pos == 0, e_hi, jnp.where(pos == 1, e_mid, jnp.where(pos == 2, e_lo, 0.0)))
            else:
                st = sa_scr[h]
                o = o + _mm_nt(qe.astype(BF16), st.astype(BF16))
                sa_scr[h] = st * eb[C - 1:C, :] + _mm_tn(v, kh.astype(BF16))
                gate = proj_scr[rs, col(3, h)]
                o_scr[rs, col(0, h)] = _rmsnorm(o, na_ref[...]) * (gate * _sigmoid(gate))

        cos, sin = cos_ref[rs, :], sin_ref[rs, :]
        for h in range(N_HEADS):
            hs = head(h)
            q = proj_scr[rs, col(4, h)]
            k = proj_scr[rs, col(5, h)]
            v = proj_scr[rs, col(6, h)].astype(BF16)
            qr = q * cos + pltpu.roll(q, D_HEAD // 2, 1) * sin
            kr = (k * cos + pltpu.roll(k, D_HEAD // 2, 1) * sin) * (D_HEAD ** -0.5)
            a = _mm_nt(qr.astype(BF16), kr.astype(BF16)) * dec_ref[h]
            o = _mm(a.astype(BF16), v)
            qi = qr * inner_ref[h]
            ks = kr * sdec_ref[h]
            if sample:
                o_scr[rs, col(1, h)] = o
                qi_scr[rs, hs] = qi
                ks_scr[rs, hs] = ks
            else:
                st = sb_scr[h]
                o = o + _mm(qi.astype(BF16), st.astype(BF16))
                sb_scr[h] = cdec[h] * st + _mm_tn(ks.astype(BF16), v)
                gate = proj_scr[rs, col(7, h)]
                o_scr[rs, col(1, h)] = _groupnorm(o, nb_ref[...]) * (gate * _sigmoid(gate))

    def apply_states():
        zeros8 = jnp.zeros((SUBLANES, D_HEAD), F32)
        sel8 = jnp.where(lax.broadcasted_iota(jnp.int32, (SUBLANES, D_HEAD), 0) < 3, 1.0, 0.0)

        def pair_readout(lhs_scr, st_ref, j, rows, group, h):
            lhs = jnp.concatenate([lhs_scr[rows, head(h)], lhs_scr[rows, head(h + 1)]], axis=0).astype(BF16)
            w = jnp.concatenate([st_ref[j, h], st_ref[j, h + 1]], axis=1).astype(BF16)
            oo = _mm(lhs, w)
            o_scr[rows, col(group, h)] += oo[:SUBLANES, :D_HEAD]
            o_scr[rows, col(group, h + 1)] += oo[SUBLANES:, D_HEAD:]

        def seq_body(j, carry):
            rows = pl.ds(pl.multiple_of(sub * (seq_per_step * SUBLANES) + j * SUBLANES, SUBLANES), SUBLANES)
            for h in range(0, N_HEADS, 2):
                pair_readout(qe_scr, sa_in_ref, j, rows, 0, h)
                pair_readout(qi_scr, sb_in_ref, j, rows, 1, h)
            for h in range(N_HEADS):
                hs = head(h)
                lhs = jnp.concatenate([kh_scr[rows, hs], ex_scr[rows, hs]], axis=0).astype(BF16)
                v = proj_scr[rows, col(2, h)]
                rhs = jnp.concatenate([jnp.concatenate([v, zeros8], axis=1),
                                       jnp.concatenate([zeros8, sel8], axis=1)], axis=0).astype(BF16)
                upd = _mm_tn(lhs, rhs)
                sa_out_ref[j, h] = sa_in_ref[j, h] * upd[:, D_HEAD:] + upd[:, :D_HEAD]
                v = proj_scr[rows, col(6, h)].astype(BF16)
                sb_out_ref[j, h] = cdec[h] * sb_in_ref[j, h] + _mm_tn(ks_scr[rows, hs].astype(BF16), v)
            return carry

        lax.fori_loop(0, seq_per_step, seq_body, 0, unroll=SEQ_UNROLL)

    def project_out():
        if sample:
            for h in range(N_HEADS):
                gate = proj_scr[:, col(3, h)]
                o_scr[:, col(0, h)] = _rmsnorm(o_scr[:, col(0, h)], na_ref[...]) * (gate * _sigmoid(gate))
                gate = proj_scr[:, col(7, h)]
                o_scr[:, col(1, h)] = _groupnorm(o_scr[:, col(1, h)], nb_ref[...]) * (gate * _sigmoid(gate))
        x1_ref[...] = x_ref[...] + _mm(o_scr[...].astype(BF16), wout_ref[...])

    chunk_rows = [slice(c * C, (c + 1) * C) for c in range(chunks)]
    if sample:
        @pl.when(sub == 0)
        def _():
            project(chunk_rows[0])
            score(chunk_rows[0])

        apply_states()
        pl.when(sub == steps_per_tile - 1)(project_out)
    else:
        for rs in chunk_rows:
            project(rs)
        for rs in chunk_rows:
            score(rs)
        project_out()
        for wide, narrow in zip(wide_refs, narrow_refs):
            narrow[...] = wide[...].astype(BF16)

        @pl.when(step == pl.num_programs(1) - 1)
        def _():
            for h in range(N_HEADS):
                sa_out_ref[0, h] = sa_scr[h].T
                sb_out_ref[0, h] = sb_scr[h]


def _full(shape):
    return pl.BlockSpec(shape, lambda *_: (0,) * len(shape))


def _row_block_spec(shape, n_steps, inner_steps):
    _, n_rows, width = shape
    share = next(k for k in (1, 2, 4, 8) if n_steps % k == 0 and n_rows % ((n_steps // k) * BF16_ROWS) == 0)
    return pl.BlockSpec((1, n_rows // (n_steps // share), width),
                        lambda b, i: (0, (b * inner_steps + i) // share, 0))


def _mix_call(x2, cos, sin, w1, w_in, lbp, na, nb, w_out, states, to_narrow, *, n_seq, seq_len, tile, chunks):
    sample = states is not None
    n_rows = n_seq * seq_len
    assert n_rows % tile == 0 and tile % chunks == 0
    chunk = tile // chunks
    nlev, cdec, consts = _chunk_consts(chunk, min(seq_len, chunk))
    state_shape = jax.ShapeDtypeStruct((n_seq, N_HEADS, D_HEAD, D_HEAD), F32)
    if sample:
        assert seq_len == SUBLANES and chunks == 1 and tile % (SAMPLE_SEQ_PER_STEP * seq_len) == 0
        steps_per_tile = tile // (SAMPLE_SEQ_PER_STEP * seq_len)
        grid = (n_seq // SAMPLE_SEQ_PER_STEP,)
        row_map = lambda n: (n // steps_per_tile, 0)
        pos_map = lambda n: (0, 0)
        state_spec = pl.BlockSpec((SAMPLE_SEQ_PER_STEP, N_HEADS, D_HEAD, D_HEAD), lambda n: (n, 0, 0, 0))
        extra_in, extra_args, extra_out, extra_shapes = [state_spec, state_spec], list(states), [], []
        scratch = [pltpu.VMEM((tile, GROUP_W), F32)] * 5
    else:
        assert seq_len % tile == 0
        steps = seq_len // tile
        grid = (n_seq, steps)
        row_map = lambda b, i: (b * steps + i, 0)
        pos_map = lambda b, i: (i, 0)
        state_spec = pl.BlockSpec((1, N_HEADS, D_HEAD, D_HEAD), lambda b, i: (b, 0, 0, 0))
        n_steps = n_seq * steps
        extra_in = [_row_block_spec(w.shape, n_steps, steps) for w in to_narrow]
        extra_args, extra_out = list(to_narrow), extra_in
        extra_shapes = [jax.ShapeDtypeStruct(w.shape, BF16) for w in to_narrow]
        scratch = [pltpu.VMEM((N_HEADS, D_HEAD, D_HEAD), F32)] * 2
    const_specs = [_full(c.shape) for c in consts]
    in_specs = [pl.BlockSpec((tile, D_MODEL), row_map), pl.BlockSpec((tile, D_HEAD), pos_map),
                pl.BlockSpec((tile, D_HEAD), pos_map), _full(w1.shape), _full(w_in.shape), _full(lbp.shape),
                _full(na.shape), _full(nb.shape), _full(w_out.shape)] + const_specs + extra_in
    args = [x2, cos, sin, w1, w_in, lbp, na, nb, w_out, *consts] + extra_args
    return pl.pallas_call(
        functools.partial(_mix_kernel, sample=sample, tile=tile, chunks=chunks, seq_rows=min(seq_len, chunk),
                          nlev=nlev, cdec=cdec, seq_per_step=SAMPLE_SEQ_PER_STEP),
        grid=grid,
        in_specs=in_specs,
        out_specs=[pl.BlockSpec((tile, D_MODEL), row_map), state_spec, state_spec] + extra_out,
        out_shape=[jax.ShapeDtypeStruct((n_rows, D_MODEL), F32), state_shape, state_shape] + extra_shapes,
        scratch_shapes=[pltpu.VMEM((tile, D_MODEL), BF16), pltpu.VMEM((tile, IN_WIDTH), F32),
                        pltpu.VMEM((tile, 2 * GROUP_W), F32), pltpu.VMEM((tile, GROUP_W), F32),
                        pltpu.VMEM((tile, GROUP_W), F32)] + scratch,
        compiler_params=pltpu.CompilerParams(dimension_semantics=("arbitrary",) * len(grid),
                                             vmem_limit_bytes=VMEM_LIMIT),
        name="mix_sample" if sample else "mix_prompt",
    )(*args)


def _ffn_kernel(*refs, sample, groups, rows):
    x_ref, w2_ref, wup_ref, cw_ref, cb_ref, wdown_ref, wf_ref = refs[:7]
    refs = refs[7:]
    if sample:
        hist_ref, y_ref, hist_out_ref, h_scr, act_scr = refs
    else:
        y_ref, hist_out_ref, h_scr, act_scr, tail_scr = refs
        step = pl.program_id(1)

        @pl.when(step == 0)
        def _():
            tail_scr[...] = jnp.zeros_like(tail_scr)

    G, L, P = groups, rows, SUBLANES
    parts = 1 if sample else ROW_PARTS
    LP = L // parts
    row_id = lax.broadcasted_iota(jnp.int32, (G, P, FF_COLS), 1)

    def shifted(up, prev2, prev1):
        r1, r2 = pltpu.roll(up, 1, 1), pltpu.roll(up, 2, 1)
        top1 = jnp.where(row_id == 0, prev1, r1[:, :P])
        top2 = jnp.where(row_id == 0, prev2, jnp.where(row_id == 1, prev1, r2[:, :P]))
        if LP == P:
            return top1, top2
        return jnp.concatenate([top1, r1[:, P:]], axis=1), jnp.concatenate([top2, r2[:, P:]], axis=1)

    tails = {}
    for part in range(parts):
        rows_p = slice(part * G * LP, (part + 1) * G * LP)
        h_scr[rows_p, :] = _rmsnorm(x_ref[rows_p, :], w2_ref[...]).astype(BF16)
        for n in range(0, D_FF, FF_COLS):
            conv = []
            for cols in (slice(n, n + FF_COLS), slice(D_FF + n, D_FF + n + FF_COLS)):
                up = _mm(h_scr[rows_p, :], wup_ref[0, :, cols]).reshape(G, LP, FF_COLS)
                if sample:
                    prev2, prev1 = hist_ref[:, 0:1, cols], hist_ref[:, 1:2, cols]
                    hist_out_ref[:, :, cols] = up[:, LP - 2:, :]
                else:
                    if part == 0:
                        prev2, prev1 = tail_scr[:, P - 2:P - 1, cols], tail_scr[:, P - 1:P, cols]
                    else:
                        prev = tails[cols.start]
                        prev2, prev1 = prev[:, P - 2:P - 1, :], prev[:, P - 1:P, :]
                    tails[cols.start] = up[:, LP - P:, :]
                    if part == parts - 1:
                        tail_scr[:, :, cols] = up[:, LP - P:, :]
                sh1, sh2 = shifted(up, prev2, prev1)
                conv.append(cb_ref[:, cols] + cw_ref[0:1, cols] * sh2 + cw_ref[1:2, cols] * sh1 + cw_ref[2:3, cols] * up)
            u, g = (c.astype(BF16) for c in conv)
            one = jnp.ones((), BF16)
            act_scr[rows_p, n:n + FF_COLS] = ((g * (one / (one + jnp.exp(-g)))) * u).reshape(G * LP, FF_COLS)

        x2 = x_ref[rows_p, :] + _mm(act_scr[rows_p, :], wdown_ref[0])
        y_ref[rows_p, :] = _rmsnorm(x2, wf_ref[...])

    if not sample:
        @pl.when(step == pl.num_programs(1) - 1)
        def _():
            hist_out_ref[...] = tail_scr[:, P - 2:, :]


def _ffn_call(x1, w2, w_up, cw, cb, w_down, wf, hist, *, n_seq, seq_len, tile):
    sample = hist is not None
    n_rows = n_seq * seq_len
    hist_shape = jax.ShapeDtypeStruct((n_seq, CONV_W - 1, FF2), F32)
    if sample:
        groups, rows = tile // seq_len, seq_len
        grid = (n_rows // tile,)
        row_map = lambda n: (n, 0)
        hist_spec = pl.BlockSpec((groups, CONV_W - 1, FF2), lambda n: (n, 0, 0))
        hist_in = [hist_spec]
    else:
        assert seq_len % tile == 0
        groups, rows = 1, tile
        steps = seq_len // tile
        grid = (n_seq, steps)
        row_map = lambda b, i: (b * steps + i, 0)
        hist_spec = pl.BlockSpec((1, CONV_W - 1, FF2), lambda b, i: (b, 0, 0))
        hist_in = []
    in_specs = [pl.BlockSpec((tile, D_MODEL), row_map), _full(w2.shape), _full(w_up.shape), _full(cw.shape),
                _full(cb.shape), _full(w_down.shape), _full(wf.shape)] + hist_in
    args = [x1, w2, w_up, cw, cb, w_down, wf] + ([hist] if sample else [])
    return pl.pallas_call(
        functools.partial(_ffn_kernel, sample=sample, groups=groups, rows=rows),
        grid=grid,
        in_specs=in_specs,
        out_specs=[pl.BlockSpec((tile, D_MODEL), row_map), hist_spec],
        out_shape=[jax.ShapeDtypeStruct((n_rows, D_MODEL), F32), hist_shape],
        scratch_shapes=[pltpu.VMEM((tile, D_MODEL), BF16), pltpu.VMEM((tile, D_FF), BF16)]
        + ([] if sample else [pltpu.VMEM((1, SUBLANES, FF2), F32)]),
        compiler_params=pltpu.CompilerParams(dimension_semantics=("arbitrary",) * len(grid),
                                             vmem_limit_bytes=VMEM_LIMIT),
        name="ffn_sample" if sample else "ffn_prompt",
    )(*args)


def kernel(x_prompt, x_sample, state_hgrn, state_ret, state_conv, w_norm1, w_in, hgrn_lb, hgrn_norm_w, ret_norm_w,
           w_out, w_norm2, w_ffn_in, conv_w, conv_b, w_ffn_out, w_norm_f):
    assert w_in.shape == (1, D_MODEL, IN_WIDTH) and hgrn_lb.shape == (2, GROUP_W)
    mix_w = (w_norm1, w_in[0].astype(BF16), hgrn_lb, hgrn_norm_w, ret_norm_w, w_out[0].astype(BF16))
    wf = w_norm_f.reshape(1, D_MODEL)

    n_seq, seq_len, _ = x_prompt.shape
    cos, sin = _rope_tables(np.arange(seq_len))
    x1, ha_p, rb_p, w_up, w_down = _mix_call(
        x_prompt.reshape(n_seq * seq_len, D_MODEL), cos, sin, *mix_w, None, (w_ffn_in, w_ffn_out),
        n_seq=n_seq, seq_len=seq_len, tile=512, chunks=2)
    ffn_w = (w_norm2, w_up, conv_w[0], conv_b, w_down, wf)
    y_p, cv_p = _ffn_call(x1, *ffn_w, None, n_seq=n_seq, seq_len=seq_len, tile=512)

    n_smp, smp_len, _ = x_sample.shape
    smp_tile = 256
    cos, sin = _rope_tables(np.tile(PAST_LEN + np.arange(smp_len), smp_tile // smp_len))
    x1, ha_s, rb_s = _mix_call(
        x_sample.reshape(n_smp * smp_len, D_MODEL), cos, sin, *mix_w, (state_hgrn[0], state_ret[0]), (),
        n_seq=n_smp, seq_len=smp_len, tile=smp_tile, chunks=1)
    y_s, cv_s = _ffn_call(x1, *ffn_w, state_conv[0], n_seq=n_smp, seq_len=smp_len, tile=256)
    return (y_p.reshape(x_prompt.shape), y_s.reshape(x_sample.shape), ha_p[None], rb_p[None], cv_p[None],
            ha_s[None], rb_s[None], cv_s[None])
```

```python
import functools

import numpy as np
import jax
import jax.numpy as jnp
from jax import lax
from jax.experimental import pallas as pl
from jax.experimental.pallas import tpu as pltpu

F32 = jnp.float32
BF16 = jnp.bfloat16

D_MODEL = 1024
N_HEADS = 4
D_HEAD = 128
GROUP_W = N_HEADS * D_HEAD
IN_WIDTH = 8 * GROUP_W
D_FF = 2816
FF2 = 2 * D_FF
CONV_W = 3
PAST_LEN = 16384
ROPE_BASE = 10000.0
EPS = 1e-6
LOG2E = 1.4426950408889634

SUBLANES = 8
BF16_ROWS = 16
PROJ_COLS = 512
FF_COLS = 256
ROW_PARTS = 2
KEPT_GROUPS = (2, 3, 6, 7)
VMEM_LIMIT = 56 * 1024 * 1024


def _mm(a, b):
    return jnp.dot(a, b, preferred_element_type=F32)


def _mm_nt(a, b):
    return lax.dot_general(a, b, (((1,), (1,)), ((), ())), preferred_element_type=F32)


def _mm_tn(a, b):
    return lax.dot_general(a, b, (((0,), (0,)), ((), ())), preferred_element_type=F32)


def _sigmoid(x):
    return 1.0 / (1.0 + jnp.exp(-x))


def _rmsnorm(x, w):
    return x * lax.rsqrt(jnp.mean(x * x, axis=-1, keepdims=True) + EPS) * w


def _groupnorm(x, w):
    xc = x - jnp.mean(x, axis=-1, keepdims=True)
    return xc * lax.rsqrt(jnp.mean(xc * xc, axis=-1, keepdims=True) + EPS) * w


def _chunk_consts(chunk, seq_len):
    nlev = int(np.log2(seq_len))
    assert 1 << nlev == seq_len and chunk % seq_len == 0
    r = np.arange(chunk)
    rr, cc = r[:, None], r[None, :]
    same_seq = (rr // seq_len) == (cc // seq_len)
    cum = (same_seq & (cc <= rr)).astype(np.float32)
    x = rr ^ cc
    bit_len = np.where(x > 0, np.floor(np.log2(np.maximum(x, 1))).astype(np.int64) + 1, 0)
    level = np.where(same_seq & (cc <= rr), bit_len, -1).astype(np.int32)

    pos = r % seq_len
    log_gamma = np.log1p(-np.exp2(-5.0 - np.arange(N_HEADS, dtype=np.float64)))[:, None, None]
    rel = (pos[:, None] - pos[None, :]).astype(np.float64)[None]
    causal = (same_seq & (cc <= rr))[None]
    dec = np.where(causal, np.exp(np.where(causal, rel, 0.0) * log_gamma), 0.0)
    ones = np.ones((1, 1, D_HEAD))
    inner = np.exp((pos + 1.0)[None, :, None] * log_gamma) * ones
    sdec = np.exp((seq_len - 1.0 - pos)[None, :, None] * log_gamma) * ones
    cdec = tuple(float(v) for v in np.exp(seq_len * log_gamma[:, 0, 0]))
    consts = (jnp.asarray(cum, BF16), jnp.asarray(level), jnp.asarray(dec, F32), jnp.asarray(inner, F32),
              jnp.asarray(sdec, F32))
    return nlev, cdec, consts


def _rope_tables(pos):
    half = D_HEAD // 2
    inv = 1.0 / (ROPE_BASE ** (np.arange(half, dtype=np.float64) / half))
    ang = np.asarray(pos, np.float64)[:, None] * inv[None, :]
    cos, sin = np.cos(ang), np.sin(ang)
    return (jnp.asarray(np.concatenate([cos, cos], axis=-1), F32),
            jnp.asarray(np.concatenate([-sin, sin], axis=-1), F32))


def _mix_kernel(*refs, sample, tile, chunks, seq_rows, nlev, cdec, smp_cdec, smp_seq_per_step):
    (x_ref, cos_ref, sin_ref, w1_ref, win_ref, lbp_ref, na_ref, nb_ref, wout_ref,
     cum_ref, level_ref, dec_ref, inner_ref, sdec_ref) = refs[:14]
    refs = refs[14:]
    if sample:
        o_scr, qe_scr, kh_scr, ex_scr, qi_scr, ks_scr, keep_ref = refs[:7]
        h_scr, proj_scr, d_scr, k_scr = refs[7:]
    else:
        wide_refs = refs[:2]
        qe_ref, kh_ref, ex_ref, qi_ref, ks_ref, keep_in_ref, os_in_ref, sa_in_ref, sb_in_ref = refs[2:11]
        x1_ref, sa_out_ref, sb_out_ref = refs[11:14]
        narrow_refs = refs[14:16]
        os_out_ref, sas_out_ref, sbs_out_ref = refs[16:19]
        h_scr, proj_scr, o_scr, d_scr, k_scr, sa_scr, sb_scr = refs[19:]
        step = pl.program_id(1)

        @pl.when(step == 0)
        def _():
            sa_scr[...] = jnp.zeros_like(sa_scr)
            sb_scr[...] = jnp.zeros_like(sb_scr)

    C = tile // chunks

    def col(group, h):
        return slice(group * GROUP_W + h * D_HEAD, group * GROUP_W + (h + 1) * D_HEAD)

    def head(h):
        return slice(h * D_HEAD, (h + 1) * D_HEAD)

    def block_rows(x, m, row):
        x3 = x.reshape(C // m, m, D_HEAD)
        return jnp.broadcast_to(x3[:, row:row + 1, :], x3.shape).reshape(C, D_HEAD)

    def upper_rows(x, m):
        return x.reshape(C // m, 2, m // 2, x.shape[-1])[:, 1].reshape(C // 2, x.shape[-1])

    def put_upper_rows(x, xu, m):
        x4 = x.reshape(C // m, 2, m // 2, x.shape[-1])
        xu4 = xu.reshape(C // m, 1, m // 2, x.shape[-1])
        return jnp.concatenate([x4[:, 0:1], xu4], axis=1).reshape(C, x.shape[-1])

    def project(rs):
        h_scr[rs, :] = _rmsnorm(x_ref[rs, :], w1_ref[...]).astype(BF16)
        for n in range(0, IN_WIDTH, PROJ_COLS):
            proj_scr[rs, n:n + PROJ_COLS] = _mm(h_scr[rs, :], win_ref[:, n:n + PROJ_COLS])

    def score(rs):
        lb0, lb1 = lbp_ref[0:1, :], lbp_ref[1:2, :]
        lb_max = jnp.maximum(lb0, lb1)
        e0, e1 = jnp.exp(lb0 - lb_max), jnp.exp(lb1 - lb_max)
        lb = e0 / (e0 + e1)

        row_id = lax.broadcasted_iota(jnp.int32, (C, D_HEAD), 0)

        f = lb + (1.0 - lb) * _sigmoid(proj_scr[rs, GROUP_W:2 * GROUP_W])
        k_scr[rs, :] = 1.0 - f
        g = jnp.log(f)
        g_hi = g.astype(BF16)
        g_lo = (g - g_hi.astype(F32)).astype(BF16)
        d_scr[rs, :] = _mm(cum_ref[...], g_hi) + _mm(cum_ref[...], g_lo)

        signs = [jnp.where((row_id & (1 << (lev - 1))) != 0, LOG2E, -LOG2E) for lev in range(3, nlev + 1)]

        for h in range(N_HEADS):
            hs = head(h)
            q = proj_scr[rs, col(0, h)]
            k = k_scr[rs, hs]
            v = proj_scr[rs, col(2, h)].astype(BF16)
            b = d_scr[rs, hs]
            a = jnp.where(level_ref[...] == 0, _mm_nt(q.astype(BF16), k.astype(BF16)), 0.0)
            for lev in range(1, nlev + 1):
                m = 1 << lev
                upper = (row_id & (m // 2)) != 0
                if lev == 1:
                    z = jnp.where(upper, q * (1.0 - k), k)
                elif lev == 2:
                    fh = 1.0 - k
                    pos4 = row_id & 3
                    decay = jnp.where(pos4 == 0, pltpu.roll(fh, C - 1, 0),
                                      jnp.where(pos4 == 1, 1.0, jnp.where(pos4 == 2, fh, fh * pltpu.roll(fh, 1, 0))))
                    z = jnp.where(upper, q, k) * decay
                else:
                    z = jnp.where(upper, q, k) * jnp.exp2((b - block_rows(b, m, m // 2 - 1)) * signs[lev - 3])
                if m < 2 * SUBLANES:
                    z = z.astype(BF16)
                    a = jnp.where(level_ref[...] == lev, _mm_nt(z, z), a)
                else:
                    zq = upper_rows(z, m).astype(BF16)
                    zk = z.astype(BF16)
                    if m > D_HEAD:
                        p = [_mm_nt(zq[i * (m // 2):(i + 1) * (m // 2)], zk[i * m:i * m + m // 2])
                             for i in range(C // m)]
                        width = m // 2
                    else:
                        p = [_mm_nt(zq[i * (D_HEAD // 2):(i + 1) * (D_HEAD // 2)], zk[i * D_HEAD:(i + 1) * D_HEAD])
                             for i in range(C // D_HEAD)]
                        width = D_HEAD
                    full = jnp.concatenate([jnp.concatenate([pi] * (C // width), axis=1) for pi in p], axis=0)
                    lvl_u = upper_rows(level_ref[...], m)
                    a = put_upper_rows(a, jnp.where(lvl_u == lev, full, upper_rows(a, m)), m)
            o = _mm(a.astype(BF16), v)
            eb = jnp.exp(b)
            qe = q * eb
            kh = k * jnp.exp(block_rows(b, seq_rows, seq_rows - 1) - b)
            if sample:
                o_scr[rs, col(0, h)] = o
                qe_scr[rs, hs] = qe
                kh_scr[rs, hs] = kh
                e_all = block_rows(eb, seq_rows, seq_rows - 1)
                e_hi = e_all.astype(BF16).astype(F32)
                e_mid = (e_all - e_hi).astype(BF16).astype(F32)
                e_lo = e_all - e_hi - e_mid
                pos = row_id & (seq_rows - 1)
                ex_scr[rs, hs] = jnp.where(pos == 0, e_hi, jnp.where(pos == 1, e_mid, jnp.where(pos == 2, e_lo, 0.0)))
            else:
                st = sa_scr[h]
                o = o + _mm_nt(qe.astype(BF16), st.astype(BF16))
                sa_scr[h] = st * eb[C - 1:C, :] + _mm_tn(v, kh.astype(BF16))
                gate = proj_scr[rs, col(3, h)]
                o_scr[rs, col(0, h)] = _rmsnorm(o, na_ref[...]) * (gate * _sigmoid(gate))

        cos, sin = cos_ref[rs, :], sin_ref[rs, :]
        for h in range(N_HEADS):
            hs = head(h)
            q = proj_scr[rs, col(4, h)]
            k = proj_scr[rs, col(5, h)]
            v = proj_scr[rs, col(6, h)].astype(BF16)
            qr = q * cos + pltpu.roll(q, D_HEAD // 2, 1) * sin
            kr = (k * cos + pltpu.roll(k, D_HEAD // 2, 1) * sin) * (D_HEAD ** -0.5)
            a = _mm_nt(qr.astype(BF16), kr.astype(BF16)) * dec_ref[h]
            o = _mm(a.astype(BF16), v)
            qi = qr * inner_ref[h]
            ks = kr * sdec_ref[h]
            if sample:
                o_scr[rs, col(1, h)] = o
                qi_scr[rs, hs] = qi
                ks_scr[rs, hs] = ks
            else:
                st = sb_scr[h]
                o = o + _mm(qi.astype(BF16), st.astype(BF16))
                sb_scr[h] = cdec[h] * st + _mm_tn(ks.astype(BF16), v)
                gate = proj_scr[rs, col(7, h)]
                o_scr[rs, col(1, h)] = _groupnorm(o, nb_ref[...]) * (gate * _sigmoid(gate))

        if sample:
            for slot, group in enumerate(KEPT_GROUPS):
                keep_ref[:, slot * GROUP_W:(slot + 1) * GROUP_W] = proj_scr[rs, group * GROUP_W:(group + 1) * GROUP_W]

    def kept(group, h):
        slot = KEPT_GROUPS.index(group)
        return slice(slot * GROUP_W + h * D_HEAD, slot * GROUP_W + (h + 1) * D_HEAD)

    def apply_states():
        zeros8 = jnp.zeros((SUBLANES, D_HEAD), F32)
        sel8 = jnp.where(lax.broadcasted_iota(jnp.int32, (SUBLANES, D_HEAD), 0) < 3, 1.0, 0.0)

        def pair_readout(lhs_ref, st_ref, j, rows, group, h):
            lhs = jnp.concatenate([lhs_ref[rows, head(h)], lhs_ref[rows, head(h + 1)]], axis=0).astype(BF16)
            w = jnp.concatenate([st_ref[j, h], st_ref[j, h + 1]], axis=1).astype(BF16)
            oo = _mm(lhs, w)
            os_out_ref[rows, col(group, h)] = os_in_ref[rows, col(group, h)] + oo[:SUBLANES, :D_HEAD]
            os_out_ref[rows, col(group, h + 1)] = os_in_ref[rows, col(group, h + 1)] + oo[SUBLANES:, D_HEAD:]

        for j in range(smp_seq_per_step):
            rows = slice(j * SUBLANES, (j + 1) * SUBLANES)
            for h in range(0, N_HEADS, 2):
                pair_readout(qe_ref, sa_in_ref, j, rows, 0, h)
                pair_readout(qi_ref, sb_in_ref, j, rows, 1, h)
            for h in range(N_HEADS):
                hs = head(h)
                lhs = jnp.concatenate([kh_ref[rows, hs], ex_ref[rows, hs]], axis=0).astype(BF16)
                v = keep_in_ref[rows, kept(2, h)]
                rhs = jnp.concatenate([jnp.concatenate([v, zeros8], axis=1),
                                       jnp.concatenate([zeros8, sel8], axis=1)], axis=0).astype(BF16)
                upd = _mm_tn(lhs, rhs)
                sas_out_ref[j, h] = sa_in_ref[j, h] * upd[:, D_HEAD:] + upd[:, :D_HEAD]
                v = keep_in_ref[rows, kept(6, h)].astype(BF16)
                sbs_out_ref[j, h] = smp_cdec[h] * sb_in_ref[j, h] + _mm_tn(ks_ref[rows, hs].astype(BF16), v)

    chunk_rows = [slice(c * C, (c + 1) * C) for c in range(chunks)]
    for rs in chunk_rows:
        project(rs)
    for rs in chunk_rows:
        score(rs)
    if not sample:
        x1_ref[...] = x_ref[...] + _mm(o_scr[...].astype(BF16), wout_ref[...])
        for wide, narrow in zip(wide_refs, narrow_refs):
            narrow[...] = wide[...].astype(BF16)
        apply_states()

        @pl.when(step == pl.num_programs(1) - 1)
        def _():
            for h in range(N_HEADS):
                sa_out_ref[0, h] = sa_scr[h].T
                sb_out_ref[0, h] = sb_scr[h]


def _full(shape):
    return pl.BlockSpec(shape, lambda *_: (0,) * len(shape))


def _row_block_spec(shape, n_steps, inner_steps):
    _, n_rows, width = shape
    share = next(k for k in (1, 2, 4, 8) if n_steps % k == 0 and n_rows % ((n_steps // k) * BF16_ROWS) == 0)
    return pl.BlockSpec((1, n_rows // (n_steps // share), width),
                        lambda b, i: (0, (b * inner_steps + i) // share, 0))


def _mix_scratch(tile):
    return [pltpu.VMEM((tile, D_MODEL), BF16), pltpu.VMEM((tile, IN_WIDTH), F32)]


def _smp_score_call(x2, weights, *, seq_len, tile):
    n_rows = x2.shape[0]
    assert n_rows % tile == 0 and tile % seq_len == 0
    nlev, cdec, consts = _chunk_consts(tile, seq_len)
    cos, sin = _rope_tables(np.tile(PAST_LEN + np.arange(seq_len), tile // seq_len))
    rows = lambda width: pl.BlockSpec((tile, width), lambda n: (n, 0))
    widths = [2 * GROUP_W] + [GROUP_W] * 5 + [len(KEPT_GROUPS) * GROUP_W]
    args = [x2, cos, sin, *weights, *consts]
    return cdec, pl.pallas_call(
        functools.partial(_mix_kernel, sample=True, tile=tile, chunks=1, seq_rows=seq_len, nlev=nlev, cdec=cdec,
                          smp_cdec=None, smp_seq_per_step=0),
        grid=(n_rows // tile,),
        in_specs=[rows(D_MODEL)] + [_full(a.shape) for a in args[1:]],
        out_specs=[rows(w) for w in widths],
        out_shape=[jax.ShapeDtypeStruct((n_rows, w), F32) for w in widths],
        scratch_shapes=_mix_scratch(tile) + [pltpu.VMEM((tile, GROUP_W), F32)] * 2,
        compiler_params=pltpu.CompilerParams(dimension_semantics=("arbitrary",), vmem_limit_bytes=VMEM_LIMIT),
        name="score_sample",
    )(*args)


def _mix_call(x2, weights, to_narrow, smp, smp_states, smp_cdec, *, n_seq, seq_len, tile, chunks, smp_len):
    n_rows = n_seq * seq_len
    assert seq_len % tile == 0 and tile % chunks == 0
    chunk = tile // chunks
    steps = seq_len // tile
    n_steps = n_seq * steps
    nlev, cdec, consts = _chunk_consts(chunk, min(seq_len, chunk))
    cos, sin = _rope_tables(np.arange(seq_len))
    n_smp = smp_states[0].shape[0]
    assert n_smp % n_steps == 0
    seq_per_step = n_smp // n_steps
    smp_rows = seq_per_step * smp_len
    assert smp_rows % SUBLANES == 0

    row_map = lambda b, i: (b * steps + i, 0)
    pos_map = lambda b, i: (i, 0)
    state_spec = pl.BlockSpec((1, N_HEADS, D_HEAD, D_HEAD), lambda b, i: (b, 0, 0, 0))
    smp_state_spec = pl.BlockSpec((seq_per_step, N_HEADS, D_HEAD, D_HEAD), lambda b, i: (b * steps + i, 0, 0, 0))
    smp_spec = lambda a: pl.BlockSpec((smp_rows, a.shape[1]), row_map)
    narrow_specs = [_row_block_spec(w.shape, n_steps, steps) for w in to_narrow]
    smp_scores, *smp_factors, smp_keep = smp
    side_in = [*smp_factors, smp_keep, smp_scores]
    in_specs = ([pl.BlockSpec((tile, D_MODEL), row_map), pl.BlockSpec((tile, D_HEAD), pos_map),
                 pl.BlockSpec((tile, D_HEAD), pos_map)] + [_full(w.shape) for w in weights]
                + [_full(c.shape) for c in consts] + narrow_specs + [smp_spec(a) for a in side_in]
                + [smp_state_spec, smp_state_spec])
    args = [x2, cos, sin, *weights, *consts, *to_narrow, *side_in, *smp_states]
    state_shape = lambda n: jax.ShapeDtypeStruct((n, N_HEADS, D_HEAD, D_HEAD), F32)
    return pl.pallas_call(
        functools.partial(_mix_kernel, sample=False, tile=tile, chunks=chunks, seq_rows=min(seq_len, chunk),
                          nlev=nlev, cdec=cdec, smp_cdec=smp_cdec, smp_seq_per_step=seq_per_step),
        grid=(n_seq, steps),
        in_specs=in_specs,
        out_specs=[pl.BlockSpec((tile, D_MODEL), row_map), state_spec, state_spec] + narrow_specs
        + [smp_spec(smp_scores), smp_state_spec, smp_state_spec],
        out_shape=[jax.ShapeDtypeStruct((n_rows, D_MODEL), F32), state_shape(n_seq), state_shape(n_seq)]
        + [jax.ShapeDtypeStruct(w.shape, BF16) for w in to_narrow]
        + [jax.ShapeDtypeStruct(smp_scores.shape, F32), state_shape(n_smp), state_shape(n_smp)],
        scratch_shapes=_mix_scratch(tile) + [pltpu.VMEM((tile, 2 * GROUP_W), F32)]
        + [pltpu.VMEM((tile, GROUP_W), F32)] * 2 + [pltpu.VMEM((N_HEADS, D_HEAD, D_HEAD), F32)] * 2,
        compiler_params=pltpu.CompilerParams(dimension_semantics=("arbitrary", "arbitrary"),
                                             vmem_limit_bytes=VMEM_LIMIT),
        name="mix_prompt",
    )(*args)


def _ffn_kernel(*refs, sample, groups, rows):
    x_ref, w2_ref, wup_ref, cw_ref, cb_ref, wdown_ref, wf_ref = refs[:7]
    refs = refs[7:]
    if sample:
        (hist_ref, os_ref, keep_ref, na_ref, nb_ref, wout_ref, y_ref, hist_out_ref, h_scr, act_scr, mixed_scr,
         xin) = refs
        for h in range(N_HEADS):
            for group, norm, w_ref in ((0, _rmsnorm, na_ref), (1, _groupnorm, nb_ref)):
                cols = slice(group * GROUP_W + h * D_HEAD, group * GROUP_W + (h + 1) * D_HEAD)
                slot = KEPT_GROUPS.index(4 * group + 3)
                gate = keep_ref[:, slot * GROUP_W + h * D_HEAD:slot * GROUP_W + (h + 1) * D_HEAD]
                mixed_scr[:, cols] = (norm(os_ref[:, cols], w_ref[...]) * (gate * _sigmoid(gate))).astype(BF16)
        xin[...] = x_ref[...] + _mm(mixed_scr[...], wout_ref[...])
    else:
        y_ref, hist_out_ref, h_scr, act_scr, tail_scr = refs
        xin = x_ref
        step = pl.program_id(1)

        @pl.when(step == 0)
        def _():
            tail_scr[...] = jnp.zeros_like(tail_scr)

    G, L, P = groups, rows, SUBLANES
    parts = 1 if sample else ROW_PARTS
    LP = L // parts
    row_id = lax.broadcasted_iota(jnp.int32, (G, P, FF_COLS), 1)

    def shifted(up, prev2, prev1):
        r1, r2 = pltpu.roll(up, 1, 1), pltpu.roll(up, 2, 1)
        top1 = jnp.where(row_id == 0, prev1, r1[:, :P])
        top2 = jnp.where(row_id == 0, prev2, jnp.where(row_id == 1, prev1, r2[:, :P]))
        if LP == P:
            return top1, top2
        return jnp.concatenate([top1, r1[:, P:]], axis=1), jnp.concatenate([top2, r2[:, P:]], axis=1)

    tails = {}
    for part in range(parts):
        rows_p = slice(part * G * LP, (part + 1) * G * LP)
        h_scr[rows_p, :] = _rmsnorm(xin[rows_p, :], w2_ref[...]).astype(BF16)
        for n in range(0, D_FF, FF_COLS):
            conv = []
            for cols in (slice(n, n + FF_COLS), slice(D_FF + n, D_FF + n + FF_COLS)):
                up = _mm(h_scr[rows_p, :], wup_ref[0, :, cols]).reshape(G, LP, FF_COLS)
                if sample:
                    prev2, prev1 = hist_ref[:, 0:1, cols], hist_ref[:, 1:2, cols]
                    hist_out_ref[:, :, cols] = up[:, LP - 2:, :]
                else:
                    if part == 0:
                        prev2, prev1 = tail_scr[:, P - 2:P - 1, cols], tail_scr[:, P - 1:P, cols]
                    else:
                        prev = tails[cols.start]
                        prev2, prev1 = prev[:, P - 2:P - 1, :], prev[:, P - 1:P, :]
                    tails[cols.start] = up[:, LP - P:, :]
                    if part == parts - 1:
                        tail_scr[:, :, cols] = up[:, LP - P:, :]
                sh1, sh2 = shifted(up, prev2, prev1)
                conv.append(cb_ref[:, cols] + cw_ref[0:1, cols] * sh2 + cw_ref[1:2, cols] * sh1 + cw_ref[2:3, cols] * up)
            u, g = (c.astype(BF16) for c in conv)
            one = jnp.ones((), BF16)
            act_scr[rows_p, n:n + FF_COLS] = ((g * (one / (one + jnp.exp(-g)))) * u).reshape(G * LP, FF_COLS)

        x2 = xin[rows_p, :] + _mm(act_scr[rows_p, :], wdown_ref[0])
        y_ref[rows_p, :] = _rmsnorm(x2, wf_ref[...])

    if not sample:
        @pl.when(step == pl.num_programs(1) - 1)
        def _():
            hist_out_ref[...] = tail_scr[:, P - 2:, :]


def _ffn_call(x, w2, w_up, cw, cb, w_down, wf, smp, *, n_seq, seq_len, tile):
    sample = smp is not None
    n_rows = n_seq * seq_len
    hist_shape = jax.ShapeDtypeStruct((n_seq, CONV_W - 1, FF2), F32)
    if sample:
        hist, scores, keep, na, nb, w_out = smp
        groups, rows = tile // seq_len, seq_len
        grid = (n_rows // tile,)
        row_map = lambda n: (n, 0)
        hist_spec = pl.BlockSpec((groups, CONV_W - 1, FF2), lambda n: (n, 0, 0))
        extra_in = [hist_spec, pl.BlockSpec((tile, scores.shape[1]), row_map),
                    pl.BlockSpec((tile, keep.shape[1]), row_map), _full(na.shape), _full(nb.shape), _full(w_out.shape)]
        extra_args = [hist, scores, keep, na, nb, w_out]
        scratch = [pltpu.VMEM((tile, 2 * GROUP_W), BF16), pltpu.VMEM((tile, D_MODEL), F32)]
    else:
        assert seq_len % tile == 0
        groups, rows = 1, tile
        steps = seq_len // tile
        grid = (n_seq, steps)
        row_map = lambda b, i: (b * steps + i, 0)
        hist_spec = pl.BlockSpec((1, CONV_W - 1, FF2), lambda b, i: (b, 0, 0))
        extra_in, extra_args = [], []
        scratch = [pltpu.VMEM((1, SUBLANES, FF2), F32)]
    in_specs = [pl.BlockSpec((tile, D_MODEL), row_map), _full(w2.shape), _full(w_up.shape), _full(cw.shape),
                _full(cb.shape), _full(w_down.shape), _full(wf.shape)] + extra_in
    args = [x, w2, w_up, cw, cb, w_down, wf] + extra_args
    return pl.pallas_call(
        functools.partial(_ffn_kernel, sample=sample, groups=groups, rows=rows),
        grid=grid,
        in_specs=in_specs,
        out_specs=[pl.BlockSpec((tile, D_MODEL), row_map), hist_spec],
        out_shape=[jax.ShapeDtypeStruct((n_rows, D_MODEL), F32), hist_shape],
        scratch_shapes=[pltpu.VMEM((tile, D_MODEL), BF16), pltpu.VMEM((tile, D_FF), BF16)] + scratch,
        compiler_params=pltpu.CompilerParams(dimension_semantics=("arbitrary",) * len(grid),
                                             vmem_limit_bytes=VMEM_LIMIT),
        name="ffn_sample" if sample else "ffn_prompt",
    )(*args)


def kernel(x_prompt, x_sample, state_hgrn, state_ret, state_conv, w_norm1, w_in, hgrn_lb, hgrn_norm_w, ret_norm_w,
           w_out, w_norm2, w_ffn_in, conv_w, conv_b, w_ffn_out, w_norm_f):
    assert w_in.shape == (1, D_MODEL, IN_WIDTH) and hgrn_lb.shape == (2, GROUP_W)
    w_out_bf = w_out[0].astype(BF16)
    mix_w = (w_norm1, w_in[0].astype(BF16), hgrn_lb, hgrn_norm_w, ret_norm_w, w_out_bf)
    n_seq, seq_len, _ = x_prompt.shape
    n_smp, smp_len, _ = x_sample.shape
    xs = x_sample.reshape(n_smp * smp_len, D_MODEL)

    smp_cdec, smp = _smp_score_call(xs, mix_w, seq_len=smp_len, tile=256)
    x1, ha_p, rb_p, w_up, w_down, smp_scores, ha_s, rb_s = _mix_call(
        x_prompt.reshape(n_seq * seq_len, D_MODEL), mix_w, (w_ffn_in, w_ffn_out), smp,
        (state_hgrn[0], state_ret[0]), smp_cdec, n_seq=n_seq, seq_len=seq_len, tile=512, chunks=2, smp_len=smp_len)
    ffn_w = (w_norm2, w_up, conv_w[0], conv_b, w_down, w_norm_f.reshape(1, D_MODEL))
    y_p, cv_p = _ffn_call(x1, *ffn_w, None, n_seq=n_seq, seq_len=seq_len, tile=512)
    y_s, cv_s = _ffn_call(xs, *ffn_w, (state_conv[0], smp_scores, smp[-1], hgrn_norm_w, ret_norm_w, w_out_bf),
                          n_seq=n_smp, seq_len=smp_len, tile=256)
    return (y_p.reshape(x_prompt.shape), y_s.reshape(x_sample.shape), ha_p[None], rb_p[None], cv_p[None],
            ha_s[None], rb_s[None], cv_s[None])
```

```python
import functools

import numpy as np
import jax
import jax.numpy as jnp
from jax import lax
from jax.experimental import pallas as pl
from jax.experimental.pallas import tpu as pltpu

F32 = jnp.float32
BF16 = jnp.bfloat16

D_MODEL = 1024
N_HEADS = 4
D_HEAD = 128
GROUP_W = N_HEADS * D_HEAD
IN_WIDTH = 8 * GROUP_W
D_FF = 2816
FF2 = 2 * D_FF
CONV_W = 3
PAST_LEN = 16384
ROPE_BASE = 10000.0
EPS = 1e-6
LOG2E = 1.4426950408889634

SUBLANES = 8
BF16_ROWS = 16
PROJ_COLS = 512
FF_COLS = 256
ROW_PARTS = 2
KEPT_GROUPS = (2, 3, 6, 7)
VMEM_LIMIT = 56 * 1024 * 1024


def _mm(a, b):
    return jnp.dot(a, b, preferred_element_type=F32)


def _mm_nt(a, b):
    return lax.dot_general(a, b, (((1,), (1,)), ((), ())), preferred_element_type=F32)


def _mm_tn(a, b):
    return lax.dot_general(a, b, (((0,), (0,)), ((), ())), preferred_element_type=F32)


def _sigmoid(x):
    return 1.0 / (1.0 + jnp.exp(-x))


def _rmsnorm(x, w):
    return x * lax.rsqrt(jnp.mean(x * x, axis=-1, keepdims=True) + EPS) * w


def _groupnorm(x, w):
    xc = x - jnp.mean(x, axis=-1, keepdims=True)
    return xc * lax.rsqrt(jnp.mean(xc * xc, axis=-1, keepdims=True) + EPS) * w


def _chunk_consts(chunk, seq_len):
    nlev = int(np.log2(seq_len))
    assert 1 << nlev == seq_len and chunk % seq_len == 0
    r = np.arange(chunk)
    rr, cc = r[:, None], r[None, :]
    same_seq = (rr // seq_len) == (cc // seq_len)
    cum = (same_seq & (cc <= rr)).astype(np.float32)
    x = rr ^ cc
    bit_len = np.where(x > 0, np.floor(np.log2(np.maximum(x, 1))).astype(np.int64) + 1, 0)
    level = np.where(same_seq & (cc <= rr), bit_len, -1).astype(np.int32)

    pos = r % seq_len
    log_gamma = np.log1p(-np.exp2(-5.0 - np.arange(N_HEADS, dtype=np.float64)))[:, None, None]
    rel = (pos[:, None] - pos[None, :]).astype(np.float64)[None]
    causal = (same_seq & (cc <= rr))[None]
    dec = np.where(causal, np.exp(np.where(causal, rel, 0.0) * log_gamma), 0.0)
    ones = np.ones((1, 1, D_HEAD))
    inner = np.exp((pos + 1.0)[None, :, None] * log_gamma) * ones
    sdec = np.exp((seq_len - 1.0 - pos)[None, :, None] * log_gamma) * ones
    cdec = tuple(float(v) for v in np.exp(seq_len * log_gamma[:, 0, 0]))
    consts = (jnp.asarray(cum, BF16), jnp.asarray(level), jnp.asarray(dec, F32), jnp.asarray(inner, F32),
              jnp.asarray(sdec, F32))
    return nlev, cdec, consts


def _rope_tables(pos):
    half = D_HEAD // 2
    inv = 1.0 / (ROPE_BASE ** (np.arange(half, dtype=np.float64) / half))
    ang = np.asarray(pos, np.float64)[:, None] * inv[None, :]
    cos, sin = np.cos(ang), np.sin(ang)
    return (jnp.asarray(np.concatenate([cos, cos], axis=-1), F32),
            jnp.asarray(np.concatenate([-sin, sin], axis=-1), F32))


def _mix_kernel(*refs, sample, tile, chunks, seq_rows, nlev, cdec, smp_cdec, smp_seq_per_step):
    (x_ref, cos_ref, sin_ref, w1_ref, win_ref, lbp_ref, na_ref, nb_ref, wout_ref,
     cum_ref, level_ref, dec_ref, inner_ref, sdec_ref) = refs[:14]
    refs = refs[14:]
    if sample:
        o_scr, qe_scr, kh_scr, ex_scr, qi_scr, ks_scr, keep_ref, win_bf_ref, wout_bf_ref = refs[:9]
        h_scr, proj_scr, d_scr, k_scr = refs[9:]

        @pl.when(pl.program_id(0) == 0)
        def _():
            for n in range(0, IN_WIDTH, PROJ_COLS):
                win_bf_ref[:, n:n + PROJ_COLS] = win_ref[0, :, n:n + PROJ_COLS].astype(BF16)
            wout_bf_ref[...] = wout_ref[0].astype(BF16)

        win_ref = win_bf_ref
    else:
        wide_refs = refs[:2]
        qe_ref, kh_ref, ex_ref, qi_ref, ks_ref, keep_in_ref, os_in_ref, sa_in_ref, sb_in_ref = refs[2:11]
        x1_ref, sa_out_ref, sb_out_ref = refs[11:14]
        narrow_refs = refs[14:16]
        os_out_ref, sas_out_ref, sbs_out_ref = refs[16:19]
        h_scr, proj_scr, o_scr, d_scr, k_scr, sa_scr, sb_scr = refs[19:]
        step = pl.program_id(1)

        @pl.when(step == 0)
        def _():
            sa_scr[...] = jnp.zeros_like(sa_scr)
            sb_scr[...] = jnp.zeros_like(sb_scr)

    C = tile // chunks

    def col(group, h):
        return slice(group * GROUP_W + h * D_HEAD, group * GROUP_W + (h + 1) * D_HEAD)

    def head(h):
        return slice(h * D_HEAD, (h + 1) * D_HEAD)

    def block_rows(x, m, row):
        x3 = x.reshape(C // m, m, D_HEAD)
        return jnp.broadcast_to(x3[:, row:row + 1, :], x3.shape).reshape(C, D_HEAD)

    def upper_rows(x, m):
        return x.reshape(C // m, 2, m // 2, x.shape[-1])[:, 1].reshape(C // 2, x.shape[-1])

    def put_upper_rows(x, xu, m):
        x4 = x.reshape(C // m, 2, m // 2, x.shape[-1])
        xu4 = xu.reshape(C // m, 1, m // 2, x.shape[-1])
        return jnp.concatenate([x4[:, 0:1], xu4], axis=1).reshape(C, x.shape[-1])

    def project(rs):
        h_scr[rs, :] = _rmsnorm(x_ref[rs, :], w1_ref[...]).astype(BF16)
        for n in range(0, IN_WIDTH, PROJ_COLS):
            proj_scr[rs, n:n + PROJ_COLS] = _mm(h_scr[rs, :], win_ref[:, n:n + PROJ_COLS])

    def score(rs):
        lb0, lb1 = lbp_ref[0:1, :], lbp_ref[1:2, :]
        lb_max = jnp.maximum(lb0, lb1)
        e0, e1 = jnp.exp(lb0 - lb_max), jnp.exp(lb1 - lb_max)
        lb = e0 / (e0 + e1)

        row_id = lax.broadcasted_iota(jnp.int32, (C, D_HEAD), 0)

        f = lb + (1.0 - lb) * _sigmoid(proj_scr[rs, GROUP_W:2 * GROUP_W])
        k_scr[rs, :] = 1.0 - f
        g = jnp.log(f)
        g_hi = g.astype(BF16)
        g_lo = (g - g_hi.astype(F32)).astype(BF16)
        d_scr[rs, :] = _mm(cum_ref[...], g_hi) + _mm(cum_ref[...], g_lo)

        signs = [jnp.where((row_id & (1 << (lev - 1))) != 0, LOG2E, -LOG2E) for lev in range(3, nlev + 1)]

        for h in range(N_HEADS):
            hs = head(h)
            q = proj_scr[rs, col(0, h)]
            k = k_scr[rs, hs]
            v = proj_scr[rs, col(2, h)].astype(BF16)
            b = d_scr[rs, hs]
            a = jnp.where(level_ref[...] == 0, _mm_nt(q.astype(BF16), k.astype(BF16)), 0.0)
            for lev in range(1, nlev + 1):
                m = 1 << lev
                upper = (row_id & (m // 2)) != 0
                if lev == 1:
                    z = jnp.where(upper, q * (1.0 - k), k)
                elif lev == 2:
                    fh = 1.0 - k
                    pos4 = row_id & 3
                    decay = jnp.where(pos4 == 0, pltpu.roll(fh, C - 1, 0),
                                      jnp.where(pos4 == 1, 1.0, jnp.where(pos4 == 2, fh, fh * pltpu.roll(fh, 1, 0))))
                    z = jnp.where(upper, q, k) * decay
                else:
                    z = jnp.where(upper, q, k) * jnp.exp2((b - block_rows(b, m, m // 2 - 1)) * signs[lev - 3])
                if m < 2 * SUBLANES:
                    z = z.astype(BF16)
                    a = jnp.where(level_ref[...] == lev, _mm_nt(z, z), a)
                else:
                    zq = upper_rows(z, m).astype(BF16)
                    zk = z.astype(BF16)
                    if m > D_HEAD:
                        p = [_mm_nt(zq[i * (m // 2):(i + 1) * (m // 2)], zk[i * m:i * m + m // 2])
                             for i in range(C // m)]
                        width = m // 2
                    else:
                        p = [_mm_nt(zq[i * (D_HEAD // 2):(i + 1) * (D_HEAD // 2)], zk[i * D_HEAD:(i + 1) * D_HEAD])
                             for i in range(C // D_HEAD)]
                        width = D_HEAD
                    full = jnp.concatenate([jnp.concatenate([pi] * (C // width), axis=1) for pi in p], axis=0)
                    lvl_u = upper_rows(level_ref[...], m)
                    a = put_upper_rows(a, jnp.where(lvl_u == lev, full, upper_rows(a, m)), m)
            o = _mm(a.astype(BF16), v)
            eb = jnp.exp(b)
            qe = q * eb
            kh = k * jnp.exp(block_rows(b, seq_rows, seq_rows - 1) - b)
            if sample:
                o_scr[rs, col(0, h)] = o
                qe_scr[rs, hs] = qe
                kh_scr[rs, hs] = kh
                e_all = block_rows(eb, seq_rows, seq_rows - 1)
                e_hi = e_all.astype(BF16).astype(F32)
                e_mid = (e_all - e_hi).astype(BF16).astype(F32)
                e_lo = e_all - e_hi - e_mid
                pos = row_id & (seq_rows - 1)
                ex_scr[rs, hs] = jnp.where(pos == 0, e_hi, jnp.where(pos == 1, e_mid, jnp.where(pos == 2, e_lo, 0.0)))
            else:
                st = sa_scr[h]
                o = o + _mm_nt(qe.astype(BF16), st.astype(BF16))
                sa_scr[h] = st * eb[C - 1:C, :] + _mm_tn(v, kh.astype(BF16))
                gate = proj_scr[rs, col(3, h)]
                o_scr[rs, col(0, h)] = _rmsnorm(o, na_ref[...]) * (gate * _sigmoid(gate))

        cos, sin = cos_ref[rs, :], sin_ref[rs, :]
        for h in range(N_HEADS):
            hs = head(h)
            q = proj_scr[rs, col(4, h)]
            k = proj_scr[rs, col(5, h)]
            v = proj_scr[rs, col(6, h)].astype(BF16)
            qr = q * cos + pltpu.roll(q, D_HEAD // 2, 1) * sin
            kr = (k * cos + pltpu.roll(k, D_HEAD // 2, 1) * sin) * (D_HEAD ** -0.5)
            a = _mm_nt(qr.astype(BF16), kr.astype(BF16)) * dec_ref[h]
            o = _mm(a.astype(BF16), v)
            qi = qr * inner_ref[h]
            ks = kr * sdec_ref[h]
            if sample:
                o_scr[rs, col(1, h)] = o
                qi_scr[rs, hs] = qi
                ks_scr[rs, hs] = ks
            else:
                st = sb_scr[h]
                o = o + _mm(qi.astype(BF16), st.astype(BF16))
                sb_scr[h] = cdec[h] * st + _mm_tn(ks.astype(BF16), v)
                gate = proj_scr[rs, col(7, h)]
                o_scr[rs, col(1, h)] = _groupnorm(o, nb_ref[...]) * (gate * _sigmoid(gate))

        if sample:
            for slot, group in enumerate(KEPT_GROUPS):
                keep_ref[:, slot * GROUP_W:(slot + 1) * GROUP_W] = proj_scr[rs, group * GROUP_W:(group + 1) * GROUP_W]

    def kept(group, h):
        slot = KEPT_GROUPS.index(group)
        return slice(slot * GROUP_W + h * D_HEAD, slot * GROUP_W + (h + 1) * D_HEAD)

    def apply_states():
        zeros8 = jnp.zeros((SUBLANES, D_HEAD), F32)
        sel8 = jnp.where(lax.broadcasted_iota(jnp.int32, (SUBLANES, D_HEAD), 0) < 3, 1.0, 0.0)

        def pair_readout(lhs_ref, st_ref, j, rows, group, h):
            lhs = jnp.concatenate([lhs_ref[rows, head(h)], lhs_ref[rows, head(h + 1)]], axis=0).astype(BF16)
            w = jnp.concatenate([st_ref[j, h], st_ref[j, h + 1]], axis=1).astype(BF16)
            oo = _mm(lhs, w)
            os_out_ref[rows, col(group, h)] = os_in_ref[rows, col(group, h)] + oo[:SUBLANES, :D_HEAD]
            os_out_ref[rows, col(group, h + 1)] = os_in_ref[rows, col(group, h + 1)] + oo[SUBLANES:, D_HEAD:]

        for j in range(smp_seq_per_step):
            rows = slice(j * SUBLANES, (j + 1) * SUBLANES)
            for h in range(0, N_HEADS, 2):
                pair_readout(qe_ref, sa_in_ref, j, rows, 0, h)
                pair_readout(qi_ref, sb_in_ref, j, rows, 1, h)
            for h in range(N_HEADS):
                hs = head(h)
                lhs = jnp.concatenate([kh_ref[rows, hs], ex_ref[rows, hs]], axis=0).astype(BF16)
                v = keep_in_ref[rows, kept(2, h)]
                rhs = jnp.concatenate([jnp.concatenate([v, zeros8], axis=1),
                                       jnp.concatenate([zeros8, sel8], axis=1)], axis=0).astype(BF16)
                upd = _mm_tn(lhs, rhs)
                sas_out_ref[j, h] = sa_in_ref[j, h] * upd[:, D_HEAD:] + upd[:, :D_HEAD]
                v = keep_in_ref[rows, kept(6, h)].astype(BF16)
                sbs_out_ref[j, h] = smp_cdec[h] * sb_in_ref[j, h] + _mm_tn(ks_ref[rows, hs].astype(BF16), v)

    chunk_rows = [slice(c * C, (c + 1) * C) for c in range(chunks)]
    for rs in chunk_rows:
        project(rs)
    for rs in chunk_rows:
        score(rs)
    if not sample:
        x1_ref[...] = x_ref[...] + _mm(o_scr[...].astype(BF16), wout_ref[...])
        for wide, narrow in zip(wide_refs, narrow_refs):
            narrow[...] = wide[...].astype(BF16)
        apply_states()

        @pl.when(step == pl.num_programs(1) - 1)
        def _():
            for h in range(N_HEADS):
                sa_out_ref[0, h] = sa_scr[h].T
                sb_out_ref[0, h] = sb_scr[h]


def _full(shape):
    return pl.BlockSpec(shape, lambda *_: (0,) * len(shape))


def _row_block_spec(shape, n_steps, inner_steps):
    _, n_rows, width = shape
    share = next(k for k in (1, 2, 4, 8) if n_steps % k == 0 and n_rows % ((n_steps // k) * BF16_ROWS) == 0)
    return pl.BlockSpec((1, n_rows // (n_steps // share), width),
                        lambda b, i: (0, (b * inner_steps + i) // share, 0))


def _mix_scratch(tile):
    return [pltpu.VMEM((tile, D_MODEL), BF16), pltpu.VMEM((tile, IN_WIDTH), F32)]


def _smp_score_call(x2, weights, *, seq_len, tile):
    n_rows = x2.shape[0]
    assert n_rows % tile == 0 and tile % seq_len == 0
    nlev, cdec, consts = _chunk_consts(tile, seq_len)
    cos, sin = _rope_tables(np.tile(PAST_LEN + np.arange(seq_len), tile // seq_len))
    rows = lambda width: pl.BlockSpec((tile, width), lambda n: (n, 0))
    widths = [2 * GROUP_W] + [GROUP_W] * 5 + [len(KEPT_GROUPS) * GROUP_W]
    args = [x2, cos, sin, *weights, *consts]
    w_in, w_out = weights[1], weights[5]
    narrow_shapes = [w.shape[1:] for w in (w_in, w_out)]
    *outs, w_in_bf, w_out_bf = pl.pallas_call(
        functools.partial(_mix_kernel, sample=True, tile=tile, chunks=1, seq_rows=seq_len, nlev=nlev, cdec=cdec,
                          smp_cdec=None, smp_seq_per_step=0),
        grid=(n_rows // tile,),
        in_specs=[rows(D_MODEL)] + [_full(a.shape) for a in args[1:]],
        out_specs=[rows(w) for w in widths] + [_full(shape) for shape in narrow_shapes],
        out_shape=[jax.ShapeDtypeStruct((n_rows, w), F32) for w in widths]
        + [jax.ShapeDtypeStruct(shape, BF16) for shape in narrow_shapes],
        scratch_shapes=_mix_scratch(tile) + [pltpu.VMEM((tile, GROUP_W), F32)] * 2,
        compiler_params=pltpu.CompilerParams(dimension_semantics=("arbitrary",), vmem_limit_bytes=VMEM_LIMIT),
        name="score_sample",
    )(*args)
    return cdec, outs, w_in_bf, w_out_bf


def _mix_call(x2, weights, to_narrow, smp, smp_states, smp_cdec, *, n_seq, seq_len, tile, chunks, smp_len):
    n_rows = n_seq * seq_len
    assert seq_len % tile == 0 and tile % chunks == 0
    chunk = tile // chunks
    steps = seq_len // tile
    n_steps = n_seq * steps
    nlev, cdec, consts = _chunk_consts(chunk, min(seq_len, chunk))
    cos, sin = _rope_tables(np.arange(seq_len))
    n_smp = smp_states[0].shape[0]
    assert n_smp % n_steps == 0
    seq_per_step = n_smp // n_steps
    smp_rows = seq_per_step * smp_len
    assert smp_rows % SUBLANES == 0

    row_map = lambda b, i: (b * steps + i, 0)
    pos_map = lambda b, i: (i, 0)
    state_spec = pl.BlockSpec((1, N_HEADS, D_HEAD, D_HEAD), lambda b, i: (b, 0, 0, 0))
    smp_state_spec = pl.BlockSpec((seq_per_step, N_HEADS, D_HEAD, D_HEAD), lambda b, i: (b * steps + i, 0, 0, 0))
    smp_spec = lambda a: pl.BlockSpec((smp_rows, a.shape[1]), row_map)
    narrow_specs = [_row_block_spec(w.shape, n_steps, steps) for w in to_narrow]
    smp_scores, *smp_factors, smp_keep = smp
    side_in = [*smp_factors, smp_keep, smp_scores]
    in_specs = ([pl.BlockSpec((tile, D_MODEL), row_map), pl.BlockSpec((tile, D_HEAD), pos_map),
                 pl.BlockSpec((tile, D_HEAD), pos_map)] + [_full(w.shape) for w in weights]
                + [_full(c.shape) for c in consts] + narrow_specs + [smp_spec(a) for a in side_in]
                + [smp_state_spec, smp_state_spec])
    args = [x2, cos, sin, *weights, *consts, *to_narrow, *side_in, *smp_states]
    state_shape = lambda n: jax.ShapeDtypeStruct((n, N_HEADS, D_HEAD, D_HEAD), F32)
    return pl.pallas_call(
        functools.partial(_mix_kernel, sample=False, tile=tile, chunks=chunks, seq_rows=min(seq_len, chunk),
                          nlev=nlev, cdec=cdec, smp_cdec=smp_cdec, smp_seq_per_step=seq_per_step),
        grid=(n_seq, steps),
        in_specs=in_specs,
        out_specs=[pl.BlockSpec((tile, D_MODEL), row_map), state_spec, state_spec] + narrow_specs
        + [smp_spec(smp_scores), smp_state_spec, smp_state_spec],
        out_shape=[jax.ShapeDtypeStruct((n_rows, D_MODEL), F32), state_shape(n_seq), state_shape(n_seq)]
        + [jax.ShapeDtypeStruct(w.shape, BF16) for w in to_narrow]
        + [jax.ShapeDtypeStruct(smp_scores.shape, F32), state_shape(n_smp), state_shape(n_smp)],
        scratch_shapes=_mix_scratch(tile) + [pltpu.VMEM((tile, 2 * GROUP_W), F32)]
        + [pltpu.VMEM((tile, GROUP_W), F32)] * 2 + [pltpu.VMEM((N_HEADS, D_HEAD, D_HEAD), F32)] * 2,
        compiler_params=pltpu.CompilerParams(dimension_semantics=("arbitrary", "arbitrary"),
                                             vmem_limit_bytes=VMEM_LIMIT),
        name="mix_prompt",
    )(*args)


def _ffn_kernel(*refs, sample, groups, rows):
    x_ref, w2_ref, wup_ref, cw_ref, cb_ref, wdown_ref, wf_ref = refs[:7]
    refs = refs[7:]
    if sample:
        (hist_ref, os_ref, keep_ref, na_ref, nb_ref, wout_ref, y_ref, hist_out_ref, h_scr, act_scr, mixed_scr,
         xin) = refs
        for h in range(N_HEADS):
            for group, norm, w_ref in ((0, _rmsnorm, na_ref), (1, _groupnorm, nb_ref)):
                cols = slice(group * GROUP_W + h * D_HEAD, group * GROUP_W + (h + 1) * D_HEAD)
                slot = KEPT_GROUPS.index(4 * group + 3)
                gate = keep_ref[:, slot * GROUP_W + h * D_HEAD:slot * GROUP_W + (h + 1) * D_HEAD]
                mixed_scr[:, cols] = (norm(os_ref[:, cols], w_ref[...]) * (gate * _sigmoid(gate))).astype(BF16)
        xin[...] = x_ref[...] + _mm(mixed_scr[...], wout_ref[...])
    else:
        y_ref, hist_out_ref, h_scr, act_scr, tail_scr = refs
        xin = x_ref
        step = pl.program_id(1)

        @pl.when(step == 0)
        def _():
            tail_scr[...] = jnp.zeros_like(tail_scr)

    G, L, P = groups, rows, SUBLANES
    parts = 1 if sample else ROW_PARTS
    LP = L // parts
    row_id = lax.broadcasted_iota(jnp.int32, (G, P, FF_COLS), 1)

    def shifted(up, prev2, prev1):
        r1, r2 = pltpu.roll(up, 1, 1), pltpu.roll(up, 2, 1)
        top1 = jnp.where(row_id == 0, prev1, r1[:, :P])
        top2 = jnp.where(row_id == 0, prev2, jnp.where(row_id == 1, prev1, r2[:, :P]))
        if LP == P:
            return top1, top2
        return jnp.concatenate([top1, r1[:, P:]], axis=1), jnp.concatenate([top2, r2[:, P:]], axis=1)

    tails = {}
    for part in range(parts):
        rows_p = slice(part * G * LP, (part + 1) * G * LP)
        h_scr[rows_p, :] = _rmsnorm(xin[rows_p, :], w2_ref[...]).astype(BF16)
        for n in range(0, D_FF, FF_COLS):
            conv = []
            for cols in (slice(n, n + FF_COLS), slice(D_FF + n, D_FF + n + FF_COLS)):
                up = _mm(h_scr[rows_p, :], wup_ref[0, :, cols]).reshape(G, LP, FF_COLS)
                if sample:
                    prev2, prev1 = hist_ref[:, 0:1, cols], hist_ref[:, 1:2, cols]
                    hist_out_ref[:, :, cols] = up[:, LP - 2:, :]
                else:
                    if part == 0:
                        prev2, prev1 = tail_scr[:, P - 2:P - 1, cols], tail_scr[:, P - 1:P, cols]
                    else:
                        prev = tails[cols.start]
                        prev2, prev1 = prev[:, P - 2:P - 1, :], prev[:, P - 1:P, :]
                    tails[cols.start] = up[:, LP - P:, :]
                    if part == parts - 1:
                        tail_scr[:, :, cols] = up[:, LP - P:, :]
                sh1, sh2 = shifted(up, prev2, prev1)
                conv.append(cb_ref[:, cols] + cw_ref[0:1, cols] * sh2 + cw_ref[1:2, cols] * sh1 + cw_ref[2:3, cols] * up)
            u, g = (c.astype(BF16) for c in conv)
            one = jnp.ones((), BF16)
            act_scr[rows_p, n:n + FF_COLS] = ((g * (one / (one + jnp.exp(-g)))) * u).reshape(G * LP, FF_COLS)

        x2 = xin[rows_p, :] + _mm(act_scr[rows_p, :], wdown_ref[0])
        y_ref[rows_p, :] = _rmsnorm(x2, wf_ref[...])

    if not sample:
        @pl.when(step == pl.num_programs(1) - 1)
        def _():
            hist_out_ref[...] = tail_scr[:, P - 2:, :]


def _ffn_call(x, w2, w_up, cw, cb, w_down, wf, smp, *, n_seq, seq_len, tile):
    sample = smp is not None
    n_rows = n_seq * seq_len
    hist_shape = jax.ShapeDtypeStruct((n_seq, CONV_W - 1, FF2), F32)
    if sample:
        hist, scores, keep, na, nb, w_out = smp
        groups, rows = tile // seq_len, seq_len
        grid = (n_rows // tile,)
        row_map = lambda n: (n, 0)
        hist_spec = pl.BlockSpec((groups, CONV_W - 1, FF2), lambda n: (n, 0, 0))
        extra_in = [hist_spec, pl.BlockSpec((tile, scores.shape[1]), row_map),
                    pl.BlockSpec((tile, keep.shape[1]), row_map), _full(na.shape), _full(nb.shape), _full(w_out.shape)]
        extra_args = [hist, scores, keep, na, nb, w_out]
        scratch = [pltpu.VMEM((tile, 2 * GROUP_W), BF16), pltpu.VMEM((tile, D_MODEL), F32)]
    else:
        assert seq_len % tile == 0
        groups, rows = 1, tile
        steps = seq_len // tile
        grid = (n_seq, steps)
        row_map = lambda b, i: (b * steps + i, 0)
        hist_spec = pl.BlockSpec((1, CONV_W - 1, FF2), lambda b, i: (b, 0, 0))
        extra_in, extra_args = [], []
        scratch = [pltpu.VMEM((1, SUBLANES, FF2), F32)]
    in_specs = [pl.BlockSpec((tile, D_MODEL), row_map), _full(w2.shape), _full(w_up.shape), _full(cw.shape),
                _full(cb.shape), _full(w_down.shape), _full(wf.shape)] + extra_in
    args = [x, w2, w_up, cw, cb, w_down, wf] + extra_args
    return pl.pallas_call(
        functools.partial(_ffn_kernel, sample=sample, groups=groups, rows=rows),
        grid=grid,
        in_specs=in_specs,
        out_specs=[pl.BlockSpec((tile, D_MODEL), row_map), hist_spec],
        out_shape=[jax.ShapeDtypeStruct((n_rows, D_MODEL), F32), hist_shape],
        scratch_shapes=[pltpu.VMEM((tile, D_MODEL), BF16), pltpu.VMEM((tile, D_FF), BF16)] + scratch,
        compiler_params=pltpu.CompilerParams(dimension_semantics=("arbitrary",) * len(grid),
                                             vmem_limit_bytes=VMEM_LIMIT),
        name="ffn_sample" if sample else "ffn_prompt",
    )(*args)


def kernel(x_prompt, x_sample, state_hgrn, state_ret, state_conv, w_norm1, w_in, hgrn_lb, hgrn_norm_w, ret_norm_w,
           w_out, w_norm2, w_ffn_in, conv_w, conv_b, w_ffn_out, w_norm_f):
    assert w_in.shape == (1, D_MODEL, IN_WIDTH) and hgrn_lb.shape == (2, GROUP_W)
    n_seq, seq_len, _ = x_prompt.shape
    n_smp, smp_len, _ = x_sample.shape
    xs = x_sample.reshape(n_smp * smp_len, D_MODEL)

    smp_cdec, smp, w_in_bf, w_out_bf = _smp_score_call(
        xs, (w_norm1, w_in, hgrn_lb, hgrn_norm_w, ret_norm_w, w_out), seq_len=smp_len, tile=256)
    mix_w = (w_norm1, w_in_bf, hgrn_lb, hgrn_norm_w, ret_norm_w, w_out_bf)
    x1, ha_p, rb_p, w_up, w_down, smp_scores, ha_s, rb_s = _mix_call(
        x_prompt.reshape(n_seq * seq_len, D_MODEL), mix_w, (w_ffn_in, w_ffn_out), smp,
        (state_hgrn[0], state_ret[0]), smp_cdec, n_seq=n_seq, seq_len=seq_len, tile=512, chunks=2, smp_len=smp_len)
    ffn_w = (w_norm2, w_up, conv_w[0], conv_b, w_down, w_norm_f.reshape(1, D_MODEL))
    y_p, cv_p = _ffn_call(x1, *ffn_w, None, n_seq=n_seq, seq_len=seq_len, tile=512)
    y_s, cv_s = _ffn_call(xs, *ffn_w, (state_conv[0], smp_scores, smp[-1], hgrn_norm_w, ret_norm_w, w_out_bf),
                          n_seq=n_smp, seq_len=smp_len, tile=256)
    return (y_p.reshape(x_prompt.shape), y_s.reshape(x_sample.shape), ha_p[None], rb_p[None], cv_p[None],
            ha_s[None], rb_s[None], cv_s[None])
```

```python
import functools

import numpy as np
import jax
import jax.numpy as jnp
from jax import lax
from jax.experimental import pallas as pl
from jax.experimental.pallas import tpu as pltpu

F32 = jnp.float32
BF16 = jnp.bfloat16

D_MODEL = 1024
N_HEADS = 4
D_HEAD = 128
GROUP_W = N_HEADS * D_HEAD
IN_WIDTH = 8 * GROUP_W
D_FF = 2816
FF2 = 2 * D_FF
CONV_W = 3
PAST_LEN = 16384
ROPE_BASE = 10000.0
EPS = 1e-6
LOG2E = 1.4426950408889634

SUBLANES = 8
BF16_ROWS = 16
PROJ_COLS = 512
FF_COLS = 256
ROW_PARTS = 2
KEPT_GROUPS = (2, 3, 6, 7)
VMEM_LIMIT = 56 * 1024 * 1024


def _mm(a, b):
    return jnp.dot(a, b, preferred_element_type=F32)


def _mm_nt(a, b):
    return lax.dot_general(a, b, (((1,), (1,)), ((), ())), preferred_element_type=F32)


def _mm_tn(a, b):
    return lax.dot_general(a, b, (((0,), (0,)), ((), ())), preferred_element_type=F32)


def _sigmoid(x):
    return 1.0 / (1.0 + jnp.exp(-x))


def _rmsnorm(x, w):
    return x * lax.rsqrt(jnp.mean(x * x, axis=-1, keepdims=True) + EPS) * w


def _groupnorm(x, w):
    xc = x - jnp.mean(x, axis=-1, keepdims=True)
    return xc * lax.rsqrt(jnp.mean(xc * xc, axis=-1, keepdims=True) + EPS) * w


def _chunk_consts(chunk, seq_len):
    nlev = int(np.log2(seq_len))
    assert 1 << nlev == seq_len and chunk % seq_len == 0
    r = np.arange(chunk)
    rr, cc = r[:, None], r[None, :]
    same_seq = (rr // seq_len) == (cc // seq_len)
    cum = (same_seq & (cc <= rr)).astype(np.float32)
    x = rr ^ cc
    bit_len = np.where(x > 0, np.floor(np.log2(np.maximum(x, 1))).astype(np.int64) + 1, 0)
    level = np.where(same_seq & (cc <= rr), bit_len, -1).astype(np.int32)

    pos = r % seq_len
    log_gamma = np.log1p(-np.exp2(-5.0 - np.arange(N_HEADS, dtype=np.float64)))[:, None, None]
    rel = (pos[:, None] - pos[None, :]).astype(np.float64)[None]
    causal = (same_seq & (cc <= rr))[None]
    dec = np.where(causal, np.exp(np.where(causal, rel, 0.0) * log_gamma), 0.0)
    ones = np.ones((1, 1, D_HEAD))
    inner = np.exp((pos + 1.0)[None, :, None] * log_gamma) * ones
    sdec = np.exp((seq_len - 1.0 - pos)[None, :, None] * log_gamma) * ones
    cdec = tuple(float(v) for v in np.exp(seq_len * log_gamma[:, 0, 0]))
    consts = (jnp.asarray(cum, BF16), jnp.asarray(level), jnp.asarray(dec, F32), jnp.asarray(inner, F32),
              jnp.asarray(sdec, F32))
    return nlev, cdec, consts


def _rope_tables(pos):
    half = D_HEAD // 2
    inv = 1.0 / (ROPE_BASE ** (np.arange(half, dtype=np.float64) / half))
    ang = np.asarray(pos, np.float64)[:, None] * inv[None, :]
    cos, sin = np.cos(ang), np.sin(ang)
    return (jnp.asarray(np.concatenate([cos, cos], axis=-1), F32),
            jnp.asarray(np.concatenate([-sin, sin], axis=-1), F32))


def _mix_kernel(*refs, sample, tile, chunks, seq_rows, nlev, cdec, smp_cdec, smp_seq_per_step):
    (x_ref, cos_ref, sin_ref, w1_ref, win_ref, lbp_ref, na_ref, nb_ref, wout_ref,
     cum_ref, level_ref, dec_ref, inner_ref, sdec_ref) = refs[:14]
    refs = refs[14:]
    if sample:
        o_scr, qe_scr, kh_scr, ex_scr, qi_scr, ks_scr, keep_ref, win_bf_ref, wout_bf_ref = refs[:9]
        h_scr, proj_scr, d_scr, k_scr = refs[9:]

        @pl.when(pl.program_id(0) == 0)
        def _():
            for n in range(0, IN_WIDTH, PROJ_COLS):
                win_bf_ref[:, n:n + PROJ_COLS] = win_ref[0, :, n:n + PROJ_COLS].astype(BF16)
            wout_bf_ref[...] = wout_ref[0].astype(BF16)

        win_ref = win_bf_ref
    else:
        wide_refs = refs[:2]
        qe_ref, kh_ref, ex_ref, qi_ref, ks_ref, keep_in_ref, os_in_ref, sa_in_ref, sb_in_ref = refs[2:11]
        x1_ref, sa_out_ref, sb_out_ref = refs[11:14]
        narrow_refs = refs[14:16]
        os_out_ref, sas_out_ref, sbs_out_ref = refs[16:19]
        h_scr, proj_scr, o_scr, d_scr, k_scr, sa_scr, sb_scr = refs[19:]
        step = pl.program_id(1)

        @pl.when(step == 0)
        def _():
            sa_scr[...] = jnp.zeros_like(sa_scr)
            sb_scr[...] = jnp.zeros_like(sb_scr)

    C = tile // chunks

    def col(group, h):
        return slice(group * GROUP_W + h * D_HEAD, group * GROUP_W + (h + 1) * D_HEAD)

    def head(h):
        return slice(h * D_HEAD, (h + 1) * D_HEAD)

    def block_rows(x, m, row):
        x3 = x.reshape(C // m, m, D_HEAD)
        return jnp.broadcast_to(x3[:, row:row + 1, :], x3.shape).reshape(C, D_HEAD)

    def upper_rows(x, m):
        return x.reshape(C // m, 2, m // 2, x.shape[-1])[:, 1].reshape(C // 2, x.shape[-1])

    def put_upper_rows(x, xu, m):
        x4 = x.reshape(C // m, 2, m // 2, x.shape[-1])
        xu4 = xu.reshape(C // m, 1, m // 2, x.shape[-1])
        return jnp.concatenate([x4[:, 0:1], xu4], axis=1).reshape(C, x.shape[-1])

    def project(rs):
        h_scr[rs, :] = _rmsnorm(x_ref[rs, :], w1_ref[...]).astype(BF16)
        for n in range(0, IN_WIDTH, PROJ_COLS):
            proj_scr[rs, n:n + PROJ_COLS] = _mm(h_scr[rs, :], win_ref[:, n:n + PROJ_COLS])

    def score(rs):
        lb0, lb1 = lbp_ref[0:1, :], lbp_ref[1:2, :]
        lb_max = jnp.maximum(lb0, lb1)
        e0, e1 = jnp.exp(lb0 - lb_max), jnp.exp(lb1 - lb_max)
        lb = e0 / (e0 + e1)

        row_id = lax.broadcasted_iota(jnp.int32, (C, D_HEAD), 0)

        f = lb + (1.0 - lb) * _sigmoid(proj_scr[rs, GROUP_W:2 * GROUP_W])
        k_scr[rs, :] = 1.0 - f
        g = jnp.log(f)
        g_hi = g.astype(BF16)
        g_lo = (g - g_hi.astype(F32)).astype(BF16)
        d_scr[rs, :] = _mm(cum_ref[...], g_hi) + _mm(cum_ref[...], g_lo)

        signs = [jnp.where((row_id & (1 << (lev - 1))) != 0, LOG2E, -LOG2E) for lev in range(3, nlev + 1)]

        for h in range(N_HEADS):
            hs = head(h)
            q = proj_scr[rs, col(0, h)]
            k = k_scr[rs, hs]
            v = proj_scr[rs, col(2, h)].astype(BF16)
            b = d_scr[rs, hs]
            a = jnp.where(level_ref[...] == 0, _mm_nt(q.astype(BF16), k.astype(BF16)), 0.0)
            for lev in range(1, nlev + 1):
                m = 1 << lev
                upper = (row_id & (m // 2)) != 0
                if lev == 1:
                    z = jnp.where(upper, q * (1.0 - k), k)
                elif lev == 2:
                    fh = 1.0 - k
                    pos4 = row_id & 3
                    decay = jnp.where(pos4 == 0, pltpu.roll(fh, C - 1, 0),
                                      jnp.where(pos4 == 1, 1.0, jnp.where(pos4 == 2, fh, fh * pltpu.roll(fh, 1, 0))))
                    z = jnp.where(upper, q, k) * decay
                else:
                    z = jnp.where(upper, q, k) * jnp.exp2((b - block_rows(b, m, m // 2 - 1)) * signs[lev - 3])
                if m < 2 * SUBLANES:
                    z = z.astype(BF16)
                    a = jnp.where(level_ref[...] == lev, _mm_nt(z, z), a)
                else:
                    zq = upper_rows(z, m).astype(BF16)
                    zk = z.astype(BF16)
                    if m > D_HEAD:
                        p = [_mm_nt(zq[i * (m // 2):(i + 1) * (m // 2)], zk[i * m:i * m + m // 2])
                             for i in range(C // m)]
                        width = m // 2
                    else:
                        p = [_mm_nt(zq[i * (D_HEAD // 2):(i + 1) * (D_HEAD // 2)], zk[i * D_HEAD:(i + 1) * D_HEAD])
                             for i in range(C // D_HEAD)]
                        width = D_HEAD
                    full = jnp.concatenate([jnp.concatenate([pi] * (C // width), axis=1) for pi in p], axis=0)
                    lvl_u = upper_rows(level_ref[...], m)
                    a = put_upper_rows(a, jnp.where(lvl_u == lev, full, upper_rows(a, m)), m)
            o = _mm(a.astype(BF16), v)
            eb = jnp.exp(b)
            qe = q * eb
            kh = k * jnp.exp(block_rows(b, seq_rows, seq_rows - 1) - b)
            if sample:
                o_scr[rs, col(0, h)] = o
                qe_scr[rs, hs] = qe
                kh_scr[rs, hs] = kh
                e_all = block_rows(eb, seq_rows, seq_rows - 1)
                e_hi = e_all.astype(BF16).astype(F32)
                e_mid = (e_all - e_hi).astype(BF16).astype(F32)
                e_lo = e_all - e_hi - e_mid
                pos = row_id & (seq_rows - 1)
                ex_scr[rs, hs] = jnp.where(pos == 0, e_hi, jnp.where(pos == 1, e_mid, jnp.where(pos == 2, e_lo, 0.0)))
            else:
                st = sa_scr[h]
                o = o + _mm_nt(qe.astype(BF16), st.astype(BF16))
                sa_scr[h] = st * eb[C - 1:C, :] + _mm_tn(v, kh.astype(BF16))
                gate = proj_scr[rs, col(3, h)]
                o_scr[rs, col(0, h)] = _rmsnorm(o, na_ref[...]) * (gate * _sigmoid(gate))

        cos, sin = cos_ref[rs, :], sin_ref[rs, :]
        for h in range(N_HEADS):
            hs = head(h)
            q = proj_scr[rs, col(4, h)]
            k = proj_scr[rs, col(5, h)]
            v = proj_scr[rs, col(6, h)].astype(BF16)
            qr = q * cos + pltpu.roll(q, D_HEAD // 2, 1) * sin
            kr = (k * cos + pltpu.roll(k, D_HEAD // 2, 1) * sin) * (D_HEAD ** -0.5)
            a = _mm_nt(qr.astype(BF16), kr.astype(BF16)) * dec_ref[h]
            o = _mm(a.astype(BF16), v)
            qi = qr * inner_ref[h]
            ks = kr * sdec_ref[h]
            if sample:
                o_scr[rs, col(1, h)] = o
                qi_scr[rs, hs] = qi
                ks_scr[rs, hs] = ks
            else:
                st = sb_scr[h]
                o = o + _mm(qi.astype(BF16), st.astype(BF16))
                sb_scr[h] = cdec[h] * st + _mm_tn(ks.astype(BF16), v)
                gate = proj_scr[rs, col(7, h)]
                o_scr[rs, col(1, h)] = _groupnorm(o, nb_ref[...]) * (gate * _sigmoid(gate))

        if sample:
            for slot, group in enumerate(KEPT_GROUPS):
                keep_ref[:, slot * GROUP_W:(slot + 1) * GROUP_W] = proj_scr[rs, group * GROUP_W:(group + 1) * GROUP_W]

    def kept(group, h):
        slot = KEPT_GROUPS.index(group)
        return slice(slot * GROUP_W + h * D_HEAD, slot * GROUP_W + (h + 1) * D_HEAD)

    def apply_states():
        zeros8 = jnp.zeros((SUBLANES, D_HEAD), F32)
        sel8 = jnp.where(lax.broadcasted_iota(jnp.int32, (SUBLANES, D_HEAD), 0) < 3, 1.0, 0.0)

        def pair_readout(lhs_ref, st_ref, j, rows, group, h):
            lhs = jnp.concatenate([lhs_ref[rows, head(h)], lhs_ref[rows, head(h + 1)]], axis=0).astype(BF16)
            w = jnp.concatenate([st_ref[j, h], st_ref[j, h + 1]], axis=1).astype(BF16)
            oo = _mm(lhs, w)
            os_out_ref[rows, col(group, h)] = os_in_ref[rows, col(group, h)] + oo[:SUBLANES, :D_HEAD]
            os_out_ref[rows, col(group, h + 1)] = os_in_ref[rows, col(group, h + 1)] + oo[SUBLANES:, D_HEAD:]

        for j in range(smp_seq_per_step):
            rows = slice(j * SUBLANES, (j + 1) * SUBLANES)
            for h in range(0, N_HEADS, 2):
                pair_readout(qe_ref, sa_in_ref, j, rows, 0, h)
                pair_readout(qi_ref, sb_in_ref, j, rows, 1, h)
            for h in range(N_HEADS):
                hs = head(h)
                lhs = jnp.concatenate([kh_ref[rows, hs], ex_ref[rows, hs]], axis=0).astype(BF16)
                v = keep_in_ref[rows, kept(2, h)]
                rhs = jnp.concatenate([jnp.concatenate([v, zeros8], axis=1),
                                       jnp.concatenate([zeros8, sel8], axis=1)], axis=0).astype(BF16)
                upd = _mm_tn(lhs, rhs)
                sas_out_ref[j, h] = sa_in_ref[j, h] * upd[:, D_HEAD:] + upd[:, :D_HEAD]
                v = keep_in_ref[rows, kept(6, h)].astype(BF16)
                sbs_out_ref[j, h] = smp_cdec[h] * sb_in_ref[j, h] + _mm_tn(ks_ref[rows, hs].astype(BF16), v)

    chunk_rows = [slice(c * C, (c + 1) * C) for c in range(chunks)]
    for rs in chunk_rows:
        project(rs)
    for rs in chunk_rows:
        score(rs)
    if not sample:
        x1_ref[...] = x_ref[...] + _mm(o_scr[...].astype(BF16), wout_ref[...])
        for wide, narrow in zip(wide_refs, narrow_refs):
            narrow[...] = wide[...].astype(BF16)
        apply_states()

        @pl.when(step == pl.num_programs(1) - 1)
        def _():
            for h in range(N_HEADS):
                sa_out_ref[0, h] = sa_scr[h].T
                sb_out_ref[0, h] = sb_scr[h]


def _full(shape):
    return pl.BlockSpec(shape, lambda *_: (0,) * len(shape))


def _row_block_spec(shape, n_steps, inner_steps):
    _, n_rows, width = shape
    share = next(k for k in (1, 2, 4, 8) if n_steps % k == 0 and n_rows % ((n_steps // k) * BF16_ROWS) == 0)
    return pl.BlockSpec((1, n_rows // (n_steps // share), width),
                        lambda b, i: (0, (b * inner_steps + i) // share, 0))


def _mix_scratch(tile):
    return [pltpu.VMEM((tile, D_MODEL), BF16), pltpu.VMEM((tile, IN_WIDTH), F32)]


def _smp_score_call(x2, weights, *, seq_len, tile):
    n_rows = x2.shape[0]
    assert n_rows % tile == 0 and tile % seq_len == 0
    nlev, cdec, consts = _chunk_consts(tile, seq_len)
    cos, sin = _rope_tables(np.tile(PAST_LEN + np.arange(seq_len), tile // seq_len))
    rows = lambda width: pl.BlockSpec((tile, width), lambda n: (n, 0))
    widths = [2 * GROUP_W] + [GROUP_W] * 5 + [len(KEPT_GROUPS) * GROUP_W]
    args = [x2, cos, sin, *weights, *consts]
    w_in, w_out = weights[1], weights[5]
    narrow_shapes = [w.shape[1:] for w in (w_in, w_out)]
    *outs, w_in_bf, w_out_bf = pl.pallas_call(
        functools.partial(_mix_kernel, sample=True, tile=tile, chunks=1, seq_rows=seq_len, nlev=nlev, cdec=cdec,
                          smp_cdec=None, smp_seq_per_step=0),
        grid=(n_rows // tile,),
        in_specs=[rows(D_MODEL)] + [_full(a.shape) for a in args[1:]],
        out_specs=[rows(w) for w in widths] + [_full(shape) for shape in narrow_shapes],
        out_shape=[jax.ShapeDtypeStruct((n_rows, w), F32) for w in widths]
        + [jax.ShapeDtypeStruct(shape, BF16) for shape in narrow_shapes],
        scratch_shapes=_mix_scratch(tile) + [pltpu.VMEM((tile, GROUP_W), F32)] * 2,
        compiler_params=pltpu.CompilerParams(dimension_semantics=("arbitrary",), vmem_limit_bytes=VMEM_LIMIT),
        name="score_sample",
    )(*args)
    return cdec, outs, w_in_bf, w_out_bf


def _mix_call(x2, weights, to_narrow, smp, smp_states, smp_cdec, *, n_seq, seq_len, tile, chunks, smp_len):
    n_rows = n_seq * seq_len
    assert seq_len % tile == 0 and tile % chunks == 0
    chunk = tile // chunks
    steps = seq_len // tile
    n_steps = n_seq * steps
    nlev, cdec, consts = _chunk_consts(chunk, min(seq_len, chunk))
    cos, sin = _rope_tables(np.arange(seq_len))
    n_smp = smp_states[0].shape[0]
    assert n_smp % n_steps == 0
    seq_per_step = n_smp // n_steps
    smp_rows = seq_per_step * smp_len
    assert smp_rows % SUBLANES == 0

    row_map = lambda b, i: (b * steps + i, 0)
    pos_map = lambda b, i: (i, 0)
    state_spec = pl.BlockSpec((1, N_HEADS, D_HEAD, D_HEAD), lambda b, i: (b, 0, 0, 0))
    smp_state_spec = pl.BlockSpec((seq_per_step, N_HEADS, D_HEAD, D_HEAD), lambda b, i: (b * steps + i, 0, 0, 0))
    smp_spec = lambda a: pl.BlockSpec((smp_rows, a.shape[1]), row_map)
    narrow_specs = [_row_block_spec(w.shape, n_steps, steps) for w in to_narrow]
    smp_scores, *smp_factors, smp_keep = smp
    side_in = [*smp_factors, smp_keep, smp_scores]
    in_specs = ([pl.BlockSpec((tile, D_MODEL), row_map), pl.BlockSpec((tile, D_HEAD), pos_map),
                 pl.BlockSpec((tile, D_HEAD), pos_map)] + [_full(w.shape) for w in weights]
                + [_full(c.shape) for c in consts] + narrow_specs + [smp_spec(a) for a in side_in]
                + [smp_state_spec, smp_state_spec])
    args = [x2, cos, sin, *weights, *consts, *to_narrow, *side_in, *smp_states]
    state_shape = lambda n: jax.ShapeDtypeStruct((n, N_HEADS, D_HEAD, D_HEAD), F32)
    return pl.pallas_call(
        functools.partial(_mix_kernel, sample=False, tile=tile, chunks=chunks, seq_rows=min(seq_len, chunk),
                          nlev=nlev, cdec=cdec, smp_cdec=smp_cdec, smp_seq_per_step=seq_per_step),
        grid=(n_seq, steps),
        in_specs=in_specs,
        out_specs=[pl.BlockSpec((tile, D_MODEL), row_map), state_spec, state_spec] + narrow_specs
        + [smp_spec(smp_scores), smp_state_spec, smp_state_spec],
        out_shape=[jax.ShapeDtypeStruct((n_rows, D_MODEL), F32), state_shape(n_seq), state_shape(n_seq)]
        + [jax.ShapeDtypeStruct(w.shape, BF16) for w in to_narrow]
        + [jax.ShapeDtypeStruct(smp_scores.shape, F32), state_shape(n_smp), state_shape(n_smp)],
        scratch_shapes=_mix_scratch(tile) + [pltpu.VMEM((tile, 2 * GROUP_W), F32)]
        + [pltpu.VMEM((tile, GROUP_W), F32)] * 2 + [pltpu.VMEM((N_HEADS, D_HEAD, D_HEAD), F32)] * 2,
        compiler_params=pltpu.CompilerParams(dimension_semantics=("arbitrary", "arbitrary"),
                                             vmem_limit_bytes=VMEM_LIMIT),
        name="mix_prompt",
    )(*args)


def _ffn_kernel(*refs, sample, groups, rows):
    x_ref, w2_ref, wup_ref, cw_ref, cb_ref, wdown_ref, wf_ref = refs[:7]
    refs = refs[7:]
    if sample:
        (hist_ref, os_ref, keep_ref, na_ref, nb_ref, wout_ref, y_ref, hist_out_ref, h_scr, act_scr, mixed_scr,
         xin) = refs
        for h in range(N_HEADS):
            for group, norm, w_ref in ((0, _rmsnorm, na_ref), (1, _groupnorm, nb_ref)):
                cols = slice(group * GROUP_W + h * D_HEAD, group * GROUP_W + (h + 1) * D_HEAD)
                slot = KEPT_GROUPS.index(4 * group + 3)
                gate = keep_ref[:, slot * GROUP_W + h * D_HEAD:slot * GROUP_W + (h + 1) * D_HEAD]
                mixed_scr[:, cols] = (norm(os_ref[:, cols], w_ref[...]) * (gate * _sigmoid(gate))).astype(BF16)
        xin[...] = x_ref[...] + _mm(mixed_scr[...], wout_ref[...])
    else:
        y_ref, hist_out_ref, h_scr, act_scr, tail_scr = refs
        xin = x_ref
        step = pl.program_id(1)

        @pl.when(step == 0)
        def _():
            tail_scr[...] = jnp.zeros_like(tail_scr)

    G, L, P = groups, rows, SUBLANES
    parts = 1 if sample else ROW_PARTS
    LP = L // parts
    row_id = lax.broadcasted_iota(jnp.int32, (G, P, FF_COLS), 1)

    def shifted(up, prev2, prev1):
        r1, r2 = pltpu.roll(up, 1, 1), pltpu.roll(up, 2, 1)
        top1 = jnp.where(row_id == 0, prev1, r1[:, :P])
        top2 = jnp.where(row_id == 0, prev2, jnp.where(row_id == 1, prev1, r2[:, :P]))
        if LP == P:
            return top1, top2
        return jnp.concatenate([top1, r1[:, P:]], axis=1), jnp.concatenate([top2, r2[:, P:]], axis=1)

    tails = {}
    for part in range(parts):
        rows_p = slice(part * G * LP, (part + 1) * G * LP)
        h_scr[rows_p, :] = _rmsnorm(xin[rows_p, :], w2_ref[...]).astype(BF16)
        for n in range(0, D_FF, FF_COLS):
            conv = []
            for cols in (slice(n, n + FF_COLS), slice(D_FF + n, D_FF + n + FF_COLS)):
                up = _mm(h_scr[rows_p, :], wup_ref[0, :, cols]).reshape(G, LP, FF_COLS)
                if sample:
                    prev2, prev1 = hist_ref[:, 0:1, cols], hist_ref[:, 1:2, cols]
                    hist_out_ref[:, :, cols] = up[:, LP - 2:, :]
                else:
                    if part == 0:
                        prev2, prev1 = tail_scr[:, P - 2:P - 1, cols], tail_scr[:, P - 1:P, cols]
                    else:
                        prev = tails[cols.start]
                        prev2, prev1 = prev[:, P - 2:P - 1, :], prev[:, P - 1:P, :]
                    tails[cols.start] = up[:, LP - P:, :]
                    if part == parts - 1:
                        tail_scr[:, :, cols] = up[:, LP - P:, :]
                sh1, sh2 = shifted(up, prev2, prev1)
                conv.append(cb_ref[:, cols] + cw_ref[0:1, cols] * sh2 + cw_ref[1:2, cols] * sh1 + cw_ref[2:3, cols] * up)
            u, g = (c.astype(BF16) for c in conv)
            one = jnp.ones((), BF16)
            act_scr[rows_p, n:n + FF_COLS] = ((g * (one / (one + jnp.exp(-g)))) * u).reshape(G * LP, FF_COLS)

        x2 = xin[rows_p, :] + _mm(act_scr[rows_p, :], wdown_ref[0])
        y_ref[rows_p, :] = _rmsnorm(x2, wf_ref[...])

    if not sample:
        @pl.when(step == pl.num_programs(1) - 1)
        def _():
            hist_out_ref[...] = tail_scr[:, P - 2:, :]


def _ffn_call(x, w2, w_up, cw, cb, w_down, wf, smp, *, n_seq, seq_len, tile):
    sample = smp is not None
    n_rows = n_seq * seq_len
    hist_shape = jax.ShapeDtypeStruct((n_seq, CONV_W - 1, FF2), F32)
    if sample:
        hist, scores, keep, na, nb, w_out = smp
        groups, rows = tile // seq_len, seq_len
        grid = (n_rows // tile,)
        row_map = lambda n: (n, 0)
        hist_spec = pl.BlockSpec((groups, CONV_W - 1, FF2), lambda n: (n, 0, 0))
        extra_in = [hist_spec, pl.BlockSpec((tile, scores.shape[1]), row_map),
                    pl.BlockSpec((tile, keep.shape[1]), row_map), _full(na.shape), _full(nb.shape), _full(w_out.shape)]
        extra_args = [hist, scores, keep, na, nb, w_out]
        scratch = [pltpu.VMEM((tile, 2 * GROUP_W), BF16), pltpu.VMEM((tile, D_MODEL), F32)]
    else:
        assert seq_len % tile == 0
        groups, rows = 1, tile
        steps = seq_len // tile
        grid = (n_seq, steps)
        row_map = lambda b, i: (b * steps + i, 0)
        hist_spec = pl.BlockSpec((1, CONV_W - 1, FF2), lambda b, i: (b, 0, 0))
        extra_in, extra_args = [], []
        scratch = [pltpu.VMEM((1, SUBLANES, FF2), F32)]
    in_specs = [pl.BlockSpec((tile, D_MODEL), row_map), _full(w2.shape), _full(w_up.shape), _full(cw.shape),
                _full(cb.shape), _full(w_down.shape), _full(wf.shape)] + extra_in
    args = [x, w2, w_up, cw, cb, w_down, wf] + extra_args
    return pl.pallas_call(
        functools.partial(_ffn_kernel, sample=sample, groups=groups, rows=rows),
        grid=grid,
        in_specs=in_specs,
        out_specs=[pl.BlockSpec((tile, D_MODEL), row_map), hist_spec],
        out_shape=[jax.ShapeDtypeStruct((n_rows, D_MODEL), F32), hist_shape],
        scratch_shapes=[pltpu.VMEM((tile, D_MODEL), BF16), pltpu.VMEM((tile, D_FF), BF16)] + scratch,
        compiler_params=pltpu.CompilerParams(dimension_semantics=("arbitrary",) * len(grid),
                                             vmem_limit_bytes=VMEM_LIMIT),
        name="ffn_sample" if sample else "ffn_prompt",
    )(*args)


def kernel(x_prompt, x_sample, state_hgrn, state_ret, state_conv, w_norm1, w_in, hgrn_lb, hgrn_norm_w, ret_norm_w,
           w_out, w_norm2, w_ffn_in, conv_w, conv_b, w_ffn_out, w_norm_f):
    assert w_in.shape == (1, D_MODEL, IN_WIDTH) and hgrn_lb.shape == (2, GROUP_W)
    n_seq, seq_len, _ = x_prompt.shape
    n_smp, smp_len, _ = x_sample.shape
    xs = x_sample.reshape(n_smp * smp_len, D_MODEL)

    smp_cdec, smp, w_in_bf, w_out_bf = _smp_score_call(
        xs, (w_norm1, w_in, hgrn_lb, hgrn_norm_w, ret_norm_w, w_out), seq_len=smp_len, tile=256)
    mix_w = (w_norm1, w_in_bf, hgrn_lb, hgrn_norm_w, ret_norm_w, w_out_bf)
    x1, ha_p, rb_p, w_up, w_down, smp_scores, ha_s, rb_s = _mix_call(
        x_prompt.reshape(n_seq * seq_len, D_MODEL), mix_w, (w_ffn_in, w_ffn_out), smp,
        (state_hgrn[0], state_ret[0]), smp_cdec, n_seq=n_seq, seq_len=seq_len, tile=512, chunks=2, smp_len=smp_len)
    ffn_w = (w_norm2, w_up, conv_w[0], conv_b, w_down, w_norm_f.reshape(1, D_MODEL))
    y_p, cv_p = _ffn_call(x1, *ffn_w, None, n_seq=n_seq, seq_len=seq_len, tile=1024)
    y_s, cv_s = _ffn_call(xs, *ffn_w, (state_conv[0], smp_scores, smp[-1], hgrn_norm_w, ret_norm_w, w_out_bf),
                          n_seq=n_smp, seq_len=smp_len, tile=256)
    return (y_p.reshape(x_prompt.shape), y_s.reshape(x_sample.shape), ha_p[None], rb_p[None], cv_p[None],
            ha_s[None], rb_s[None], cv_s[None])
```

```python
import functools

import numpy as np
import jax
import jax.numpy as jnp
from jax import lax
from jax.experimental import pallas as pl
from jax.experimental.pallas import tpu as pltpu

F32 = jnp.float32
BF16 = jnp.bfloat16

D_MODEL = 1024
N_HEADS = 4
D_HEAD = 128
GROUP_W = N_HEADS * D_HEAD
IN_WIDTH = 8 * GROUP_W
D_FF = 2816
FF2 = 2 * D_FF
CONV_W = 3
PAST_LEN = 16384
ROPE_BASE = 10000.0
EPS = 1e-6
LOG2E = 1.4426950408889634

SUBLANES = 8
BF16_ROWS = 16
PROJ_COLS = 512
FF_COLS = 256
ROW_PARTS = 2
KEPT_GROUPS = (2, 3, 6, 7)
VMEM_LIMIT = 56 * 1024 * 1024


def _mm(a, b):
    return jnp.dot(a, b, preferred_element_type=F32)


def _mm_nt(a, b):
    return lax.dot_general(a, b, (((1,), (1,)), ((), ())), preferred_element_type=F32)


def _mm_tn(a, b):
    return lax.dot_general(a, b, (((0,), (0,)), ((), ())), preferred_element_type=F32)


def _sigmoid(x):
    return 1.0 / (1.0 + jnp.exp(-x))


def _rmsnorm(x, w):
    return x * lax.rsqrt(jnp.mean(x * x, axis=-1, keepdims=True) + EPS) * w


def _groupnorm(x, w):
    xc = x - jnp.mean(x, axis=-1, keepdims=True)
    return xc * lax.rsqrt(jnp.mean(xc * xc, axis=-1, keepdims=True) + EPS) * w


def _chunk_consts(chunk, seq_len):
    nlev = int(np.log2(seq_len))
    assert 1 << nlev == seq_len and chunk % seq_len == 0
    r = np.arange(chunk)
    rr, cc = r[:, None], r[None, :]
    same_seq = (rr // seq_len) == (cc // seq_len)
    cum = (same_seq & (cc <= rr)).astype(np.float32)
    x = rr ^ cc
    bit_len = np.where(x > 0, np.floor(np.log2(np.maximum(x, 1))).astype(np.int64) + 1, 0)
    level = np.where(same_seq & (cc <= rr), bit_len, -1).astype(np.int32)

    pos = r % seq_len
    log_gamma = np.log1p(-np.exp2(-5.0 - np.arange(N_HEADS, dtype=np.float64)))[:, None, None]
    rel = (pos[:, None] - pos[None, :]).astype(np.float64)[None]
    causal = (same_seq & (cc <= rr))[None]
    dec = np.where(causal, np.exp(np.where(causal, rel, 0.0) * log_gamma), 0.0)
    ones = np.ones((1, 1, D_HEAD))
    inner = np.exp((pos + 1.0)[None, :, None] * log_gamma) * ones
    sdec = np.exp((seq_len - 1.0 - pos)[None, :, None] * log_gamma) * ones
    cdec = tuple(float(v) for v in np.exp(seq_len * log_gamma[:, 0, 0]))
    consts = (jnp.asarray(cum, BF16), jnp.asarray(level, BF16), jnp.asarray(dec, F32), jnp.asarray(inner, F32),
              jnp.asarray(sdec, F32))
    return nlev, cdec, consts


def _rope_tables(pos):
    half = D_HEAD // 2
    inv = 1.0 / (ROPE_BASE ** (np.arange(half, dtype=np.float64) / half))
    ang = np.asarray(pos, np.float64)[:, None] * inv[None, :]
    cos, sin = np.cos(ang), np.sin(ang)
    return (jnp.asarray(np.concatenate([cos, cos], axis=-1), F32),
            jnp.asarray(np.concatenate([-sin, sin], axis=-1), F32))


def _mix_kernel(*refs, sample, tile, chunks, seq_rows, nlev, cdec, smp_cdec, smp_seq_per_step):
    (x_ref, cos_ref, sin_ref, w1_ref, win_ref, lbp_ref, na_ref, nb_ref, wout_ref,
     cum_ref, level_ref, dec_ref, inner_ref, sdec_ref) = refs[:14]
    refs = refs[14:]
    if sample:
        o_scr, qe_scr, kh_scr, ex_scr, qi_scr, ks_scr, keep_ref, win_bf_ref, wout_bf_ref = refs[:9]
        h_scr, proj_scr, d_scr, k_scr = refs[9:]

        @pl.when(pl.program_id(0) == 0)
        def _():
            for n in range(0, IN_WIDTH, PROJ_COLS):
                win_bf_ref[:, n:n + PROJ_COLS] = win_ref[0, :, n:n + PROJ_COLS].astype(BF16)
            wout_bf_ref[...] = wout_ref[0].astype(BF16)

        win_ref = win_bf_ref
    else:
        wide_refs = refs[:2]
        qe_ref, kh_ref, ex_ref, qi_ref, ks_ref, keep_in_ref, os_in_ref, sa_in_ref, sb_in_ref = refs[2:11]
        x1_ref, sa_out_ref, sb_out_ref = refs[11:14]
        narrow_refs = refs[14:16]
        os_out_ref, sas_out_ref, sbs_out_ref = refs[16:19]
        h_scr, proj_scr, o_scr, d_scr, k_scr, sa_scr, sb_scr = refs[19:]
        step = pl.program_id(1)

        @pl.when(step == 0)
        def _():
            sa_scr[...] = jnp.zeros_like(sa_scr)
            sb_scr[...] = jnp.zeros_like(sb_scr)

    C = tile // chunks

    def col(group, h):
        return slice(group * GROUP_W + h * D_HEAD, group * GROUP_W + (h + 1) * D_HEAD)

    def head(h):
        return slice(h * D_HEAD, (h + 1) * D_HEAD)

    def block_rows(x, m, row):
        x3 = x.reshape(C // m, m, D_HEAD)
        return jnp.broadcast_to(x3[:, row:row + 1, :], x3.shape).reshape(C, D_HEAD)

    def upper_rows(x, m):
        return x.reshape(C // m, 2, m // 2, x.shape[-1])[:, 1].reshape(C // 2, x.shape[-1])

    def put_upper_rows(x, xu, m):
        x4 = x.reshape(C // m, 2, m // 2, x.shape[-1])
        xu4 = xu.reshape(C // m, 1, m // 2, x.shape[-1])
        return jnp.concatenate([x4[:, 0:1], xu4], axis=1).reshape(C, x.shape[-1])

    def project(rs):
        h_scr[rs, :] = _rmsnorm(x_ref[rs, :], w1_ref[...]).astype(BF16)
        for n in range(0, IN_WIDTH, PROJ_COLS):
            proj_scr[rs, n:n + PROJ_COLS] = _mm(h_scr[rs, :], win_ref[:, n:n + PROJ_COLS])

    def score(rs):
        lb0, lb1 = lbp_ref[0:1, :], lbp_ref[1:2, :]
        lb_max = jnp.maximum(lb0, lb1)
        e0, e1 = jnp.exp(lb0 - lb_max), jnp.exp(lb1 - lb_max)
        lb = e0 / (e0 + e1)

        row_id = lax.broadcasted_iota(jnp.int32, (C, D_HEAD), 0)

        f = lb + (1.0 - lb) * _sigmoid(proj_scr[rs, GROUP_W:2 * GROUP_W])
        k_scr[rs, :] = 1.0 - f
        g = jnp.log(f)
        g_hi = g.astype(BF16)
        g_lo = (g - g_hi.astype(F32)).astype(BF16)
        d_scr[rs, :] = _mm(cum_ref[...], g_hi) + _mm(cum_ref[...], g_lo)

        signs = [jnp.where((row_id & (1 << (lev - 1))) != 0, LOG2E, -LOG2E) for lev in range(3, nlev + 1)]

        for h in range(N_HEADS):
            hs = head(h)
            q = proj_scr[rs, col(0, h)]
            k = k_scr[rs, hs]
            v = proj_scr[rs, col(2, h)].astype(BF16)
            b = d_scr[rs, hs]
            a = jnp.where(level_ref[...] == 0, _mm_nt(q.astype(BF16), k.astype(BF16)).astype(BF16), 0.0)
            for lev in range(1, nlev + 1):
                m = 1 << lev
                upper = (row_id & (m // 2)) != 0
                if lev == 1:
                    z = jnp.where(upper, q * (1.0 - k), k)
                elif lev == 2:
                    fh = 1.0 - k
                    pos4 = row_id & 3
                    decay = jnp.where(pos4 == 0, pltpu.roll(fh, C - 1, 0),
                                      jnp.where(pos4 == 1, 1.0, jnp.where(pos4 == 2, fh, fh * pltpu.roll(fh, 1, 0))))
                    z = jnp.where(upper, q, k) * decay
                else:
                    z = jnp.where(upper, q, k) * jnp.exp2((b - block_rows(b, m, m // 2 - 1)) * signs[lev - 3])
                if m < 2 * BF16_ROWS:
                    z = z.astype(BF16)
                    a = jnp.where(level_ref[...] == lev, _mm_nt(z, z).astype(BF16), a)
                else:
                    zq = upper_rows(z, m).astype(BF16)
                    zk = z.astype(BF16)
                    if m > D_HEAD:
                        p = [_mm_nt(zq[i * (m // 2):(i + 1) * (m // 2)], zk[i * m:i * m + m // 2])
                             for i in range(C // m)]
                        width = m // 2
                    else:
                        p = [_mm_nt(zq[i * (D_HEAD // 2):(i + 1) * (D_HEAD // 2)], zk[i * D_HEAD:(i + 1) * D_HEAD])
                             for i in range(C // D_HEAD)]
                        width = D_HEAD
                    full = jnp.concatenate([jnp.concatenate([pi] * (C // width), axis=1) for pi in p], axis=0)
                    lvl_u = upper_rows(level_ref[...], m)
                    a = put_upper_rows(a, jnp.where(lvl_u == lev, full.astype(BF16), upper_rows(a, m)), m)
            o = _mm(a, v)
            eb = jnp.exp(b)
            qe = q * eb
            kh = k * jnp.exp(block_rows(b, seq_rows, seq_rows - 1) - b)
            if sample:
                o_scr[rs, col(0, h)] = o
                qe_scr[rs, hs] = qe
                kh_scr[rs, hs] = kh
                e_all = block_rows(eb, seq_rows, seq_rows - 1)
                e_hi = e_all.astype(BF16).astype(F32)
                e_mid = (e_all - e_hi).astype(BF16).astype(F32)
                e_lo = e_all - e_hi - e_mid
                pos = row_id & (seq_rows - 1)
                ex_scr[rs, hs] = jnp.where(pos == 0, e_hi, jnp.where(pos == 1, e_mid, jnp.where(pos == 2, e_lo, 0.0)))
            else:
                st = sa_scr[h]
                o = o + _mm_nt(qe.astype(BF16), st.astype(BF16))
                sa_scr[h] = st * eb[C - 1:C, :] + _mm_tn(v, kh.astype(BF16))
                gate = proj_scr[rs, col(3, h)]
                o_scr[rs, col(0, h)] = _rmsnorm(o, na_ref[...]) * (gate * _sigmoid(gate))

        cos, sin = cos_ref[rs, :], sin_ref[rs, :]
        for h in range(N_HEADS):
            hs = head(h)
            q = proj_scr[rs, col(4, h)]
            k = proj_scr[rs, col(5, h)]
            v = proj_scr[rs, col(6, h)].astype(BF16)
            qr = q * cos + pltpu.roll(q, D_HEAD // 2, 1) * sin
            kr = (k * cos + pltpu.roll(k, D_HEAD // 2, 1) * sin) * (D_HEAD ** -0.5)
            a = _mm_nt(qr.astype(BF16), kr.astype(BF16)) * dec_ref[h]
            o = _mm(a.astype(BF16), v)
            qi = qr * inner_ref[h]
            ks = kr * sdec_ref[h]
            if sample:
                o_scr[rs, col(1, h)] = o
                qi_scr[rs, hs] = qi
                ks_scr[rs, hs] = ks
            else:
                st = sb_scr[h]
                o = o + _mm(qi.astype(BF16), st.astype(BF16))
                sb_scr[h] = cdec[h] * st + _mm_tn(ks.astype(BF16), v)
                gate = proj_scr[rs, col(7, h)]
                o_scr[rs, col(1, h)] = _groupnorm(o, nb_ref[...]) * (gate * _sigmoid(gate))

        if sample:
            for slot, group in enumerate(KEPT_GROUPS):
                keep_ref[:, slot * GROUP_W:(slot + 1) * GROUP_W] = proj_scr[rs, group * GROUP_W:(group + 1) * GROUP_W]

    def kept(group, h):
        slot = KEPT_GROUPS.index(group)
        return slice(slot * GROUP_W + h * D_HEAD, slot * GROUP_W + (h + 1) * D_HEAD)

    def apply_states():
        zeros8 = jnp.zeros((SUBLANES, D_HEAD), F32)
        sel8 = jnp.where(lax.broadcasted_iota(jnp.int32, (SUBLANES, D_HEAD), 0) < 3, 1.0, 0.0)

        def pair_readout(lhs_ref, st_ref, j, rows, group, h):
            lhs = jnp.concatenate([lhs_ref[rows, head(h)], lhs_ref[rows, head(h + 1)]], axis=0).astype(BF16)
            w = jnp.concatenate([st_ref[j, h], st_ref[j, h + 1]], axis=1).astype(BF16)
            oo = _mm(lhs, w)
            os_out_ref[rows, col(group, h)] = os_in_ref[rows, col(group, h)] + oo[:SUBLANES, :D_HEAD]
            os_out_ref[rows, col(group, h + 1)] = os_in_ref[rows, col(group, h + 1)] + oo[SUBLANES:, D_HEAD:]

        for j in range(smp_seq_per_step):
            rows = slice(j * SUBLANES, (j + 1) * SUBLANES)
            for h in range(0, N_HEADS, 2):
                pair_readout(qe_ref, sa_in_ref, j, rows, 0, h)
                pair_readout(qi_ref, sb_in_ref, j, rows, 1, h)
            for h in range(N_HEADS):
                hs = head(h)
                lhs = jnp.concatenate([kh_ref[rows, hs], ex_ref[rows, hs]], axis=0).astype(BF16)
                v = keep_in_ref[rows, kept(2, h)]
                rhs = jnp.concatenate([jnp.concatenate([v, zeros8], axis=1),
                                       jnp.concatenate([zeros8, sel8], axis=1)], axis=0).astype(BF16)
                upd = _mm_tn(lhs, rhs)
                sas_out_ref[j, h] = sa_in_ref[j, h] * upd[:, D_HEAD:] + upd[:, :D_HEAD]
                v = keep_in_ref[rows, kept(6, h)].astype(BF16)
                sbs_out_ref[j, h] = smp_cdec[h] * sb_in_ref[j, h] + _mm_tn(ks_ref[rows, hs].astype(BF16), v)

    chunk_rows = [slice(c * C, (c + 1) * C) for c in range(chunks)]
    for rs in chunk_rows:
        project(rs)
    for rs in chunk_rows:
        score(rs)
    if not sample:
        x1_ref[...] = x_ref[...] + _mm(o_scr[...].astype(BF16), wout_ref[...])
        for wide, narrow in zip(wide_refs, narrow_refs):
            narrow[...] = wide[...].astype(BF16)
        apply_states()

        @pl.when(step == pl.num_programs(1) - 1)
        def _():
            for h in range(N_HEADS):
                sa_out_ref[0, h] = sa_scr[h].T
                sb_out_ref[0, h] = sb_scr[h]


def _full(shape):
    return pl.BlockSpec(shape, lambda *_: (0,) * len(shape))


def _row_block_spec(shape, n_steps, inner_steps):
    _, n_rows, width = shape
    share = next(k for k in (1, 2, 4, 8) if n_steps % k == 0 and n_rows % ((n_steps // k) * BF16_ROWS) == 0)
    return pl.BlockSpec((1, n_rows // (n_steps // share), width),
                        lambda b, i: (0, (b * inner_steps + i) // share, 0))


def _mix_scratch(tile):
    return [pltpu.VMEM((tile, D_MODEL), BF16), pltpu.VMEM((tile, IN_WIDTH), F32)]


def _smp_score_call(x2, weights, *, seq_len, tile):
    n_rows = x2.shape[0]
    assert n_rows % tile == 0 and tile % seq_len == 0
    nlev, cdec, consts = _chunk_consts(tile, seq_len)
    cos, sin = _rope_tables(np.tile(PAST_LEN + np.arange(seq_len), tile // seq_len))
    rows = lambda width: pl.BlockSpec((tile, width), lambda n: (n, 0))
    widths = [2 * GROUP_W] + [GROUP_W] * 5 + [len(KEPT_GROUPS) * GROUP_W]
    args = [x2, cos, sin, *weights, *consts]
    w_in, w_out = weights[1], weights[5]
    narrow_shapes = [w.shape[1:] for w in (w_in, w_out)]
    *outs, w_in_bf, w_out_bf = pl.pallas_call(
        functools.partial(_mix_kernel, sample=True, tile=tile, chunks=1, seq_rows=seq_len, nlev=nlev, cdec=cdec,
                          smp_cdec=None, smp_seq_per_step=0),
        grid=(n_rows // tile,),
        in_specs=[rows(D_MODEL)] + [_full(a.shape) for a in args[1:]],
        out_specs=[rows(w) for w in widths] + [_full(shape) for shape in narrow_shapes],
        out_shape=[jax.ShapeDtypeStruct((n_rows, w), F32) for w in widths]
        + [jax.ShapeDtypeStruct(shape, BF16) for shape in narrow_shapes],
        scratch_shapes=_mix_scratch(tile) + [pltpu.VMEM((tile, GROUP_W), F32)] * 2,
        compiler_params=pltpu.CompilerParams(dimension_semantics=("arbitrary",), vmem_limit_bytes=VMEM_LIMIT),
        name="score_sample",
    )(*args)
    return cdec, outs, w_in_bf, w_out_bf


def _mix_call(x2, weights, to_narrow, smp, smp_states, smp_cdec, *, n_seq, seq_len, tile, chunks, smp_len):
    n_rows = n_seq * seq_len
    assert seq_len % tile == 0 and tile % chunks == 0
    chunk = tile // chunks
    steps = seq_len // tile
    n_steps = n_seq * steps
    nlev, cdec, consts = _chunk_consts(chunk, min(seq_len, chunk))
    cos, sin = _rope_tables(np.arange(seq_len))
    n_smp = smp_states[0].shape[0]
    assert n_smp % n_steps == 0
    seq_per_step = n_smp // n_steps
    smp_rows = seq_per_step * smp_len
    assert smp_rows % SUBLANES == 0

    row_map = lambda b, i: (b * steps + i, 0)
    pos_map = lambda b, i: (i, 0)
    state_spec = pl.BlockSpec((1, N_HEADS, D_HEAD, D_HEAD), lambda b, i: (b, 0, 0, 0))
    smp_state_spec = pl.BlockSpec((seq_per_step, N_HEADS, D_HEAD, D_HEAD), lambda b, i: (b * steps + i, 0, 0, 0))
    smp_spec = lambda a: pl.BlockSpec((smp_rows, a.shape[1]), row_map)
    narrow_specs = [_row_block_spec(w.shape, n_steps, steps) for w in to_narrow]
    smp_scores, *smp_factors, smp_keep = smp
    side_in = [*smp_factors, smp_keep, smp_scores]
    in_specs = ([pl.BlockSpec((tile, D_MODEL), row_map), pl.BlockSpec((tile, D_HEAD), pos_map),
                 pl.BlockSpec((tile, D_HEAD), pos_map)] + [_full(w.shape) for w in weights]
                + [_full(c.shape) for c in consts] + narrow_specs + [smp_spec(a) for a in side_in]
                + [smp_state_spec, smp_state_spec])
    args = [x2, cos, sin, *weights, *consts, *to_narrow, *side_in, *smp_states]
    state_shape = lambda n: jax.ShapeDtypeStruct((n, N_HEADS, D_HEAD, D_HEAD), F32)
    return pl.pallas_call(
        functools.partial(_mix_kernel, sample=False, tile=tile, chunks=chunks, seq_rows=min(seq_len, chunk),
                          nlev=nlev, cdec=cdec, smp_cdec=smp_cdec, smp_seq_per_step=seq_per_step),
        grid=(n_seq, steps),
        in_specs=in_specs,
        out_specs=[pl.BlockSpec((tile, D_MODEL), row_map), state_spec, state_spec] + narrow_specs
        + [smp_spec(smp_scores), smp_state_spec, smp_state_spec],
        out_shape=[jax.ShapeDtypeStruct((n_rows, D_MODEL), F32), state_shape(n_seq), state_shape(n_seq)]
        + [jax.ShapeDtypeStruct(w.shape, BF16) for w in to_narrow]
        + [jax.ShapeDtypeStruct(smp_scores.shape, F32), state_shape(n_smp), state_shape(n_smp)],
        scratch_shapes=_mix_scratch(tile) + [pltpu.VMEM((tile, 2 * GROUP_W), F32)]
        + [pltpu.VMEM((tile, GROUP_W), F32)] * 2 + [pltpu.VMEM((N_HEADS, D_HEAD, D_HEAD), F32)] * 2,
        compiler_params=pltpu.CompilerParams(dimension_semantics=("arbitrary", "arbitrary"),
                                             vmem_limit_bytes=VMEM_LIMIT),
        name="mix_prompt",
    )(*args)


def _ffn_kernel(*refs, sample, groups, rows):
    x_ref, w2_ref, wup_ref, cw_ref, cb_ref, wdown_ref, wf_ref = refs[:7]
    refs = refs[7:]
    if sample:
        (hist_ref, os_ref, keep_ref, na_ref, nb_ref, wout_ref, y_ref, hist_out_ref, h_scr, act_scr, mixed_scr,
         xin) = refs
        for h in range(N_HEADS):
            for group, norm, w_ref in ((0, _rmsnorm, na_ref), (1, _groupnorm, nb_ref)):
                cols = slice(group * GROUP_W + h * D_HEAD, group * GROUP_W + (h + 1) * D_HEAD)
                slot = KEPT_GROUPS.index(4 * group + 3)
                gate = keep_ref[:, slot * GROUP_W + h * D_HEAD:slot * GROUP_W + (h + 1) * D_HEAD]
                mixed_scr[:, cols] = (norm(os_ref[:, cols], w_ref[...]) * (gate * _sigmoid(gate))).astype(BF16)
        xin[...] = x_ref[...] + _mm(mixed_scr[...], wout_ref[...])
    else:
        y_ref, hist_out_ref, h_scr, act_scr, tail_scr = refs
        xin = x_ref
        step = pl.program_id(1)

        @pl.when(step == 0)
        def _():
            tail_scr[...] = jnp.zeros_like(tail_scr)

    G, L, P = groups, rows, SUBLANES
    parts = 1 if sample else ROW_PARTS
    LP = L // parts
    row_id = lax.broadcasted_iota(jnp.int32, (G, P, FF_COLS), 1)

    def shifted(up, prev2, prev1):
        r1, r2 = pltpu.roll(up, 1, 1), pltpu.roll(up, 2, 1)
        top1 = jnp.where(row_id == 0, prev1, r1[:, :P])
        top2 = jnp.where(row_id == 0, prev2, jnp.where(row_id == 1, prev1, r2[:, :P]))
        if LP == P:
            return top1, top2
        return jnp.concatenate([top1, r1[:, P:]], axis=1), jnp.concatenate([top2, r2[:, P:]], axis=1)

    tails = {}
    for part in range(parts):
        rows_p = slice(part * G * LP, (part + 1) * G * LP)
        h_scr[rows_p, :] = _rmsnorm(xin[rows_p, :], w2_ref[...]).astype(BF16)
        for n in range(0, D_FF, FF_COLS):
            conv = []
            for cols in (slice(n, n + FF_COLS), slice(D_FF + n, D_FF + n + FF_COLS)):
                up = _mm(h_scr[rows_p, :], wup_ref[0, :, cols]).reshape(G, LP, FF_COLS)
                if sample:
                    prev2, prev1 = hist_ref[:, 0:1, cols], hist_ref[:, 1:2, cols]
                    hist_out_ref[:, :, cols] = up[:, LP - 2:, :]
                else:
                    if part == 0:
                        prev2, prev1 = tail_scr[:, P - 2:P - 1, cols], tail_scr[:, P - 1:P, cols]
                    else:
                        prev = tails[cols.start]
                        prev2, prev1 = prev[:, P - 2:P - 1, :], prev[:, P - 1:P, :]
                    tails[cols.start] = up[:, LP - P:, :]
                    if part == parts - 1:
                        tail_scr[:, :, cols] = up[:, LP - P:, :]
                sh1, sh2 = shifted(up, prev2, prev1)
                conv.append(cb_ref[:, cols] + cw_ref[0:1, cols] * sh2 + cw_ref[1:2, cols] * sh1 + cw_ref[2:3, cols] * up)
            u, g = (c.reshape(G * LP, FF_COLS).astype(BF16) for c in conv)
            one = jnp.ones((), BF16)
            act_scr[rows_p, n:n + FF_COLS] = (g * (one / (one + jnp.exp(-g)))) * u

        x2 = xin[rows_p, :] + _mm(act_scr[rows_p, :], wdown_ref[0])
        y_ref[rows_p, :] = _rmsnorm(x2, wf_ref[...])

    if not sample:
        @pl.when(step == pl.num_programs(1) - 1)
        def _():
            hist_out_ref[...] = tail_scr[:, P - 2:, :]


def _ffn_call(x, w2, w_up, cw, cb, w_down, wf, smp, *, n_seq, seq_len, tile):
    sample = smp is not None
    n_rows = n_seq * seq_len
    hist_shape = jax.ShapeDtypeStruct((n_seq, CONV_W - 1, FF2), F32)
    if sample:
        hist, scores, keep, na, nb, w_out = smp
        groups, rows = tile // seq_len, seq_len
        grid = (n_rows // tile,)
        row_map = lambda n: (n, 0)
        hist_spec = pl.BlockSpec((groups, CONV_W - 1, FF2), lambda n: (n, 0, 0))
        extra_in = [hist_spec, pl.BlockSpec((tile, scores.shape[1]), row_map),
                    pl.BlockSpec((tile, keep.shape[1]), row_map), _full(na.shape), _full(nb.shape), _full(w_out.shape)]
        extra_args = [hist, scores, keep, na, nb, w_out]
        scratch = [pltpu.VMEM((tile, 2 * GROUP_W), BF16), pltpu.VMEM((tile, D_MODEL), F32)]
    else:
        assert seq_len % tile == 0
        groups, rows = 1, tile
        steps = seq_len // tile
        grid = (n_seq, steps)
        row_map = lambda b, i: (b * steps + i, 0)
        hist_spec = pl.BlockSpec((1, CONV_W - 1, FF2), lambda b, i: (b, 0, 0))
        extra_in, extra_args = [], []
        scratch = [pltpu.VMEM((1, SUBLANES, FF2), F32)]
    in_specs = [pl.BlockSpec((tile, D_MODEL), row_map), _full(w2.shape), _full(w_up.shape), _full(cw.shape),
                _full(cb.shape), _full(w_down.shape), _full(wf.shape)] + extra_in
    args = [x, w2, w_up, cw, cb, w_down, wf] + extra_args
    return pl.pallas_call(
        functools.partial(_ffn_kernel, sample=sample, groups=groups, rows=rows),
        grid=grid,
        in_specs=in_specs,
        out_specs=[pl.BlockSpec((tile, D_MODEL), row_map), hist_spec],
        out_shape=[jax.ShapeDtypeStruct((n_rows, D_MODEL), F32), hist_shape],
        scratch_shapes=[pltpu.VMEM((tile, D_MODEL), BF16), pltpu.VMEM((tile, D_FF), BF16)] + scratch,
        compiler_params=pltpu.CompilerParams(dimension_semantics=("arbitrary",) * len(grid),
                                             vmem_limit_bytes=VMEM_LIMIT),
        name="ffn_sample" if sample else "ffn_prompt",
    )(*args)


def kernel(x_prompt, x_sample, state_hgrn, state_ret, state_conv, w_norm1, w_in, hgrn_lb, hgrn_norm_w, ret_norm_w,
           w_out, w_norm2, w_ffn_in, conv_w, conv_b, w_ffn_out, w_norm_f):
    assert w_in.shape == (1, D_MODEL, IN_WIDTH) and hgrn_lb.shape == (2, GROUP_W)
    n_seq, seq_len, _ = x_prompt.shape
    n_smp, smp_len, _ = x_sample.shape
    xs = x_sample.reshape(n_smp * smp_len, D_MODEL)

    smp_cdec, smp, w_in_bf, w_out_bf = _smp_score_call(
        xs, (w_norm1, w_in, hgrn_lb, hgrn_norm_w, ret_norm_w, w_out), seq_len=smp_len, tile=256)
    mix_w = (w_norm1, w_in_bf, hgrn_lb, hgrn_norm_w, ret_norm_w, w_out_bf)
    x1, ha_p, rb_p, w_up, w_down, smp_scores, ha_s, rb_s = _mix_call(
        x_prompt.reshape(n_seq * seq_len, D_MODEL), mix_w, (w_ffn_in, w_ffn_out), smp,
        (state_hgrn[0], state_ret[0]), smp_cdec, n_seq=n_seq, seq_len=seq_len, tile=512, chunks=2, smp_len=smp_len)
    ffn_w = (w_norm2, w_up, conv_w[0], conv_b, w_down, w_norm_f.reshape(1, D_MODEL))
    y_p, cv_p = _ffn_call(x1, *ffn_w, None, n_seq=n_seq, seq_len=seq_len, tile=512)
    y_s, cv_s = _ffn_call(xs, *ffn_w, (state_conv[0], smp_scores, smp[-1], hgrn_norm_w, ret_norm_w, w_out_bf),
                          n_seq=n_smp, seq_len=smp_len, tile=256)
    return (y_p.reshape(x_prompt.shape), y_s.reshape(x_sample.shape), ha_p[None], rb_p[None], cv_p[None],
            ha_s[None], rb_s[None], cv_s[None])
```

```python
import functools

import numpy as np
import jax
import jax.numpy as jnp
from jax import lax
from jax.experimental import pallas as pl
from jax.experimental.pallas import tpu as pltpu

F32 = jnp.float32
BF16 = jnp.bfloat16

D_MODEL = 1024
N_HEADS = 4
D_HEAD = 128
GROUP_W = N_HEADS * D_HEAD
IN_WIDTH = 8 * GROUP_W
D_FF = 2816
FF2 = 2 * D_FF
CONV_W = 3
PAST_LEN = 16384
ROPE_BASE = 10000.0
EPS = 1e-6
LOG2E = 1.4426950408889634

SUBLANES = 8
BF16_ROWS = 16
PROJ_COLS = 512
FF_COLS = 256
ROW_PARTS = 2
KEPT_GROUPS = (2, 3, 6, 7)
VMEM_LIMIT = 56 * 1024 * 1024


def _mm(a, b):
    return jnp.dot(a, b, preferred_element_type=F32)


def _mm_nt(a, b):
    return lax.dot_general(a, b, (((1,), (1,)), ((), ())), preferred_element_type=F32)


def _mm_tn(a, b):
    return lax.dot_general(a, b, (((0,), (0,)), ((), ())), preferred_element_type=F32)


def _sigmoid(x):
    return 1.0 / (1.0 + jnp.exp(-x))


def _rmsnorm(x, w):
    return x * lax.rsqrt(jnp.mean(x * x, axis=-1, keepdims=True) + EPS) * w


def _groupnorm(x, w):
    xc = x - jnp.mean(x, axis=-1, keepdims=True)
    return xc * lax.rsqrt(jnp.mean(xc * xc, axis=-1, keepdims=True) + EPS) * w


def _chunk_consts(chunk, seq_len):
    nlev = int(np.log2(seq_len))
    assert 1 << nlev == seq_len and chunk % seq_len == 0
    r = np.arange(chunk)
    rr, cc = r[:, None], r[None, :]
    same_seq = (rr // seq_len) == (cc // seq_len)
    cum = (same_seq & (cc <= rr)).astype(np.float32)
    x = rr ^ cc
    bit_len = np.where(x > 0, np.floor(np.log2(np.maximum(x, 1))).astype(np.int64) + 1, 0)
    level = np.where(same_seq & (cc <= rr), bit_len, -1).astype(np.int32)

    pos = r % seq_len
    log_gamma = np.log1p(-np.exp2(-5.0 - np.arange(N_HEADS, dtype=np.float64)))[:, None, None]
    rel = (pos[:, None] - pos[None, :]).astype(np.float64)[None]
    causal = (same_seq & (cc <= rr))[None]
    dec = np.where(causal, np.exp(np.where(causal, rel, 0.0) * log_gamma), 0.0)
    ones = np.ones((1, 1, D_HEAD))
    inner = np.exp((pos + 1.0)[None, :, None] * log_gamma) * ones
    sdec = np.exp((seq_len - 1.0 - pos)[None, :, None] * log_gamma) * ones
    cdec = tuple(float(v) for v in np.exp(seq_len * log_gamma[:, 0, 0]))
    consts = (jnp.asarray(cum, BF16), jnp.asarray(level, BF16), jnp.asarray(dec, F32), jnp.asarray(inner, F32),
              jnp.asarray(sdec, F32))
    return nlev, cdec, consts


def _rope_tables(pos):
    half = D_HEAD // 2
    inv = 1.0 / (ROPE_BASE ** (np.arange(half, dtype=np.float64) / half))
    ang = np.asarray(pos, np.float64)[:, None] * inv[None, :]
    cos, sin = np.cos(ang), np.sin(ang)
    return (jnp.asarray(np.concatenate([cos, cos], axis=-1), F32),
            jnp.asarray(np.concatenate([-sin, sin], axis=-1), F32))


def _mix_kernel(*refs, sample, tile, chunks, seq_rows, nlev, cdec, smp_cdec, smp_seq_per_step):
    (x_ref, cos_ref, sin_ref, w1_ref, win_ref, lbp_ref, na_ref, nb_ref, wout_ref,
     cum_ref, level_ref, dec_ref, inner_ref, sdec_ref) = refs[:14]
    refs = refs[14:]
    if sample:
        o_scr, qe_scr, kh_scr, ex_scr, qi_scr, ks_scr, keep_ref, win_bf_ref, wout_bf_ref = refs[:9]
        h_scr, proj_scr, d_scr, k_scr = refs[9:]

        @pl.when(pl.program_id(0) == 0)
        def _():
            for n in range(0, IN_WIDTH, PROJ_COLS):
                win_bf_ref[:, n:n + PROJ_COLS] = win_ref[0, :, n:n + PROJ_COLS].astype(BF16)
            wout_bf_ref[...] = wout_ref[0].astype(BF16)

        win_ref = win_bf_ref
    else:
        wide_refs = refs[:2]
        qe_ref, kh_ref, ex_ref, qi_ref, ks_ref, keep_in_ref, os_in_ref, sa_in_ref, sb_in_ref = refs[2:11]
        x1_ref, sa_out_ref, sb_out_ref = refs[11:14]
        narrow_refs = refs[14:16]
        os_out_ref, sas_out_ref, sbs_out_ref = refs[16:19]
        h_scr, proj_scr, o_scr, d_scr, k_scr, sa_scr, sb_scr = refs[19:]
        step = pl.program_id(1)

        @pl.when(step == 0)
        def _():
            sa_scr[...] = jnp.zeros_like(sa_scr)
            sb_scr[...] = jnp.zeros_like(sb_scr)

    C = tile // chunks

    def col(group, h):
        return slice(group * GROUP_W + h * D_HEAD, group * GROUP_W + (h + 1) * D_HEAD)

    def head(h):
        return slice(h * D_HEAD, (h + 1) * D_HEAD)

    def block_rows(x, m, row):
        x3 = x.reshape(C // m, m, D_HEAD)
        return jnp.broadcast_to(x3[:, row:row + 1, :], x3.shape).reshape(C, D_HEAD)

    def upper_rows(x, m):
        return x.reshape(C // m, 2, m // 2, x.shape[-1])[:, 1].reshape(C // 2, x.shape[-1])

    def put_upper_rows(x, xu, m):
        x4 = x.reshape(C // m, 2, m // 2, x.shape[-1])
        xu4 = xu.reshape(C // m, 1, m // 2, x.shape[-1])
        return jnp.concatenate([x4[:, 0:1], xu4], axis=1).reshape(C, x.shape[-1])

    def project(rs):
        h_scr[rs, :] = _rmsnorm(x_ref[rs, :], w1_ref[...]).astype(BF16)
        for n in range(0, IN_WIDTH, PROJ_COLS):
            proj_scr[rs, n:n + PROJ_COLS] = _mm(h_scr[rs, :], win_ref[:, n:n + PROJ_COLS])

    def score(rs):
        lb0, lb1 = lbp_ref[0:1, :], lbp_ref[1:2, :]
        lb_max = jnp.maximum(lb0, lb1)
        e0, e1 = jnp.exp(lb0 - lb_max), jnp.exp(lb1 - lb_max)
        lb = e0 / (e0 + e1)

        row_id = lax.broadcasted_iota(jnp.int32, (C, D_HEAD), 0)

        f = lb + (1.0 - lb) * _sigmoid(proj_scr[rs, GROUP_W:2 * GROUP_W])
        k_scr[rs, :] = 1.0 - f
        g = jnp.log(f)
        g_hi = g.astype(BF16)
        g_lo = (g - g_hi.astype(F32)).astype(BF16)
        d_scr[rs, :] = _mm(cum_ref[...], g_hi) + _mm(cum_ref[...], g_lo)

        signs = [jnp.where((row_id & (1 << (lev - 1))) != 0, LOG2E, -LOG2E) for lev in range(3, nlev + 1)]

        for h in range(N_HEADS):
            hs = head(h)
            q = proj_scr[rs, col(0, h)]
            k = k_scr[rs, hs]
            v = proj_scr[rs, col(2, h)].astype(BF16)
            b = d_scr[rs, hs]
            a = jnp.where(level_ref[...] == 0, _mm_nt(q.astype(BF16), k.astype(BF16)).astype(BF16), 0.0)
            for lev in range(1, nlev + 1):
                m = 1 << lev
                upper = (row_id & (m // 2)) != 0
                if lev == 1:
                    z = jnp.where(upper, q * (1.0 - k), k)
                elif lev == 2:
                    fh = 1.0 - k
                    pos4 = row_id & 3
                    decay = jnp.where(pos4 == 0, pltpu.roll(fh, C - 1, 0),
                                      jnp.where(pos4 == 1, 1.0, jnp.where(pos4 == 2, fh, fh * pltpu.roll(fh, 1, 0))))
                    z = jnp.where(upper, q, k) * decay
                else:
                    z = jnp.where(upper, q, k) * jnp.exp2((b - block_rows(b, m, m // 2 - 1)) * signs[lev - 3])
                if m < 2 * BF16_ROWS:
                    z = z.astype(BF16)
                    a = jnp.where(level_ref[...] == lev, _mm_nt(z, z).astype(BF16), a)
                else:
                    zq = upper_rows(z, m).astype(BF16)
                    zk = z.astype(BF16)
                    if m > D_HEAD:
                        p = [_mm_nt(zq[i * (m // 2):(i + 1) * (m // 2)], zk[i * m:i * m + m // 2])
                             for i in range(C // m)]
                        width = m // 2
                    else:
                        p = [_mm_nt(zq[i * (D_HEAD // 2):(i + 1) * (D_HEAD // 2)], zk[i * D_HEAD:(i + 1) * D_HEAD])
                             for i in range(C // D_HEAD)]
                        width = D_HEAD
                    full = jnp.concatenate([jnp.concatenate([pi] * (C // width), axis=1) for pi in p], axis=0)
                    lvl_u = upper_rows(level_ref[...], m)
                    a = put_upper_rows(a, jnp.where(lvl_u == lev, full.astype(BF16), upper_rows(a, m)), m)
            o = _mm(a, v)
            eb = jnp.exp(b)
            qe = q * eb
            kh = k * jnp.exp(block_rows(b, seq_rows, seq_rows - 1) - b)
            if sample:
                o_scr[rs, col(0, h)] = o
                qe_scr[rs, hs] = qe
                kh_scr[rs, hs] = kh
                e_all = block_rows(eb, seq_rows, seq_rows - 1)
                e_hi = e_all.astype(BF16).astype(F32)
                e_mid = (e_all - e_hi).astype(BF16).astype(F32)
                e_lo = e_all - e_hi - e_mid
                pos = row_id & (seq_rows - 1)
                ex_scr[rs, hs] = jnp.where(pos == 0, e_hi, jnp.where(pos == 1, e_mid, jnp.where(pos == 2, e_lo, 0.0)))
            else:
                st = sa_scr[h]
                o = o + _mm_nt(qe.astype(BF16), st.astype(BF16))
                sa_scr[h] = st * eb[C - 1:C, :] + _mm_tn(v, kh.astype(BF16))
                gate = proj_scr[rs, col(3, h)]
                o_scr[rs, col(0, h)] = _rmsnorm(o, na_ref[...]) * (gate * _sigmoid(gate))

        cos, sin = cos_ref[rs, :], sin_ref[rs, :]
        for h in range(N_HEADS):
            hs = head(h)
            q = proj_scr[rs, col(4, h)]
            k = proj_scr[rs, col(5, h)]
            v = proj_scr[rs, col(6, h)].astype(BF16)
            qr = q * cos + pltpu.roll(q, D_HEAD // 2, 1) * sin
            kr = (k * cos + pltpu.roll(k, D_HEAD // 2, 1) * sin) * (D_HEAD ** -0.5)
            a = _mm_nt(qr.astype(BF16), kr.astype(BF16)) * dec_ref[h]
            o = _mm(a.astype(BF16), v)
            qi = qr * inner_ref[h]
            ks = kr * sdec_ref[h]
            if sample:
                o_scr[rs, col(1, h)] = o
                qi_scr[rs, hs] = qi
                ks_scr[rs, hs] = ks
            else:
                st = sb_scr[h]
                o = o + _mm(qi.astype(BF16), st.astype(BF16))
                sb_scr[h] = cdec[h] * st + _mm_tn(ks.astype(BF16), v)
                gate = proj_scr[rs, col(7, h)]
                o_scr[rs, col(1, h)] = _groupnorm(o, nb_ref[...]) * (gate * _sigmoid(gate))

        if sample:
            for slot, group in enumerate(KEPT_GROUPS):
                keep_ref[:, slot * GROUP_W:(slot + 1) * GROUP_W] = proj_scr[rs, group * GROUP_W:(group + 1) * GROUP_W]

    def kept(group, h):
        slot = KEPT_GROUPS.index(group)
        return slice(slot * GROUP_W + h * D_HEAD, slot * GROUP_W + (h + 1) * D_HEAD)

    def apply_states():
        zeros8 = jnp.zeros((SUBLANES, D_HEAD), F32)
        sel8 = jnp.where(lax.broadcasted_iota(jnp.int32, (SUBLANES, D_HEAD), 0) < 3, 1.0, 0.0)

        def pair_readout(lhs_ref, st_ref, j, rows, group, h):
            lhs = jnp.concatenate([lhs_ref[rows, head(h)], lhs_ref[rows, head(h + 1)]], axis=0).astype(BF16)
            w = jnp.concatenate([st_ref[j, h], st_ref[j, h + 1]], axis=1).astype(BF16)
            oo = _mm(lhs, w)
            os_out_ref[rows, col(group, h)] = os_in_ref[rows, col(group, h)] + oo[:SUBLANES, :D_HEAD]
            os_out_ref[rows, col(group, h + 1)] = os_in_ref[rows, col(group, h + 1)] + oo[SUBLANES:, D_HEAD:]

        for j in range(smp_seq_per_step):
            rows = slice(j * SUBLANES, (j + 1) * SUBLANES)
            for h in range(0, N_HEADS, 2):
                pair_readout(qe_ref, sa_in_ref, j, rows, 0, h)
                pair_readout(qi_ref, sb_in_ref, j, rows, 1, h)
            for h in range(N_HEADS):
                hs = head(h)
                lhs = jnp.concatenate([kh_ref[rows, hs], ex_ref[rows, hs]], axis=0).astype(BF16)
                v = keep_in_ref[rows, kept(2, h)]
                rhs = jnp.concatenate([jnp.concatenate([v, zeros8], axis=1),
                                       jnp.concatenate([zeros8, sel8], axis=1)], axis=0).astype(BF16)
                upd = _mm_tn(lhs, rhs)
                sas_out_ref[j, h] = sa_in_ref[j, h] * upd[:, D_HEAD:] + upd[:, :D_HEAD]
                v = keep_in_ref[rows, kept(6, h)].astype(BF16)
                sbs_out_ref[j, h] = smp_cdec[h] * sb_in_ref[j, h] + _mm_tn(ks_ref[rows, hs].astype(BF16), v)

    chunk_rows = [slice(c * C, (c + 1) * C) for c in range(chunks)]
    project(slice(0, tile))
    for rs in chunk_rows:
        score(rs)
    if not sample:
        x1_ref[...] = x_ref[...] + _mm(o_scr[...].astype(BF16), wout_ref[...])
        for wide, narrow in zip(wide_refs, narrow_refs):
            narrow[...] = wide[...].astype(BF16)
        apply_states()

        @pl.when(step == pl.num_programs(1) - 1)
        def _():
            for h in range(N_HEADS):
                sa_out_ref[0, h] = sa_scr[h].T
                sb_out_ref[0, h] = sb_scr[h]


def _full(shape):
    return pl.BlockSpec(shape, lambda *_: (0,) * len(shape))


def _row_block_spec(shape, n_steps, inner_steps):
    _, n_rows, width = shape
    share = next(k for k in (1, 2, 4, 8) if n_steps % k == 0 and n_rows % ((n_steps // k) * BF16_ROWS) == 0)
    return pl.BlockSpec((1, n_rows // (n_steps // share), width),
                        lambda b, i: (0, (b * inner_steps + i) // share, 0))


def _mix_scratch(tile):
    return [pltpu.VMEM((tile, D_MODEL), BF16), pltpu.VMEM((tile, IN_WIDTH), F32)]


def _smp_score_call(x2, weights, *, seq_len, tile):
    n_rows = x2.shape[0]
    assert n_rows % tile == 0 and tile % seq_len == 0
    nlev, cdec, consts = _chunk_consts(tile, seq_len)
    cos, sin = _rope_tables(np.tile(PAST_LEN + np.arange(seq_len), tile // seq_len))
    rows = lambda width: pl.BlockSpec((tile, width), lambda n: (n, 0))
    widths = [2 * GROUP_W] + [GROUP_W] * 5 + [len(KEPT_GROUPS) * GROUP_W]
    args = [x2, cos, sin, *weights, *consts]
    w_in, w_out = weights[1], weights[5]
    narrow_shapes = [w.shape[1:] for w in (w_in, w_out)]
    *outs, w_in_bf, w_out_bf = pl.pallas_call(
        functools.partial(_mix_kernel, sample=True, tile=tile, chunks=1, seq_rows=seq_len, nlev=nlev, cdec=cdec,
                          smp_cdec=None, smp_seq_per_step=0),
        grid=(n_rows // tile,),
        in_specs=[rows(D_MODEL)] + [_full(a.shape) for a in args[1:]],
        out_specs=[rows(w) for w in widths] + [_full(shape) for shape in narrow_shapes],
        out_shape=[jax.ShapeDtypeStruct((n_rows, w), F32) for w in widths]
        + [jax.ShapeDtypeStruct(shape, BF16) for shape in narrow_shapes],
        scratch_shapes=_mix_scratch(tile) + [pltpu.VMEM((tile, GROUP_W), F32)] * 2,
        compiler_params=pltpu.CompilerParams(dimension_semantics=("arbitrary",), vmem_limit_bytes=VMEM_LIMIT),
        name="score_sample",
    )(*args)
    return cdec, outs, w_in_bf, w_out_bf


def _mix_call(x2, weights, to_narrow, smp, smp_states, smp_cdec, *, n_seq, seq_len, tile, chunks, smp_len):
    n_rows = n_seq * seq_len
    assert seq_len % tile == 0 and tile % chunks == 0
    chunk = tile // chunks
    steps = seq_len // tile
    n_steps = n_seq * steps
    nlev, cdec, consts = _chunk_consts(chunk, min(seq_len, chunk))
    cos, sin = _rope_tables(np.arange(seq_len))
    n_smp = smp_states[0].shape[0]
    assert n_smp % n_steps == 0
    seq_per_step = n_smp // n_steps
    smp_rows = seq_per_step * smp_len
    assert smp_rows % SUBLANES == 0

    row_map = lambda b, i: (b * steps + i, 0)
    pos_map = lambda b, i: (i, 0)
    state_spec = pl.BlockSpec((1, N_HEADS, D_HEAD, D_HEAD), lambda b, i: (b, 0, 0, 0))
    smp_state_spec = pl.BlockSpec((seq_per_step, N_HEADS, D_HEAD, D_HEAD), lambda b, i: (b * steps + i, 0, 0, 0))
    smp_spec = lambda a: pl.BlockSpec((smp_rows, a.shape[1]), row_map)
    narrow_specs = [_row_block_spec(w.shape, n_steps, steps) for w in to_narrow]
    smp_scores, *smp_factors, smp_keep = smp
    side_in = [*smp_factors, smp_keep, smp_scores]
    in_specs = ([pl.BlockSpec((tile, D_MODEL), row_map), pl.BlockSpec((tile, D_HEAD), pos_map),
                 pl.BlockSpec((tile, D_HEAD), pos_map)] + [_full(w.shape) for w in weights]
                + [_full(c.shape) for c in consts] + narrow_specs + [smp_spec(a) for a in side_in]
                + [smp_state_spec, smp_state_spec])
    args = [x2, cos, sin, *weights, *consts, *to_narrow, *side_in, *smp_states]
    state_shape = lambda n: jax.ShapeDtypeStruct((n, N_HEADS, D_HEAD, D_HEAD), F32)
    return pl.pallas_call(
        functools.partial(_mix_kernel, sample=False, tile=tile, chunks=chunks, seq_rows=min(seq_len, chunk),
                          nlev=nlev, cdec=cdec, smp_cdec=smp_cdec, smp_seq_per_step=seq_per_step),
        grid=(n_seq, steps),
        in_specs=in_specs,
        out_specs=[pl.BlockSpec((tile, D_MODEL), row_map), state_spec, state_spec] + narrow_specs
        + [smp_spec(smp_scores), smp_state_spec, smp_state_spec],
        out_shape=[jax.ShapeDtypeStruct((n_rows, D_MODEL), F32), state_shape(n_seq), state_shape(n_seq)]
        + [jax.ShapeDtypeStruct(w.shape, BF16) for w in to_narrow]
        + [jax.ShapeDtypeStruct(smp_scores.shape, F32), state_shape(n_smp), state_shape(n_smp)],
        scratch_shapes=_mix_scratch(tile) + [pltpu.VMEM((tile, 2 * GROUP_W), F32)]
        + [pltpu.VMEM((tile, GROUP_W), F32)] * 2 + [pltpu.VMEM((N_HEADS, D_HEAD, D_HEAD), F32)] * 2,
        compiler_params=pltpu.CompilerParams(dimension_semantics=("arbitrary", "arbitrary"),
                                             vmem_limit_bytes=VMEM_LIMIT),
        name="mix_prompt",
    )(*args)


def _ffn_kernel(*refs, sample, groups, rows):
    x_ref, w2_ref, wup_ref, cw_ref, cb_ref, wdown_ref, wf_ref = refs[:7]
    refs = refs[7:]
    if sample:
        (hist_ref, os_ref, keep_ref, na_ref, nb_ref, wout_ref, y_ref, hist_out_ref, h_scr, act_scr, mixed_scr,
         xin) = refs
        for h in range(N_HEADS):
            for group, norm, w_ref in ((0, _rmsnorm, na_ref), (1, _groupnorm, nb_ref)):
                cols = slice(group * GROUP_W + h * D_HEAD, group * GROUP_W + (h + 1) * D_HEAD)
                slot = KEPT_GROUPS.index(4 * group + 3)
                gate = keep_ref[:, slot * GROUP_W + h * D_HEAD:slot * GROUP_W + (h + 1) * D_HEAD]
                mixed_scr[:, cols] = (norm(os_ref[:, cols], w_ref[...]) * (gate * _sigmoid(gate))).astype(BF16)
        xin[...] = x_ref[...] + _mm(mixed_scr[...], wout_ref[...])
    else:
        y_ref, hist_out_ref, h_scr, act_scr, tail_scr = refs
        xin = x_ref
        step = pl.program_id(1)

        @pl.when(step == 0)
        def _():
            tail_scr[...] = jnp.zeros_like(tail_scr)

    G, L, P = groups, rows, SUBLANES
    parts = 1 if sample else ROW_PARTS
    LP = L // parts
    row_id = lax.broadcasted_iota(jnp.int32, (G, P, FF_COLS), 1)

    def shifted(up, prev2, prev1):
        r1, r2 = pltpu.roll(up, 1, 1), pltpu.roll(up, 2, 1)
        top1 = jnp.where(row_id == 0, prev1, r1[:, :P])
        top2 = jnp.where(row_id == 0, prev2, jnp.where(row_id == 1, prev1, r2[:, :P]))
        if LP == P:
            return top1, top2
        return jnp.concatenate([top1, r1[:, P:]], axis=1), jnp.concatenate([top2, r2[:, P:]], axis=1)

    tails = {}
    for part in range(parts):
        rows_p = slice(part * G * LP, (part + 1) * G * LP)
        h_scr[rows_p, :] = _rmsnorm(xin[rows_p, :], w2_ref[...]).astype(BF16)
        for n in range(0, D_FF, FF_COLS):
            conv = []
            for cols in (slice(n, n + FF_COLS), slice(D_FF + n, D_FF + n + FF_COLS)):
                up = _mm(h_scr[rows_p, :], wup_ref[0, :, cols]).reshape(G, LP, FF_COLS)
                if sample:
                    prev2, prev1 = hist_ref[:, 0:1, cols], hist_ref[:, 1:2, cols]
                    hist_out_ref[:, :, cols] = up[:, LP - 2:, :]
                else:
                    if part == 0:
                        prev2, prev1 = tail_scr[:, P - 2:P - 1, cols], tail_scr[:, P - 1:P, cols]
                    else:
                        prev = tails[cols.start]
                        prev2, prev1 = prev[:, P - 2:P - 1, :], prev[:, P - 1:P, :]
                    tails[cols.start] = up[:, LP - P:, :]
                    if part == parts - 1:
                        tail_scr[:, :, cols] = up[:, LP - P:, :]
                sh1, sh2 = shifted(up, prev2, prev1)
                conv.append(cb_ref[:, cols] + cw_ref[0:1, cols] * sh2 + cw_ref[1:2, cols] * sh1 + cw_ref[2:3, cols] * up)
            u, g = (c.reshape(G * LP, FF_COLS).astype(BF16) for c in conv)
            one = jnp.ones((), BF16)
            act_scr[rows_p, n:n + FF_COLS] = (g * (one / (one + jnp.exp(-g)))) * u

        x2 = xin[rows_p, :] + _mm(act_scr[rows_p, :], wdown_ref[0])
        y_ref[rows_p, :] = _rmsnorm(x2, wf_ref[...])

    if not sample:
        @pl.when(step == pl.num_programs(1) - 1)
        def _():
            hist_out_ref[...] = tail_scr[:, P - 2:, :]


def _ffn_call(x, w2, w_up, cw, cb, w_down, wf, smp, *, n_seq, seq_len, tile):
    sample = smp is not None
    n_rows = n_seq * seq_len
    hist_shape = jax.ShapeDtypeStruct((n_seq, CONV_W - 1, FF2), F32)
    if sample:
        hist, scores, keep, na, nb, w_out = smp
        groups, rows = tile // seq_len, seq_len
        grid = (n_rows // tile,)
        row_map = lambda n: (n, 0)
        hist_spec = pl.BlockSpec((groups, CONV_W - 1, FF2), lambda n: (n, 0, 0))
        extra_in = [hist_spec, pl.BlockSpec((tile, scores.shape[1]), row_map),
                    pl.BlockSpec((tile, keep.shape[1]), row_map), _full(na.shape), _full(nb.shape), _full(w_out.shape)]
        extra_args = [hist, scores, keep, na, nb, w_out]
        scratch = [pltpu.VMEM((tile, 2 * GROUP_W), BF16), pltpu.VMEM((tile, D_MODEL), F32)]
    else:
        assert seq_len % tile == 0
        groups, rows = 1, tile
        steps = seq_len // tile
        grid = (n_seq, steps)
        row_map = lambda b, i: (b * steps + i, 0)
        hist_spec = pl.BlockSpec((1, CONV_W - 1, FF2), lambda b, i: (b, 0, 0))
        extra_in, extra_args = [], []
        scratch = [pltpu.VMEM((1, SUBLANES, FF2), F32)]
    in_specs = [pl.BlockSpec((tile, D_MODEL), row_map), _full(w2.shape), _full(w_up.shape), _full(cw.shape),
                _full(cb.shape), _full(w_down.shape), _full(wf.shape)] + extra_in
    args = [x, w2, w_up, cw, cb, w_down, wf] + extra_args
    return pl.pallas_call(
        functools.partial(_ffn_kernel, sample=sample, groups=groups, rows=rows),
        grid=grid,
        in_specs=in_specs,
        out_specs=[pl.BlockSpec((tile, D_MODEL), row_map), hist_spec],
        out_shape=[jax.ShapeDtypeStruct((n_rows, D_MODEL), F32), hist_shape],
        scratch_shapes=[pltpu.VMEM((tile, D_MODEL), BF16), pltpu.VMEM((tile, D_FF), BF16)] + scratch,
        compiler_params=pltpu.CompilerParams(dimension_semantics=("arbitrary",) * len(grid),
                                             vmem_limit_bytes=VMEM_LIMIT),
        name="ffn_sample" if sample else "ffn_prompt",
    )(*args)


def kernel(x_prompt, x_sample, state_hgrn, state_ret, state_conv, w_norm1, w_in, hgrn_lb, hgrn_norm_w, ret_norm_w,
           w_out, w_norm2, w_ffn_in, conv_w, conv_b, w_ffn_out, w_norm_f):
    assert w_in.shape == (1, D_MODEL, IN_WIDTH) and hgrn_lb.shape == (2, GROUP_W)
    n_seq, seq_len, _ = x_prompt.shape
    n_smp, smp_len, _ = x_sample.shape
    xs = x_sample.reshape(n_smp * smp_len, D_MODEL)

    smp_cdec, smp, w_in_bf, w_out_bf = _smp_score_call(
        xs, (w_norm1, w_in, hgrn_lb, hgrn_norm_w, ret_norm_w, w_out), seq_len=smp_len, tile=256)
    mix_w = (w_norm1, w_in_bf, hgrn_lb, hgrn_norm_w, ret_norm_w, w_out_bf)
    x1, ha_p, rb_p, w_up, w_down, smp_scores, ha_s, rb_s = _mix_call(
        x_prompt.reshape(n_seq * seq_len, D_MODEL), mix_w, (w_ffn_in, w_ffn_out), smp,
        (state_hgrn[0], state_ret[0]), smp_cdec, n_seq=n_seq, seq_len=seq_len, tile=512, chunks=2, smp_len=smp_len)
    ffn_w = (w_norm2, w_up, conv_w[0], conv_b, w_down, w_norm_f.reshape(1, D_MODEL))
    y_p, cv_p = _ffn_call(x1, *ffn_w, None, n_seq=n_seq, seq_len=seq_len, tile=512)
    y_s, cv_s = _ffn_call(xs, *ffn_w, (state_conv[0], smp_scores, smp[-1], hgrn_norm_w, ret_norm_w, w_out_bf),
                          n_seq=n_smp, seq_len=smp_len, tile=256)
    return (y_p.reshape(x_prompt.shape), y_s.reshape(x_sample.shape), ha_p[None], rb_p[None], cv_p[None],
            ha_s[None], rb_s[None], cv_s[None])
```

```python
import functools

import numpy as np
import jax
import jax.numpy as jnp
from jax import lax
from jax.experimental import pallas as pl
from jax.experimental.pallas import tpu as pltpu

F32 = jnp.float32
BF16 = jnp.bfloat16

D_MODEL = 1024
N_HEADS = 4
D_HEAD = 128
GROUP_W = N_HEADS * D_HEAD
IN_WIDTH = 8 * GROUP_W
D_FF = 2816
FF2 = 2 * D_FF
CONV_W = 3
PAST_LEN = 16384
ROPE_BASE = 10000.0
EPS = 1e-6
LOG2E = 1.4426950408889634

SUBLANES = 8
BF16_ROWS = 16
PROJ_COLS = 512
FF_COLS = 256
ROW_PARTS = 2
KEPT_GROUPS = (2, 3, 6, 7)
VMEM_LIMIT = 56 * 1024 * 1024


def _mm(a, b):
    return jnp.dot(a, b, preferred_element_type=F32)


def _mm_nt(a, b):
    return lax.dot_general(a, b, (((1,), (1,)), ((), ())), preferred_element_type=F32)


def _mm_tn(a, b):
    return lax.dot_general(a, b, (((0,), (0,)), ((), ())), preferred_element_type=F32)


def _sigmoid(x):
    return 1.0 / (1.0 + jnp.exp(-x))


def _rmsnorm(x, w):
    return x * lax.rsqrt(jnp.mean(x * x, axis=-1, keepdims=True) + EPS) * w


def _groupnorm(x, w):
    xc = x - jnp.mean(x, axis=-1, keepdims=True)
    return xc * lax.rsqrt(jnp.mean(xc * xc, axis=-1, keepdims=True) + EPS) * w


def _chunk_consts(chunk, seq_len):
    nlev = int(np.log2(seq_len))
    assert 1 << nlev == seq_len and chunk % seq_len == 0
    r = np.arange(chunk)
    rr, cc = r[:, None], r[None, :]
    same_seq = (rr // seq_len) == (cc // seq_len)
    cum = (same_seq & (cc <= rr)).astype(np.float32)
    x = rr ^ cc
    bit_len = np.where(x > 0, np.floor(np.log2(np.maximum(x, 1))).astype(np.int64) + 1, 0)
    level = np.where(same_seq & (cc <= rr), bit_len, -1).astype(np.int32)

    pos = r % seq_len
    log_gamma = np.log1p(-np.exp2(-5.0 - np.arange(N_HEADS, dtype=np.float64)))[:, None, None]
    rel = (pos[:, None] - pos[None, :]).astype(np.float64)[None]
    causal = (same_seq & (cc <= rr))[None]
    dec = np.where(causal, np.exp(np.where(causal, rel, 0.0) * log_gamma), 0.0)
    ones = np.ones((1, 1, D_HEAD))
    inner = np.exp((pos + 1.0)[None, :, None] * log_gamma) * ones
    sdec = np.exp((seq_len - 1.0 - pos)[None, :, None] * log_gamma) * ones
    cdec = tuple(float(v) for v in np.exp(seq_len * log_gamma[:, 0, 0]))
    consts = (jnp.asarray(cum, BF16), jnp.asarray(level, BF16), jnp.asarray(dec, F32), jnp.asarray(inner, F32),
              jnp.asarray(sdec, F32))
    return nlev, cdec, consts


def _rope_tables(pos):
    half = D_HEAD // 2
    inv = 1.0 / (ROPE_BASE ** (np.arange(half, dtype=np.float64) / half))
    ang = np.asarray(pos, np.float64)[:, None] * inv[None, :]
    cos, sin = np.cos(ang), np.sin(ang)
    return (jnp.asarray(np.concatenate([cos, cos], axis=-1), F32),
            jnp.asarray(np.concatenate([-sin, sin], axis=-1), F32))


def _mix_kernel(*refs, sample, tile, chunks, seq_rows, nlev, cdec, smp_cdec, smp_seq_per_step):
    (x_ref, cos_ref, sin_ref, w1_ref, win_ref, lbp_ref, na_ref, nb_ref, wout_ref,
     cum_ref, level_ref, dec_ref, inner_ref, sdec_ref) = refs[:14]
    refs = refs[14:]
    if sample:
        o_scr, qe_scr, kh_scr, ex_scr, qi_scr, ks_scr, keep_ref, win_bf_ref, wout_bf_ref = refs[:9]
        h_scr, proj_scr, d_scr, k_scr = refs[9:]

        @pl.when(pl.program_id(0) == 0)
        def _():
            for n in range(0, IN_WIDTH, PROJ_COLS):
                win_bf_ref[:, n:n + PROJ_COLS] = win_ref[0, :, n:n + PROJ_COLS].astype(BF16)
            wout_bf_ref[...] = wout_ref[0].astype(BF16)

        win_ref = win_bf_ref
    else:
        wide_refs = refs[:2]
        qe_ref, kh_ref, ex_ref, qi_ref, ks_ref, keep_in_ref, os_in_ref, sa_in_ref, sb_in_ref = refs[2:11]
        x1_ref, sa_out_ref, sb_out_ref = refs[11:14]
        narrow_refs = refs[14:16]
        os_out_ref, sas_out_ref, sbs_out_ref = refs[16:19]
        h_scr, proj_scr, o_scr, d_scr, k_scr, sa_scr, sb_scr = refs[19:]
        step = pl.program_id(1)

        @pl.when(step == 0)
        def _():
            sa_scr[...] = jnp.zeros_like(sa_scr)
            sb_scr[...] = jnp.zeros_like(sb_scr)

    C = tile // chunks

    def col(group, h):
        return slice(group * GROUP_W + h * D_HEAD, group * GROUP_W + (h + 1) * D_HEAD)

    def head(h):
        return slice(h * D_HEAD, (h + 1) * D_HEAD)

    def block_rows(x, m, row):
        x3 = x.reshape(C // m, m, D_HEAD)
        return jnp.broadcast_to(x3[:, row:row + 1, :], x3.shape).reshape(C, D_HEAD)

    def upper_rows(x, m):
        return x.reshape(C // m, 2, m // 2, x.shape[-1])[:, 1].reshape(C // 2, x.shape[-1])

    def put_upper_rows(x, xu, m):
        x4 = x.reshape(C // m, 2, m // 2, x.shape[-1])
        xu4 = xu.reshape(C // m, 1, m // 2, x.shape[-1])
        return jnp.concatenate([x4[:, 0:1], xu4], axis=1).reshape(C, x.shape[-1])

    def project(rs):
        h_scr[rs, :] = _rmsnorm(x_ref[rs, :], w1_ref[...]).astype(BF16)
        for n in range(0, IN_WIDTH, PROJ_COLS):
            proj_scr[rs, n:n + PROJ_COLS] = _mm(h_scr[rs, :], win_ref[:, n:n + PROJ_COLS])

    def score(rs):
        lb0, lb1 = lbp_ref[0:1, :], lbp_ref[1:2, :]
        lb_max = jnp.maximum(lb0, lb1)
        e0, e1 = jnp.exp(lb0 - lb_max), jnp.exp(lb1 - lb_max)
        lb = e0 / (e0 + e1)

        row_id = lax.broadcasted_iota(jnp.int32, (C, D_HEAD), 0)

        f = lb + (1.0 - lb) * _sigmoid(proj_scr[rs, GROUP_W:2 * GROUP_W])
        k_scr[rs, :] = 1.0 - f
        g = jnp.log(f)
        g_hi = g.astype(BF16)
        g_lo = (g - g_hi.astype(F32)).astype(BF16)
        d_scr[rs, :] = _mm(cum_ref[...], g_hi) + _mm(cum_ref[...], g_lo)

        signs = [jnp.where((row_id & (1 << (lev - 1))) != 0, LOG2E, -LOG2E) for lev in range(3, nlev + 1)]

        for h in range(N_HEADS):
            hs = head(h)
            q = proj_scr[rs, col(0, h)]
            k = k_scr[rs, hs]
            v_t = proj_scr[rs, col(2, h)].T.astype(BF16)
            b = d_scr[rs, hs]
            a = jnp.where(level_ref[...] == 0, _mm_nt(q.astype(BF16), k.astype(BF16)).astype(BF16), 0.0)
            for lev in range(1, nlev + 1):
                m = 1 << lev
                upper = (row_id & (m // 2)) != 0
                if lev == 1:
                    z = jnp.where(upper, q * (1.0 - k), k)
                elif lev == 2:
                    fh = 1.0 - k
                    pos4 = row_id & 3
                    decay = jnp.where(pos4 == 0, pltpu.roll(fh, C - 1, 0),
                                      jnp.where(pos4 == 1, 1.0, jnp.where(pos4 == 2, fh, fh * pltpu.roll(fh, 1, 0))))
                    z = jnp.where(upper, q, k) * decay
                else:
                    z = jnp.where(upper, q, k) * jnp.exp2((b - block_rows(b, m, m // 2 - 1)) * signs[lev - 3])
                if m < 2 * BF16_ROWS:
                    z = z.astype(BF16)
                    a = jnp.where(level_ref[...] == lev, _mm_nt(z, z).astype(BF16), a)
                else:
                    zq = upper_rows(z, m).astype(BF16)
                    zk = z.astype(BF16)
                    if m > D_HEAD:
                        p = [_mm_nt(zq[i * (m // 2):(i + 1) * (m // 2)], zk[i * m:i * m + m // 2])
                             for i in range(C // m)]
                        width = m // 2
                    else:
                        p = [_mm_nt(zq[i * (D_HEAD // 2):(i + 1) * (D_HEAD // 2)], zk[i * D_HEAD:(i + 1) * D_HEAD])
                             for i in range(C // D_HEAD)]
                        width = D_HEAD
                    full = jnp.concatenate([jnp.concatenate([pi] * (C // width), axis=1) for pi in p], axis=0)
                    lvl_u = upper_rows(level_ref[...], m)
                    a = put_upper_rows(a, jnp.where(lvl_u == lev, full.astype(BF16), upper_rows(a, m)), m)
            o_t = _mm_nt(v_t, a)
            eb = jnp.exp(b)
            qe = q * eb
            kh = k * jnp.exp(block_rows(b, seq_rows, seq_rows - 1) - b)
            if sample:
                o_scr[rs, col(0, h)] = o_t.T
                qe_scr[rs, hs] = qe
                kh_scr[rs, hs] = kh
                e_all = block_rows(eb, seq_rows, seq_rows - 1)
                e_hi = e_all.astype(BF16).astype(F32)
                e_mid = (e_all - e_hi).astype(BF16).astype(F32)
                e_lo = e_all - e_hi - e_mid
                pos = row_id & (seq_rows - 1)
                ex_scr[rs, hs] = jnp.where(pos == 0, e_hi, jnp.where(pos == 1, e_mid, jnp.where(pos == 2, e_lo, 0.0)))
            else:
                st = sa_scr[h]
                o = (o_t + _mm_nt(st.astype(BF16), qe.astype(BF16))).T
                sa_scr[h] = st * eb[C - 1:C, :] + _mm(v_t, kh.astype(BF16))
                gate = proj_scr[rs, col(3, h)]
                o_scr[rs, col(0, h)] = _rmsnorm(o, na_ref[...]) * (gate * _sigmoid(gate))

        cos, sin = cos_ref[rs, :], sin_ref[rs, :]
        for h in range(N_HEADS):
            hs = head(h)
            q = proj_scr[rs, col(4, h)]
            k = proj_scr[rs, col(5, h)]
            v_t = proj_scr[rs, col(6, h)].T.astype(BF16)
            qr = q * cos + pltpu.roll(q, D_HEAD // 2, 1) * sin
            kr = (k * cos + pltpu.roll(k, D_HEAD // 2, 1) * sin) * (D_HEAD ** -0.5)
            a = _mm_nt(qr.astype(BF16), kr.astype(BF16)) * dec_ref[h]
            o_t = _mm_nt(v_t, a.astype(BF16))
            qi = qr * inner_ref[h]
            ks = kr * sdec_ref[h]
            if sample:
                o_scr[rs, col(1, h)] = o_t.T
                qi_scr[rs, hs] = qi
                ks_scr[rs, hs] = ks
            else:
                st = sb_scr[h]
                o = (o_t + _mm_nt(st.astype(BF16), qi.astype(BF16))).T
                sb_scr[h] = cdec[h] * st + _mm(v_t, ks.astype(BF16))
                gate = proj_scr[rs, col(7, h)]
                o_scr[rs, col(1, h)] = _groupnorm(o, nb_ref[...]) * (gate * _sigmoid(gate))

        if sample:
            for slot, group in enumerate(KEPT_GROUPS):
                keep_ref[:, slot * GROUP_W:(slot + 1) * GROUP_W] = proj_scr[rs, group * GROUP_W:(group + 1) * GROUP_W]

    def kept(group, h):
        slot = KEPT_GROUPS.index(group)
        return slice(slot * GROUP_W + h * D_HEAD, slot * GROUP_W + (h + 1) * D_HEAD)

    def apply_states():
        zeros8 = jnp.zeros((SUBLANES, D_HEAD), F32)
        sel8 = jnp.where(lax.broadcasted_iota(jnp.int32, (SUBLANES, D_HEAD), 0) < 3, 1.0, 0.0)

        def pair_readout(lhs_ref, st_ref, j, rows, group, h):
            lhs = jnp.concatenate([lhs_ref[rows, head(h)], lhs_ref[rows, head(h + 1)]], axis=0).astype(BF16)
            w = jnp.concatenate([st_ref[j, h], st_ref[j, h + 1]], axis=1).astype(BF16)
            oo = _mm(lhs, w)
            os_out_ref[rows, col(group, h)] = os_in_ref[rows, col(group, h)] + oo[:SUBLANES, :D_HEAD]
            os_out_ref[rows, col(group, h + 1)] = os_in_ref[rows, col(group, h + 1)] + oo[SUBLANES:, D_HEAD:]

        for j in range(smp_seq_per_step):
            rows = slice(j * SUBLANES, (j + 1) * SUBLANES)
            for h in range(0, N_HEADS, 2):
                pair_readout(qe_ref, sa_in_ref, j, rows, 0, h)
                pair_readout(qi_ref, sb_in_ref, j, rows, 1, h)
            for h in range(N_HEADS):
                hs = head(h)
                lhs = jnp.concatenate([kh_ref[rows, hs], ex_ref[rows, hs]], axis=0).astype(BF16)
                v = keep_in_ref[rows, kept(2, h)]
                rhs = jnp.concatenate([jnp.concatenate([v, zeros8], axis=1),
                                       jnp.concatenate([zeros8, sel8], axis=1)], axis=0).astype(BF16)
                upd = _mm_tn(lhs, rhs)
                sas_out_ref[j, h] = sa_in_ref[j, h] * upd[:, D_HEAD:] + upd[:, :D_HEAD]
                v = keep_in_ref[rows, kept(6, h)].astype(BF16)
                sbs_out_ref[j, h] = smp_cdec[h] * sb_in_ref[j, h] + _mm_tn(ks_ref[rows, hs].astype(BF16), v)

    chunk_rows = [slice(c * C, (c + 1) * C) for c in range(chunks)]
    for rs in chunk_rows:
        project(rs)
    for rs in chunk_rows:
        score(rs)
    if not sample:
        x1_ref[...] = x_ref[...] + _mm(o_scr[...].astype(BF16), wout_ref[...])
        for wide, narrow in zip(wide_refs, narrow_refs):
            narrow[...] = wide[...].astype(BF16)
        apply_states()

        @pl.when(step == pl.num_programs(1) - 1)
        def _():
            for h in range(N_HEADS):
                sa_out_ref[0, h] = sa_scr[h].T
                sb_out_ref[0, h] = sb_scr[h].T


def _full(shape):
    return pl.BlockSpec(shape, lambda *_: (0,) * len(shape))


def _row_block_spec(shape, n_steps, inner_steps):
    _, n_rows, width = shape
    share = next(k for k in (1, 2, 4, 8) if n_steps % k == 0 and n_rows % ((n_steps // k) * BF16_ROWS) == 0)
    return pl.BlockSpec((1, n_rows // (n_steps // share), width),
                        lambda b, i: (0, (b * inner_steps + i) // share, 0))


def _mix_scratch(tile):
    return [pltpu.VMEM((tile, D_MODEL), BF16), pltpu.VMEM((tile, IN_WIDTH), F32)]


def _smp_score_call(x2, weights, *, seq_len, tile):
    n_rows = x2.shape[0]
    assert n_rows % tile == 0 and tile % seq_len == 0
    nlev, cdec, consts = _chunk_consts(tile, seq_len)
    cos, sin = _rope_tables(np.tile(PAST_LEN + np.arange(seq_len), tile // seq_len))
    rows = lambda width: pl.BlockSpec((tile, width), lambda n: (n, 0))
    widths = [2 * GROUP_W] + [GROUP_W] * 5 + [len(KEPT_GROUPS) * GROUP_W]
    args = [x2, cos, sin, *weights, *consts]
    w_in, w_out = weights[1], weights[5]
    narrow_shapes = [w.shape[1:] for w in (w_in, w_out)]
    *outs, w_in_bf, w_out_bf = pl.pallas_call(
        functools.partial(_mix_kernel, sample=True, tile=tile, chunks=1, seq_rows=seq_len, nlev=nlev, cdec=cdec,
                          smp_cdec=None, smp_seq_per_step=0),
        grid=(n_rows // tile,),
        in_specs=[rows(D_MODEL)] + [_full(a.shape) for a in args[1:]],
        out_specs=[rows(w) for w in widths] + [_full(shape) for shape in narrow_shapes],
        out_shape=[jax.ShapeDtypeStruct((n_rows, w), F32) for w in widths]
        + [jax.ShapeDtypeStruct(shape, BF16) for shape in narrow_shapes],
        scratch_shapes=_mix_scratch(tile) + [pltpu.VMEM((tile, GROUP_W), F32)] * 2,
        compiler_params=pltpu.CompilerParams(dimension_semantics=("arbitrary",), vmem_limit_bytes=VMEM_LIMIT),
        name="score_sample",
    )(*args)
    return cdec, outs, w_in_bf, w_out_bf


def _mix_call(x2, weights, to_narrow, smp, smp_states, smp_cdec, *, n_seq, seq_len, tile, chunks, smp_len):
    n_rows = n_seq * seq_len
    assert seq_len % tile == 0 and tile % chunks == 0
    chunk = tile // chunks
    steps = seq_len // tile
    n_steps = n_seq * steps
    nlev, cdec, consts = _chunk_consts(chunk, min(seq_len, chunk))
    cos, sin = _rope_tables(np.arange(seq_len))
    n_smp = smp_states[0].shape[0]
    assert n_smp % n_steps == 0
    seq_per_step = n_smp // n_steps
    smp_rows = seq_per_step * smp_len
    assert smp_rows % SUBLANES == 0

    row_map = lambda b, i: (b * steps + i, 0)
    pos_map = lambda b, i: (i, 0)
    state_spec = pl.BlockSpec((1, N_HEADS, D_HEAD, D_HEAD), lambda b, i: (b, 0, 0, 0))
    smp_state_spec = pl.BlockSpec((seq_per_step, N_HEADS, D_HEAD, D_HEAD), lambda b, i: (b * steps + i, 0, 0, 0))
    smp_spec = lambda a: pl.BlockSpec((smp_rows, a.shape[1]), row_map)
    narrow_specs = [_row_block_spec(w.shape, n_steps, steps) for w in to_narrow]
    smp_scores, *smp_factors, smp_keep = smp
    side_in = [*smp_factors, smp_keep, smp_scores]
    in_specs = ([pl.BlockSpec((tile, D_MODEL), row_map), pl.BlockSpec((tile, D_HEAD), pos_map),
                 pl.BlockSpec((tile, D_HEAD), pos_map)] + [_full(w.shape) for w in weights]
                + [_full(c.shape) for c in consts] + narrow_specs + [smp_spec(a) for a in side_in]
                + [smp_state_spec, smp_state_spec])
    args = [x2, cos, sin, *weights, *consts, *to_narrow, *side_in, *smp_states]
    state_shape = lambda n: jax.ShapeDtypeStruct((n, N_HEADS, D_HEAD, D_HEAD), F32)
    return pl.pallas_call(
        functools.partial(_mix_kernel, sample=False, tile=tile, chunks=chunks, seq_rows=min(seq_len, chunk),
                          nlev=nlev, cdec=cdec, smp_cdec=smp_cdec, smp_seq_per_step=seq_per_step),
        grid=(n_seq, steps),
        in_specs=in_specs,
        out_specs=[pl.BlockSpec((tile, D_MODEL), row_map), state_spec, state_spec] + narrow_specs
        + [smp_spec(smp_scores), smp_state_spec, smp_state_spec],
        out_shape=[jax.ShapeDtypeStruct((n_rows, D_MODEL), F32), state_shape(n_seq), state_shape(n_seq)]
        + [jax.ShapeDtypeStruct(w.shape, BF16) for w in to_narrow]
        + [jax.ShapeDtypeStruct(smp_scores.shape, F32), state_shape(n_smp), state_shape(n_smp)],
        scratch_shapes=_mix_scratch(tile) + [pltpu.VMEM((tile, 2 * GROUP_W), F32)]
        + [pltpu.VMEM((tile, GROUP_W), F32)] * 2 + [pltpu.VMEM((N_HEADS, D_HEAD, D_HEAD), F32)] * 2,
        compiler_params=pltpu.CompilerParams(dimension_semantics=("arbitrary", "arbitrary"),
                                             vmem_limit_bytes=VMEM_LIMIT),
        name="mix_prompt",
    )(*args)


def _ffn_kernel(*refs, sample, groups, rows):
    x_ref, w2_ref, wup_ref, cw_ref, cb_ref, wdown_ref, wf_ref = refs[:7]
    refs = refs[7:]
    if sample:
        (hist_ref, os_ref, keep_ref, na_ref, nb_ref, wout_ref, y_ref, hist_out_ref, h_scr, act_scr, mixed_scr,
         xin) = refs
        for h in range(N_HEADS):
            for group, norm, w_ref in ((0, _rmsnorm, na_ref), (1, _groupnorm, nb_ref)):
                cols = slice(group * GROUP_W + h * D_HEAD, group * GROUP_W + (h + 1) * D_HEAD)
                slot = KEPT_GROUPS.index(4 * group + 3)
                gate = keep_ref[:, slot * GROUP_W + h * D_HEAD:slot * GROUP_W + (h + 1) * D_HEAD]
                mixed_scr[:, cols] = (norm(os_ref[:, cols], w_ref[...]) * (gate * _sigmoid(gate))).astype(BF16)
        xin[...] = x_ref[...] + _mm(mixed_scr[...], wout_ref[...])
    else:
        y_ref, hist_out_ref, h_scr, act_scr, tail_scr = refs
        xin = x_ref
        step = pl.program_id(1)

        @pl.when(step == 0)
        def _():
            tail_scr[...] = jnp.zeros_like(tail_scr)

    G, L, P = groups, rows, SUBLANES
    parts = 1 if sample else ROW_PARTS
    LP = L // parts
    row_id = lax.broadcasted_iota(jnp.int32, (G, P, FF_COLS), 1)

    def shifted(up, prev2, prev1):
        r1, r2 = pltpu.roll(up, 1, 1), pltpu.roll(up, 2, 1)
        top1 = jnp.where(row_id == 0, prev1, r1[:, :P])
        top2 = jnp.where(row_id == 0, prev2, jnp.where(row_id == 1, prev1, r2[:, :P]))
        if LP == P:
            return top1, top2
        return jnp.concatenate([top1, r1[:, P:]], axis=1), jnp.concatenate([top2, r2[:, P:]], axis=1)

    tails = {}
    for part in range(parts):
        rows_p = slice(part * G * LP, (part + 1) * G * LP)
        h_scr[rows_p, :] = _rmsnorm(xin[rows_p, :], w2_ref[...]).astype(BF16)
        for n in range(0, D_FF, FF_COLS):
            conv = []
            for cols in (slice(n, n + FF_COLS), slice(D_FF + n, D_FF + n + FF_COLS)):
                up = _mm(h_scr[rows_p, :], wup_ref[0, :, cols]).reshape(G, LP, FF_COLS)
                if sample:
                    prev2, prev1 = hist_ref[:, 0:1, cols], hist_ref[:, 1:2, cols]
                    hist_out_ref[:, :, cols] = up[:, LP - 2:, :]
                else:
                    if part == 0:
                        prev2, prev1 = tail_scr[:, P - 2:P - 1, cols], tail_scr[:, P - 1:P, cols]
                    else:
                        prev = tails[cols.start]
                        prev2, prev1 = prev[:, P - 2:P - 1, :], prev[:, P - 1:P, :]
                    tails[cols.start] = up[:, LP - P:, :]
                    if part == parts - 1:
                        tail_scr[:, :, cols] = up[:, LP - P:, :]
                sh1, sh2 = shifted(up, prev2, prev1)
                conv.append(cb_ref[:, cols] + cw_ref[0:1, cols] * sh2 + cw_ref[1:2, cols] * sh1 + cw_ref[2:3, cols] * up)
            u, g = (c.reshape(G * LP, FF_COLS).astype(BF16) for c in conv)
            one = jnp.ones((), BF16)
            act_scr[rows_p, n:n + FF_COLS] = (g * (one / (one + jnp.exp(-g)))) * u

        x2 = xin[rows_p, :] + _mm(act_scr[rows_p, :], wdown_ref[0])
        y_ref[rows_p, :] = _rmsnorm(x2, wf_ref[...])

    if not sample:
        @pl.when(step == pl.num_programs(1) - 1)
        def _():
            hist_out_ref[...] = tail_scr[:, P - 2:, :]


def _ffn_call(x, w2, w_up, cw, cb, w_down, wf, smp, *, n_seq, seq_len, tile):
    sample = smp is not None
    n_rows = n_seq * seq_len
    hist_shape = jax.ShapeDtypeStruct((n_seq, CONV_W - 1, FF2), F32)
    if sample:
        hist, scores, keep, na, nb, w_out = smp
        groups, rows = tile // seq_len, seq_len
        grid = (n_rows // tile,)
        row_map = lambda n: (n, 0)
        hist_spec = pl.BlockSpec((groups, CONV_W - 1, FF2), lambda n: (n, 0, 0))
        extra_in = [hist_spec, pl.BlockSpec((tile, scores.shape[1]), row_map),
                    pl.BlockSpec((tile, keep.shape[1]), row_map), _full(na.shape), _full(nb.shape), _full(w_out.shape)]
        extra_args = [hist, scores, keep, na, nb, w_out]
        scratch = [pltpu.VMEM((tile, 2 * GROUP_W), BF16), pltpu.VMEM((tile, D_MODEL), F32)]
    else:
        assert seq_len % tile == 0
        groups, rows = 1, tile
        steps = seq_len // tile
        grid = (n_seq, steps)
        row_map = lambda b, i: (b * steps + i, 0)
        hist_spec = pl.BlockSpec((1, CONV_W - 1, FF2), lambda b, i: (b, 0, 0))
        extra_in, extra_args = [], []
        scratch = [pltpu.VMEM((1, SUBLANES, FF2), F32)]
    in_specs = [pl.BlockSpec((tile, D_MODEL), row_map), _full(w2.shape), _full(w_up.shape), _full(cw.shape),
                _full(cb.shape), _full(w_down.shape), _full(wf.shape)] + extra_in
    args = [x, w2, w_up, cw, cb, w_down, wf] + extra_args
    return pl.pallas_call(
        functools.partial(_ffn_kernel, sample=sample, groups=groups, rows=rows),
        grid=grid,
        in_specs=in_specs,
        out_specs=[pl.BlockSpec((tile, D_MODEL), row_map), hist_spec],
        out_shape=[jax.ShapeDtypeStruct((n_rows, D_MODEL), F32), hist_shape],
        scratch_shapes=[pltpu.VMEM((tile, D_MODEL), BF16), pltpu.VMEM((tile, D_FF), BF16)] + scratch,
        compiler_params=pltpu.CompilerParams(dimension_semantics=("arbitrary",) * len(grid),
                                             vmem_limit_bytes=VMEM_LIMIT),
        name="ffn_sample" if sample else "ffn_prompt",
    )(*args)


def kernel(x_prompt, x_sample, state_hgrn, state_ret, state_conv, w_norm1, w_in, hgrn_lb, hgrn_norm_w, ret_norm_w,
           w_out, w_norm2, w_ffn_in, conv_w, conv_b, w_ffn_out, w_norm_f):
    assert w_in.shape == (1, D_MODEL, IN_WIDTH) and hgrn_lb.shape == (2, GROUP_W)
    n_seq, seq_len, _ = x_prompt.shape
    n_smp, smp_len, _ = x_sample.shape
    xs = x_sample.reshape(n_smp * smp_len, D_MODEL)

    smp_cdec, smp, w_in_bf, w_out_bf = _smp_score_call(
        xs, (w_norm1, w_in, hgrn_lb, hgrn_norm_w, ret_norm_w, w_out), seq_len=smp_len, tile=256)
    mix_w = (w_norm1, w_in_bf, hgrn_lb, hgrn_norm_w, ret_norm_w, w_out_bf)
    x1, ha_p, rb_p, w_up, w_down, smp_scores, ha_s, rb_s = _mix_call(
        x_prompt.reshape(n_seq * seq_len, D_MODEL), mix_w, (w_ffn_in, w_ffn_out), smp,
        (state_hgrn[0], state_ret[0]), smp_cdec, n_seq=n_seq, seq_len=seq_len, tile=512, chunks=2, smp_len=smp_len)
    ffn_w = (w_norm2, w_up, conv_w[0], conv_b, w_down, w_norm_f.reshape(1, D_MODEL))
    y_p, cv_p = _ffn_call(x1, *ffn_w, None, n_seq=n_seq, seq_len=seq_len, tile=512)
    y_s, cv_s = _ffn_call(xs, *ffn_w, (state_conv[0], smp_scores, smp[-1], hgrn_norm_w, ret_norm_w, w_out_bf),
                          n_seq=n_smp, seq_len=smp_len, tile=256)
    return (y_p.reshape(x_prompt.shape), y_s.reshape(x_sample.shape), ha_p[None], rb_p[None], cv_p[None],
            ha_s[None], rb_s[None], cv_s[None])
```

```python
import functools

import numpy as np
import jax
import jax.numpy as jnp
from jax import lax
from jax.experimental import pallas as pl
from jax.experimental.pallas import tpu as pltpu

F32 = jnp.float32
BF16 = jnp.bfloat16

D_MODEL = 1024
N_HEADS = 4
D_HEAD = 128
GROUP_W = N_HEADS * D_HEAD
IN_WIDTH = 8 * GROUP_W
D_FF = 2816
FF2 = 2 * D_FF
CONV_W = 3
PAST_LEN = 16384
ROPE_BASE = 10000.0
EPS = 1e-6
LOG2E = 1.4426950408889634

SUBLANES = 8
BF16_ROWS = 16
PROJ_COLS = 512
FF_COLS = 256
ROW_PARTS = 2
KEPT_GROUPS = (2, 3, 6, 7)
MIB = 1024 * 1024
VMEM_LIMITS = {"score_sample": 54 * MIB, "mix_prompt": 52 * MIB, "ffn_prompt": 32 * MIB, "ffn_sample": 44 * MIB}


def _mm(a, b):
    return jnp.dot(a, b, preferred_element_type=F32)


def _mm_nt(a, b):
    return lax.dot_general(a, b, (((1,), (1,)), ((), ())), preferred_element_type=F32)


def _mm_tn(a, b):
    return lax.dot_general(a, b, (((0,), (0,)), ((), ())), preferred_element_type=F32)


def _sigmoid(x):
    return 1.0 / (1.0 + jnp.exp(-x))


def _rmsnorm(x, w):
    return x * lax.rsqrt(jnp.mean(x * x, axis=-1, keepdims=True) + EPS) * w


def _groupnorm(x, w):
    xc = x - jnp.mean(x, axis=-1, keepdims=True)
    return xc * lax.rsqrt(jnp.mean(xc * xc, axis=-1, keepdims=True) + EPS) * w


def _chunk_consts(chunk, seq_len):
    nlev = int(np.log2(seq_len))
    assert 1 << nlev == seq_len and chunk % seq_len == 0
    r = np.arange(chunk)
    rr, cc = r[:, None], r[None, :]
    same_seq = (rr // seq_len) == (cc // seq_len)
    cum = (same_seq & (cc <= rr)).astype(np.float32)
    x = rr ^ cc
    bit_len = np.where(x > 0, np.floor(np.log2(np.maximum(x, 1))).astype(np.int64) + 1, 0)
    level = np.where(same_seq & (cc <= rr), bit_len, -1).astype(np.int32)

    pos = r % seq_len
    log_gamma = np.log1p(-np.exp2(-5.0 - np.arange(N_HEADS, dtype=np.float64)))[:, None, None]
    rel = (pos[:, None] - pos[None, :]).astype(np.float64)[None]
    causal = (same_seq & (cc <= rr))[None]
    dec = np.where(causal, np.exp(np.where(causal, rel, 0.0) * log_gamma), 0.0)
    ones = np.ones((1, 1, D_HEAD))
    inner = np.exp((pos + 1.0)[None, :, None] * log_gamma) * ones
    sdec = np.exp((seq_len - 1.0 - pos)[None, :, None] * log_gamma) * ones
    cdec = tuple(float(v) for v in np.exp(seq_len * log_gamma[:, 0, 0]))
    consts = (jnp.asarray(cum, BF16), jnp.asarray(level, BF16), jnp.asarray(dec, F32), jnp.asarray(inner, F32),
              jnp.asarray(sdec, F32))
    return nlev, cdec, consts


def _rope_tables(pos):
    half = D_HEAD // 2
    inv = 1.0 / (ROPE_BASE ** (np.arange(half, dtype=np.float64) / half))
    ang = np.asarray(pos, np.float64)[:, None] * inv[None, :]
    cos, sin = np.cos(ang), np.sin(ang)
    return (jnp.asarray(np.concatenate([cos, cos], axis=-1), F32),
            jnp.asarray(np.concatenate([-sin, sin], axis=-1), F32))


def _mix_kernel(*refs, sample, tile, chunks, seq_rows, nlev, cdec, smp_cdec, smp_seq_per_step):
    (x_ref, cos_ref, sin_ref, w1_ref, win_ref, lbp_ref, na_ref, nb_ref, wout_ref,
     cum_ref, level_ref, dec_ref, inner_ref, sdec_ref) = refs[:14]
    refs = refs[14:]
    if sample:
        o_scr, qe_scr, kh_scr, ex_scr, qi_scr, ks_scr, keep_ref, win_bf_ref, wout_bf_ref = refs[:9]
        h_scr, proj_scr, d_scr, k_scr = refs[9:]

        @pl.when(pl.program_id(0) == 0)
        def _():
            for n in range(0, IN_WIDTH, PROJ_COLS):
                win_bf_ref[:, n:n + PROJ_COLS] = win_ref[0, :, n:n + PROJ_COLS].astype(BF16)
            wout_bf_ref[...] = wout_ref[0].astype(BF16)

        win_ref = win_bf_ref
    else:
        wide_refs = refs[:2]
        qe_ref, kh_ref, ex_ref, qi_ref, ks_ref, keep_in_ref, os_in_ref, sa_in_ref, sb_in_ref = refs[2:11]
        x1_ref, sa_out_ref, sb_out_ref = refs[11:14]
        narrow_refs = refs[14:16]
        os_out_ref, sas_out_ref, sbs_out_ref = refs[16:19]
        h_scr, proj_scr, o_scr, d_scr, k_scr, sa_scr, sb_scr = refs[19:]
        step = pl.program_id(1)

        @pl.when(step == 0)
        def _():
            sa_scr[...] = jnp.zeros_like(sa_scr)
            sb_scr[...] = jnp.zeros_like(sb_scr)

    C = tile // chunks

    def col(group, h):
        return slice(group * GROUP_W + h * D_HEAD, group * GROUP_W + (h + 1) * D_HEAD)

    def head(h):
        return slice(h * D_HEAD, (h + 1) * D_HEAD)

    def block_rows(x, m, row):
        x3 = x.reshape(C // m, m, D_HEAD)
        return jnp.broadcast_to(x3[:, row:row + 1, :], x3.shape).reshape(C, D_HEAD)

    def upper_rows(x, m):
        return x.reshape(C // m, 2, m // 2, x.shape[-1])[:, 1].reshape(C // 2, x.shape[-1])

    def put_upper_rows(x, xu, m):
        x4 = x.reshape(C // m, 2, m // 2, x.shape[-1])
        xu4 = xu.reshape(C // m, 1, m // 2, x.shape[-1])
        return jnp.concatenate([x4[:, 0:1], xu4], axis=1).reshape(C, x.shape[-1])

    def project(rs):
        h_scr[rs, :] = _rmsnorm(x_ref[rs, :], w1_ref[...]).astype(BF16)
        for n in range(0, IN_WIDTH, PROJ_COLS):
            proj_scr[rs, n:n + PROJ_COLS] = _mm(h_scr[rs, :], win_ref[:, n:n + PROJ_COLS])

    def score(rs):
        lb0, lb1 = lbp_ref[0:1, :], lbp_ref[1:2, :]
        lb_max = jnp.maximum(lb0, lb1)
        e0, e1 = jnp.exp(lb0 - lb_max), jnp.exp(lb1 - lb_max)
        lb = e0 / (e0 + e1)

        row_id = lax.broadcasted_iota(jnp.int32, (C, D_HEAD), 0)

        f = lb + (1.0 - lb) * _sigmoid(proj_scr[rs, GROUP_W:2 * GROUP_W])
        k_scr[rs, :] = 1.0 - f
        g = jnp.log(f)
        g_hi = g.astype(BF16)
        g_lo = (g - g_hi.astype(F32)).astype(BF16)
        d_scr[rs, :] = _mm(cum_ref[...], g_hi) + _mm(cum_ref[...], g_lo)

        signs = [jnp.where((row_id & (1 << (lev - 1))) != 0, LOG2E, -LOG2E) for lev in range(3, nlev + 1)]

        for h in range(N_HEADS):
            hs = head(h)
            q = proj_scr[rs, col(0, h)]
            k = k_scr[rs, hs]
            v = proj_scr[rs, col(2, h)].astype(BF16)
            b = d_scr[rs, hs]
            a = jnp.where(level_ref[...] == 0, _mm_nt(q.astype(BF16), k.astype(BF16)).astype(BF16), 0.0)
            for lev in range(1, nlev + 1):
                m = 1 << lev
                upper = (row_id & (m // 2)) != 0
                if lev == 1:
                    z = jnp.where(upper, q * (1.0 - k), k)
                elif lev == 2:
                    fh = 1.0 - k
                    pos4 = row_id & 3
                    decay = jnp.where(pos4 == 0, pltpu.roll(fh, C - 1, 0),
                                      jnp.where(pos4 == 1, 1.0, jnp.where(pos4 == 2, fh, fh * pltpu.roll(fh, 1, 0))))
                    z = jnp.where(upper, q, k) * decay
                else:
                    z = jnp.where(upper, q, k) * jnp.exp2((b - block_rows(b, m, m // 2 - 1)) * signs[lev - 3])
                if m < 2 * BF16_ROWS:
                    z = z.astype(BF16)
                    a = jnp.where(level_ref[...] == lev, _mm_nt(z, z).astype(BF16), a)
                else:
                    zq = upper_rows(z, m).astype(BF16)
                    zk = z.astype(BF16)
                    if m > D_HEAD:
                        p = [_mm_nt(zq[i * (m // 2):(i + 1) * (m // 2)], zk[i * m:i * m + m // 2])
                             for i in range(C // m)]
                        width = m // 2
                    else:
                        p = [_mm_nt(zq[i * (D_HEAD // 2):(i + 1) * (D_HEAD // 2)], zk[i * D_HEAD:(i + 1) * D_HEAD])
                             for i in range(C // D_HEAD)]
                        width = D_HEAD
                    full = jnp.concatenate([jnp.concatenate([pi] * (C // width), axis=1) for pi in p], axis=0)
                    lvl_u = upper_rows(level_ref[...], m)
                    a = put_upper_rows(a, jnp.where(lvl_u == lev, full.astype(BF16), upper_rows(a, m)), m)
            o = _mm(a, v)
            eb = jnp.exp(b)
            qe = q * eb
            kh = k * jnp.exp(block_rows(b, seq_rows, seq_rows - 1) - b)
            if sample:
                o_scr[rs, col(0, h)] = o
                qe_scr[rs, hs] = qe
                kh_scr[rs, hs] = kh
                e_all = block_rows(eb, seq_rows, seq_rows - 1)
                e_hi = e_all.astype(BF16).astype(F32)
                e_mid = (e_all - e_hi).astype(BF16).astype(F32)
                e_lo = e_all - e_hi - e_mid
                pos = row_id & (seq_rows - 1)
                ex_scr[rs, hs] = jnp.where(pos == 0, e_hi, jnp.where(pos == 1, e_mid, jnp.where(pos == 2, e_lo, 0.0)))
            else:
                st = sa_scr[h]
                o = o + _mm_nt(qe.astype(BF16), st.astype(BF16))
                sa_scr[h] = st * eb[C - 1:C, :] + _mm_tn(v, kh.astype(BF16))
                gate = proj_scr[rs, col(3, h)]
                o_scr[rs, col(0, h)] = _rmsnorm(o, na_ref[...]) * (gate * _sigmoid(gate))

        cos, sin = cos_ref[rs, :], sin_ref[rs, :]
        for h in range(N_HEADS):
            hs = head(h)
            q = proj_scr[rs, col(4, h)]
            k = proj_scr[rs, col(5, h)]
            v = proj_scr[rs, col(6, h)].astype(BF16)
            qr = q * cos + pltpu.roll(q, D_HEAD // 2, 1) * sin
            kr = (k * cos + pltpu.roll(k, D_HEAD // 2, 1) * sin) * (D_HEAD ** -0.5)
            a = _mm_nt(qr.astype(BF16), kr.astype(BF16)) * dec_ref[h]
            o = _mm(a.astype(BF16), v)
            qi = qr * inner_ref[h]
            ks = kr * sdec_ref[h]
            if sample:
                o_scr[rs, col(1, h)] = o
                qi_scr[rs, hs] = qi
                ks_scr[rs, hs] = ks
            else:
                st = sb_scr[h]
                o = o + _mm(qi.astype(BF16), st.astype(BF16))
                sb_scr[h] = cdec[h] * st + _mm_tn(ks.astype(BF16), v)
                gate = proj_scr[rs, col(7, h)]
                o_scr[rs, col(1, h)] = _groupnorm(o, nb_ref[...]) * (gate * _sigmoid(gate))

        if sample:
            for slot, group in enumerate(KEPT_GROUPS):
                keep_ref[:, slot * GROUP_W:(slot + 1) * GROUP_W] = proj_scr[rs, group * GROUP_W:(group + 1) * GROUP_W]

    def kept(group, h):
        slot = KEPT_GROUPS.index(group)
        return slice(slot * GROUP_W + h * D_HEAD, slot * GROUP_W + (h + 1) * D_HEAD)

    def apply_states():
        zeros8 = jnp.zeros((SUBLANES, D_HEAD), F32)
        sel8 = jnp.where(lax.broadcasted_iota(jnp.int32, (SUBLANES, D_HEAD), 0) < 3, 1.0, 0.0)

        def pair_readout(lhs_ref, st_ref, j, rows, group, h):
            lhs = jnp.concatenate([lhs_ref[rows, head(h)], lhs_ref[rows, head(h + 1)]], axis=0).astype(BF16)
            w = jnp.concatenate([st_ref[j, h], st_ref[j, h + 1]], axis=1).astype(BF16)
            oo = _mm(lhs, w)
            os_out_ref[rows, col(group, h)] = os_in_ref[rows, col(group, h)] + oo[:SUBLANES, :D_HEAD]
            os_out_ref[rows, col(group, h + 1)] = os_in_ref[rows, col(group, h + 1)] + oo[SUBLANES:, D_HEAD:]

        for j in range(smp_seq_per_step):
            rows = slice(j * SUBLANES, (j + 1) * SUBLANES)
            for h in range(0, N_HEADS, 2):
                pair_readout(qe_ref, sa_in_ref, j, rows, 0, h)
                pair_readout(qi_ref, sb_in_ref, j, rows, 1, h)
            for h in range(N_HEADS):
                hs = head(h)
                lhs = jnp.concatenate([kh_ref[rows, hs], ex_ref[rows, hs]], axis=0).astype(BF16)
                v = keep_in_ref[rows, kept(2, h)]
                rhs = jnp.concatenate([jnp.concatenate([v, zeros8], axis=1),
                                       jnp.concatenate([zeros8, sel8], axis=1)], axis=0).astype(BF16)
                upd = _mm_tn(lhs, rhs)
                sas_out_ref[j, h] = sa_in_ref[j, h] * upd[:, D_HEAD:] + upd[:, :D_HEAD]
                v = keep_in_ref[rows, kept(6, h)].astype(BF16)
                sbs_out_ref[j, h] = smp_cdec[h] * sb_in_ref[j, h] + _mm_tn(ks_ref[rows, hs].astype(BF16), v)

    chunk_rows = [slice(c * C, (c + 1) * C) for c in range(chunks)]
    for rs in chunk_rows:
        project(rs)
    for rs in chunk_rows:
        score(rs)
    if not sample:
        x1_ref[...] = x_ref[...] + _mm(o_scr[...].astype(BF16), wout_ref[...])
        for wide, narrow in zip(wide_refs, narrow_refs):
            narrow[...] = wide[...].astype(BF16)
        apply_states()

        @pl.when(step == pl.num_programs(1) - 1)
        def _():
            for h in range(N_HEADS):
                sa_out_ref[0, h] = sa_scr[h].T
                sb_out_ref[0, h] = sb_scr[h]


def _full(shape):
    return pl.BlockSpec(shape, lambda *_: (0,) * len(shape))


def _row_block_spec(shape, n_steps, inner_steps):
    _, n_rows, width = shape
    share = next(k for k in (1, 2, 4, 8) if n_steps % k == 0 and n_rows % ((n_steps // k) * BF16_ROWS) == 0)
    return pl.BlockSpec((1, n_rows // (n_steps // share), width),
                        lambda b, i: (0, (b * inner_steps + i) // share, 0))


def _mix_scratch(tile):
    return [pltpu.VMEM((tile, D_MODEL), BF16), pltpu.VMEM((tile, IN_WIDTH), F32)]


def _smp_score_call(x2, weights, *, seq_len, tile):
    n_rows = x2.shape[0]
    assert n_rows % tile == 0 and tile % seq_len == 0
    nlev, cdec, consts = _chunk_consts(tile, seq_len)
    cos, sin = _rope_tables(np.tile(PAST_LEN + np.arange(seq_len), tile // seq_len))
    rows = lambda width: pl.BlockSpec((tile, width), lambda n: (n, 0))
    widths = [2 * GROUP_W] + [GROUP_W] * 5 + [len(KEPT_GROUPS) * GROUP_W]
    args = [x2, cos, sin, *weights, *consts]
    w_in, w_out = weights[1], weights[5]
    narrow_shapes = [w.shape[1:] for w in (w_in, w_out)]
    *outs, w_in_bf, w_out_bf = pl.pallas_call(
        functools.partial(_mix_kernel, sample=True, tile=tile, chunks=1, seq_rows=seq_len, nlev=nlev, cdec=cdec,
                          smp_cdec=None, smp_seq_per_step=0),
        grid=(n_rows // tile,),
        in_specs=[rows(D_MODEL)] + [_full(a.shape) for a in args[1:]],
        out_specs=[rows(w) for w in widths] + [_full(shape) for shape in narrow_shapes],
        out_shape=[jax.ShapeDtypeStruct((n_rows, w), F32) for w in widths]
        + [jax.ShapeDtypeStruct(shape, BF16) for shape in narrow_shapes],
        scratch_shapes=_mix_scratch(tile) + [pltpu.VMEM((tile, GROUP_W), F32)] * 2,
        compiler_params=pltpu.CompilerParams(dimension_semantics=("arbitrary",),
                                             vmem_limit_bytes=VMEM_LIMITS["score_sample"]),
        name="score_sample",
    )(*args)
    return cdec, outs, w_in_bf, w_out_bf


def _mix_call(x2, weights, to_narrow, smp, smp_states, smp_cdec, *, n_seq, seq_len, tile, chunks, smp_len):
    n_rows = n_seq * seq_len
    assert seq_len % tile == 0 and tile % chunks == 0
    chunk = tile // chunks
    steps = seq_len // tile
    n_steps = n_seq * steps
    nlev, cdec, consts = _chunk_consts(chunk, min(seq_len, chunk))
    cos, sin = _rope_tables(np.arange(seq_len))
    n_smp = smp_states[0].shape[0]
    assert n_smp % n_steps == 0
    seq_per_step = n_smp // n_steps
    smp_rows = seq_per_step * smp_len
    assert smp_rows % SUBLANES == 0

    row_map = lambda b, i: (b * steps + i, 0)
    pos_map = lambda b, i: (i, 0)
    state_spec = pl.BlockSpec((1, N_HEADS, D_HEAD, D_HEAD), lambda b, i: (b, 0, 0, 0))
    smp_state_spec = pl.BlockSpec((seq_per_step, N_HEADS, D_HEAD, D_HEAD), lambda b, i: (b * steps + i, 0, 0, 0))
    smp_spec = lambda a: pl.BlockSpec((smp_rows, a.shape[1]), row_map)
    narrow_specs = [_row_block_spec(w.shape, n_steps, steps) for w in to_narrow]
    smp_scores, *smp_factors, smp_keep = smp
    side_in = [*smp_factors, smp_keep, smp_scores]
    in_specs = ([pl.BlockSpec((tile, D_MODEL), row_map), pl.BlockSpec((tile, D_HEAD), pos_map),
                 pl.BlockSpec((tile, D_HEAD), pos_map)] + [_full(w.shape) for w in weights]
                + [_full(c.shape) for c in consts] + narrow_specs + [smp_spec(a) for a in side_in]
                + [smp_state_spec, smp_state_spec])
    args = [x2, cos, sin, *weights, *consts, *to_narrow, *side_in, *smp_states]
    state_shape = lambda n: jax.ShapeDtypeStruct((n, N_HEADS, D_HEAD, D_HEAD), F32)
    return pl.pallas_call(
        functools.partial(_mix_kernel, sample=False, tile=tile, chunks=chunks, seq_rows=min(seq_len, chunk),
                          nlev=nlev, cdec=cdec, smp_cdec=smp_cdec, smp_seq_per_step=seq_per_step),
        grid=(n_seq, steps),
        in_specs=in_specs,
        out_specs=[pl.BlockSpec((tile, D_MODEL), row_map), state_spec, state_spec] + narrow_specs
        + [smp_spec(smp_scores), smp_state_spec, smp_state_spec],
        out_shape=[jax.ShapeDtypeStruct((n_rows, D_MODEL), F32), state_shape(n_seq), state_shape(n_seq)]
        + [jax.ShapeDtypeStruct(w.shape, BF16) for w in to_narrow]
        + [jax.ShapeDtypeStruct(smp_scores.shape, F32), state_shape(n_smp), state_shape(n_smp)],
        scratch_shapes=_mix_scratch(tile) + [pltpu.VMEM((tile, 2 * GROUP_W), F32)]
        + [pltpu.VMEM((tile, GROUP_W), F32)] * 2 + [pltpu.VMEM((N_HEADS, D_HEAD, D_HEAD), F32)] * 2,
        compiler_params=pltpu.CompilerParams(dimension_semantics=("arbitrary", "arbitrary"),
                                             vmem_limit_bytes=VMEM_LIMITS["mix_prompt"]),
        name="mix_prompt",
    )(*args)


def _ffn_kernel(*refs, sample, groups, rows):
    x_ref, w2_ref, wup_ref, cw_ref, cb_ref, wdown_ref, wf_ref = refs[:7]
    refs = refs[7:]
    if sample:
        (hist_ref, os_ref, keep_ref, na_ref, nb_ref, wout_ref, y_ref, hist_out_ref, h_scr, act_scr, mixed_scr,
         xin) = refs
        for h in range(N_HEADS):
            for group, norm, w_ref in ((0, _rmsnorm, na_ref), (1, _groupnorm, nb_ref)):
                cols = slice(group * GROUP_W + h * D_HEAD, group * GROUP_W + (h + 1) * D_HEAD)
                slot = KEPT_GROUPS.index(4 * group + 3)
                gate = keep_ref[:, slot * GROUP_W + h * D_HEAD:slot * GROUP_W + (h + 1) * D_HEAD]
                mixed_scr[:, cols] = (norm(os_ref[:, cols], w_ref[...]) * (gate * _sigmoid(gate))).astype(BF16)
        xin[...] = x_ref[...] + _mm(mixed_scr[...], wout_ref[...])
    else:
        y_ref, hist_out_ref, h_scr, act_scr, tail_scr = refs
        xin = x_ref
        step = pl.program_id(1)

        @pl.when(step == 0)
        def _():
            tail_scr[...] = jnp.zeros_like(tail_scr)

    G, L, P = groups, rows, SUBLANES
    parts = 1 if sample else ROW_PARTS
    LP = L // parts
    row_id = lax.broadcasted_iota(jnp.int32, (G, P, FF_COLS), 1)

    def shifted(up, prev2, prev1):
        r1, r2 = pltpu.roll(up, 1, 1), pltpu.roll(up, 2, 1)
        top1 = jnp.where(row_id == 0, prev1, r1[:, :P])
        top2 = jnp.where(row_id == 0, prev2, jnp.where(row_id == 1, prev1, r2[:, :P]))
        if LP == P:
            return top1, top2
        return jnp.concatenate([top1, r1[:, P:]], axis=1), jnp.concatenate([top2, r2[:, P:]], axis=1)

    tails = {}
    for part in range(parts):
        rows_p = slice(part * G * LP, (part + 1) * G * LP)
        h_scr[rows_p, :] = _rmsnorm(xin[rows_p, :], w2_ref[...]).astype(BF16)
        for n in range(0, D_FF, FF_COLS):
            conv = []
            for cols in (slice(n, n + FF_COLS), slice(D_FF + n, D_FF + n + FF_COLS)):
                up = _mm(h_scr[rows_p, :], wup_ref[0, :, cols]).reshape(G, LP, FF_COLS)
                if sample:
                    prev2, prev1 = hist_ref[:, 0:1, cols], hist_ref[:, 1:2, cols]
                    hist_out_ref[:, :, cols] = up[:, LP - 2:, :]
                else:
                    if part == 0:
                        prev2, prev1 = tail_scr[:, P - 2:P - 1, cols], tail_scr[:, P - 1:P, cols]
                    else:
                        prev = tails[cols.start]
                        prev2, prev1 = prev[:, P - 2:P - 1, :], prev[:, P - 1:P, :]
                    tails[cols.start] = up[:, LP - P:, :]
                    if part == parts - 1:
                        tail_scr[:, :, cols] = up[:, LP - P:, :]
                sh1, sh2 = shifted(up, prev2, prev1)
                conv.append(cb_ref[:, cols] + cw_ref[0:1, cols] * sh2 + cw_ref[1:2, cols] * sh1 + cw_ref[2:3, cols] * up)
            u, g = (c.reshape(G * LP, FF_COLS).astype(BF16) for c in conv)
            one = jnp.ones((), BF16)
            act_scr[rows_p, n:n + FF_COLS] = (g * (one / (one + jnp.exp(-g)))) * u

        x2 = xin[rows_p, :] + _mm(act_scr[rows_p, :], wdown_ref[0])
        y_ref[rows_p, :] = _rmsnorm(x2, wf_ref[...])

    if not sample:
        @pl.when(step == pl.num_programs(1) - 1)
        def _():
            hist_out_ref[...] = tail_scr[:, P - 2:, :]


def _ffn_call(x, w2, w_up, cw, cb, w_down, wf, smp, *, n_seq, seq_len, tile):
    sample = smp is not None
    n_rows = n_seq * seq_len
    hist_shape = jax.ShapeDtypeStruct((n_seq, CONV_W - 1, FF2), F32)
    if sample:
        hist, scores, keep, na, nb, w_out = smp
        groups, rows = tile // seq_len, seq_len
        grid = (n_rows // tile,)
        row_map = lambda n: (n, 0)
        hist_spec = pl.BlockSpec((groups, CONV_W - 1, FF2), lambda n: (n, 0, 0))
        extra_in = [hist_spec, pl.BlockSpec((tile, scores.shape[1]), row_map),
                    pl.BlockSpec((tile, keep.shape[1]), row_map), _full(na.shape), _full(nb.shape), _full(w_out.shape)]
        extra_args = [hist, scores, keep, na, nb, w_out]
        scratch = [pltpu.VMEM((tile, 2 * GROUP_W), BF16), pltpu.VMEM((tile, D_MODEL), F32)]
    else:
        assert seq_len % tile == 0
        groups, rows = 1, tile
        steps = seq_len // tile
        grid = (n_seq, steps)
        row_map = lambda b, i: (b * steps + i, 0)
        hist_spec = pl.BlockSpec((1, CONV_W - 1, FF2), lambda b, i: (b, 0, 0))
        extra_in, extra_args = [], []
        scratch = [pltpu.VMEM((1, SUBLANES, FF2), F32)]
    in_specs = [pl.BlockSpec((tile, D_MODEL), row_map), _full(w2.shape), _full(w_up.shape), _full(cw.shape),
                _full(cb.shape), _full(w_down.shape), _full(wf.shape)] + extra_in
    args = [x, w2, w_up, cw, cb, w_down, wf] + extra_args
    name = "ffn_sample" if sample else "ffn_prompt"
    return pl.pallas_call(
        functools.partial(_ffn_kernel, sample=sample, groups=groups, rows=rows),
        grid=grid,
        in_specs=in_specs,
        out_specs=[pl.BlockSpec((tile, D_MODEL), row_map), hist_spec],
        out_shape=[jax.ShapeDtypeStruct((n_rows, D_MODEL), F32), hist_shape],
        scratch_shapes=[pltpu.VMEM((tile, D_MODEL), BF16), pltpu.VMEM((tile, D_FF), BF16)] + scratch,
        compiler_params=pltpu.CompilerParams(dimension_semantics=("arbitrary",) * len(grid),
                                             vmem_limit_bytes=VMEM_LIMITS[name]),
        name=name,
    )(*args)


def kernel(x_prompt, x_sample, state_hgrn, state_ret, state_conv, w_norm1, w_in, hgrn_lb, hgrn_norm_w, ret_norm_w,
           w_out, w_norm2, w_ffn_in, conv_w, conv_b, w_ffn_out, w_norm_f):
    assert w_in.shape == (1, D_MODEL, IN_WIDTH) and hgrn_lb.shape == (2, GROUP_W)
    n_seq, seq_len, _ = x_prompt.shape
    n_smp, smp_len, _ = x_sample.shape
    xs = x_sample.reshape(n_smp * smp_len, D_MODEL)

    smp_cdec, smp, w_in_bf, w_out_bf = _smp_score_call(
        xs, (w_norm1, w_in, hgrn_lb, hgrn_norm_w, ret_norm_w, w_out), seq_len=smp_len, tile=256)
    mix_w = (w_norm1, w_in_bf, hgrn_lb, hgrn_norm_w, ret_norm_w, w_out_bf)
    x1, ha_p, rb_p, w_up, w_down, smp_scores, ha_s, rb_s = _mix_call(
        x_prompt.reshape(n_seq * seq_len, D_MODEL), mix_w, (w_ffn_in, w_ffn_out), smp,
        (state_hgrn[0], state_ret[0]), smp_cdec, n_seq=n_seq, seq_len=seq_len, tile=512, chunks=2, smp_len=smp_len)
    ffn_w = (w_norm2, w_up, conv_w[0], conv_b, w_down, w_norm_f.reshape(1, D_MODEL))
    y_p, cv_p = _ffn_call(x1, *ffn_w, None, n_seq=n_seq, seq_len=seq_len, tile=512)
    y_s, cv_s = _ffn_call(xs, *ffn_w, (state_conv[0], smp_scores, smp[-1], hgrn_norm_w, ret_norm_w, w_out_bf),
                          n_seq=n_smp, seq_len=smp_len, tile=256)
    return (y_p.reshape(x_prompt.shape), y_s.reshape(x_sample.shape), ha_p[None], rb_p[None], cv_p[None],
            ha_s[None], rb_s[None], cv_s[None])
```

```python
import functools

import numpy as np
import jax
import jax.numpy as jnp
from jax import lax
from jax.experimental import pallas as pl
from jax.experimental.pallas import tpu as pltpu

F32 = jnp.float32
BF16 = jnp.bfloat16

D_MODEL = 1024
N_HEADS = 4
D_HEAD = 128
GROUP_W = N_HEADS * D_HEAD
IN_WIDTH = 8 * GROUP_W
D_FF = 2816
FF2 = 2 * D_FF
CONV_W = 3
PAST_LEN = 16384
ROPE_BASE = 10000.0
EPS = 1e-6
LOG2E = 1.4426950408889634

SUBLANES = 8
BF16_ROWS = 16
PROJ_COLS = 512
FF_COLS = 256
ROW_PARTS = 2
KEPT_GROUPS = (2, 3, 6, 7)
MIB = 1024 * 1024
VMEM_LIMITS = {"score_sample": 54 * MIB, "mix_prompt": 52 * MIB, "ffn_prompt": 32 * MIB, "ffn_sample": 44 * MIB}


def _mm(a, b):
    return jnp.dot(a, b, preferred_element_type=F32)


def _mm_nt(a, b):
    return lax.dot_general(a, b, (((1,), (1,)), ((), ())), preferred_element_type=F32)


def _mm_tn(a, b):
    return lax.dot_general(a, b, (((0,), (0,)), ((), ())), preferred_element_type=F32)


def _sigmoid(x):
    return 1.0 / (1.0 + jnp.exp(-x))


def _rmsnorm(x, w):
    return x * lax.rsqrt(jnp.mean(x * x, axis=-1, keepdims=True) + EPS) * w


def _groupnorm(x, w):
    xc = x - jnp.mean(x, axis=-1, keepdims=True)
    return xc * lax.rsqrt(jnp.mean(xc * xc, axis=-1, keepdims=True) + EPS) * w


def _chunk_consts(chunk, seq_len):
    nlev = int(np.log2(seq_len))
    assert 1 << nlev == seq_len and chunk % seq_len == 0
    r = np.arange(chunk)
    rr, cc = r[:, None], r[None, :]
    same_seq = (rr // seq_len) == (cc // seq_len)
    cum = (same_seq & (cc <= rr)).astype(np.float32)
    x = rr ^ cc
    bit_len = np.where(x > 0, np.floor(np.log2(np.maximum(x, 1))).astype(np.int64) + 1, 0)
    level = np.where(same_seq & (cc <= rr), bit_len, -1).astype(np.int32)

    pos = r % seq_len
    log_gamma = np.log1p(-np.exp2(-5.0 - np.arange(N_HEADS, dtype=np.float64)))[:, None, None]
    rel = (pos[:, None] - pos[None, :]).astype(np.float64)[None]
    causal = (same_seq & (cc <= rr))[None]
    dec = np.where(causal, np.exp(np.where(causal, rel, 0.0) * log_gamma), 0.0)
    ones = np.ones((1, 1, D_HEAD))
    inner = np.exp((pos + 1.0)[None, :, None] * log_gamma) * ones
    sdec = np.exp((seq_len - 1.0 - pos)[None, :, None] * log_gamma) * ones
    cdec = tuple(float(v) for v in np.exp(seq_len * log_gamma[:, 0, 0]))
    consts = (jnp.asarray(cum, BF16), jnp.asarray(level, BF16), jnp.asarray(dec, F32), jnp.asarray(inner, F32),
              jnp.asarray(sdec, F32))
    return nlev, cdec, consts


def _rope_tables(pos):
    half = D_HEAD // 2
    inv = 1.0 / (ROPE_BASE ** (np.arange(half, dtype=np.float64) / half))
    ang = np.asarray(pos, np.float64)[:, None] * inv[None, :]
    cos, sin = np.cos(ang), np.sin(ang)
    return (jnp.asarray(np.concatenate([cos, cos], axis=-1), F32),
            jnp.asarray(np.concatenate([-sin, sin], axis=-1), F32))


def _mix_kernel(*refs, sample, tile, chunks, seq_rows, nlev, cdec, smp_cdec, smp_seq_per_step):
    (x_ref, cos_ref, sin_ref, w1_ref, win_ref, lbp_ref, na_ref, nb_ref, wout_ref,
     cum_ref, level_ref, dec_ref, inner_ref, sdec_ref) = refs[:14]
    refs = refs[14:]
    if sample:
        o_scr, qe_scr, kh_scr, ex_scr, qi_scr, ks_scr, keep_ref, win_bf_ref, wout_bf_ref = refs[:9]
        h_scr, proj_scr, d_scr, k_scr = refs[9:]

        @pl.when(pl.program_id(0) == 0)
        def _():
            for n in range(0, IN_WIDTH, PROJ_COLS):
                win_bf_ref[:, n:n + PROJ_COLS] = win_ref[0, :, n:n + PROJ_COLS].astype(BF16)
            wout_bf_ref[...] = wout_ref[0].astype(BF16)

        win_ref = win_bf_ref
    else:
        wide_refs = refs[:2]
        qe_ref, kh_ref, ex_ref, qi_ref, ks_ref, keep_in_ref, os_in_ref, sa_in_ref, sb_in_ref = refs[2:11]
        x1_ref, sa_out_ref, sb_out_ref = refs[11:14]
        narrow_refs = refs[14:16]
        os_out_ref, sas_out_ref, sbs_out_ref = refs[16:19]
        h_scr, proj_scr, o_scr, d_scr, k_scr, sa_scr, sb_scr = refs[19:]
        step = pl.program_id(1)

        @pl.when(step == 0)
        def _():
            sa_scr[...] = jnp.zeros_like(sa_scr)
            sb_scr[...] = jnp.zeros_like(sb_scr)

    C = tile // chunks

    def col(group, h):
        return slice(group * GROUP_W + h * D_HEAD, group * GROUP_W + (h + 1) * D_HEAD)

    def head(h):
        return slice(h * D_HEAD, (h + 1) * D_HEAD)

    def block_rows(x, m, row):
        x3 = x.reshape(C // m, m, D_HEAD)
        return jnp.broadcast_to(x3[:, row:row + 1, :], x3.shape).reshape(C, D_HEAD)

    def upper_rows(x, m):
        return x.reshape(C // m, 2, m // 2, x.shape[-1])[:, 1].reshape(C // 2, x.shape[-1])

    def put_upper_rows(x, xu, m):
        x4 = x.reshape(C // m, 2, m // 2, x.shape[-1])
        xu4 = xu.reshape(C // m, 1, m // 2, x.shape[-1])
        return jnp.concatenate([x4[:, 0:1], xu4], axis=1).reshape(C, x.shape[-1])

    def project(rs):
        h_scr[rs, :] = _rmsnorm(x_ref[rs, :], w1_ref[...]).astype(BF16)
        for n in range(0, IN_WIDTH, PROJ_COLS):
            proj_scr[rs, n:n + PROJ_COLS] = _mm(h_scr[rs, :], win_ref[:, n:n + PROJ_COLS])

    def score(rs):
        lb0, lb1 = lbp_ref[0:1, :], lbp_ref[1:2, :]
        lb_max = jnp.maximum(lb0, lb1)
        e0, e1 = jnp.exp(lb0 - lb_max), jnp.exp(lb1 - lb_max)
        lb = e0 / (e0 + e1)

        row_id = lax.broadcasted_iota(jnp.int32, (C, D_HEAD), 0)

        f = lb + (1.0 - lb) * _sigmoid(proj_scr[rs, GROUP_W:2 * GROUP_W])
        k_scr[rs, :] = 1.0 - f
        g = jnp.log(f)
        g_hi = g.astype(BF16)
        g_lo = (g - g_hi.astype(F32)).astype(BF16)
        d_scr[rs, :] = _mm(cum_ref[...], g_hi) + _mm(cum_ref[...], g_lo)

        signs = [jnp.where((row_id & (1 << (lev - 1))) != 0, LOG2E, -LOG2E) for lev in range(3, nlev + 1)]

        for h in range(N_HEADS):
            hs = head(h)
            q = proj_scr[rs, col(0, h)]
            k = k_scr[rs, hs]
            v = proj_scr[rs, col(2, h)].astype(BF16)
            b = d_scr[rs, hs]
            a = jnp.where(level_ref[...] == 0, _mm_nt(q.astype(BF16), k.astype(BF16)).astype(BF16), 0.0)
            for lev in range(1, nlev + 1):
                m = 1 << lev
                upper = (row_id & (m // 2)) != 0
                if lev == 1:
                    z = jnp.where(upper, q * (1.0 - k), k)
                elif lev == 2:
                    fh = 1.0 - k
                    pos4 = row_id & 3
                    decay = jnp.where(pos4 == 0, pltpu.roll(fh, C - 1, 0),
                                      jnp.where(pos4 == 1, 1.0, jnp.where(pos4 == 2, fh, fh * pltpu.roll(fh, 1, 0))))
                    z = jnp.where(upper, q, k) * decay
                else:
                    z = jnp.where(upper, q, k) * jnp.exp2((b - block_rows(b, m, m // 2 - 1)) * signs[lev - 3])
                if m < 2 * BF16_ROWS:
                    z = z.astype(BF16)
                    a = jnp.where(level_ref[...] == lev, _mm_nt(z, z).astype(BF16), a)
                else:
                    zq = upper_rows(z, m).astype(BF16)
                    zk = z.astype(BF16)
                    if m > D_HEAD:
                        p = [_mm_nt(zq[i * (m // 2):(i + 1) * (m // 2)], zk[i * m:i * m + m // 2])
                             for i in range(C // m)]
                        width = m // 2
                    else:
                        p = [_mm_nt(zq[i * (D_HEAD // 2):(i + 1) * (D_HEAD // 2)], zk[i * D_HEAD:(i + 1) * D_HEAD])
                             for i in range(C // D_HEAD)]
                        width = D_HEAD
                    full = jnp.concatenate([jnp.concatenate([pi] * (C // width), axis=1) for pi in p], axis=0)
                    lvl_u = upper_rows(level_ref[...], m)
                    a = put_upper_rows(a, jnp.where(lvl_u == lev, full.astype(BF16), upper_rows(a, m)), m)
            o = _mm(a, v)
            eb = jnp.exp(b)
            qe = q * eb
            kh = k * jnp.exp(block_rows(b, seq_rows, seq_rows - 1) - b)
            if sample:
                o_scr[rs, col(0, h)] = o
                qe_scr[rs, hs] = qe
                kh_scr[rs, hs] = kh
                e_all = block_rows(eb, seq_rows, seq_rows - 1)
                e_hi = e_all.astype(BF16).astype(F32)
                e_mid = (e_all - e_hi).astype(BF16).astype(F32)
                e_lo = e_all - e_hi - e_mid
                pos = row_id & (seq_rows - 1)
                ex_scr[rs, hs] = jnp.where(pos == 0, e_hi, jnp.where(pos == 1, e_mid, jnp.where(pos == 2, e_lo, 0.0)))
            else:
                st = sa_scr[h]
                o = o + _mm_nt(qe.astype(BF16), st.astype(BF16))
                sa_scr[h] = st * eb[C - 1:C, :] + _mm_tn(v, kh.astype(BF16))
                gate = proj_scr[rs, col(3, h)]
                o_scr[rs, col(0, h)] = _rmsnorm(o, na_ref[...]) * (gate * _sigmoid(gate))

        cos, sin = cos_ref[rs, :], sin_ref[rs, :]
        for h in range(N_HEADS):
            hs = head(h)
            q = proj_scr[rs, col(4, h)]
            k = proj_scr[rs, col(5, h)]
            v = proj_scr[rs, col(6, h)].astype(BF16)
            qr = q * cos + pltpu.roll(q, D_HEAD // 2, 1) * sin
            kr = (k * cos + pltpu.roll(k, D_HEAD // 2, 1) * sin) * (D_HEAD ** -0.5)
            a = _mm_nt(qr.astype(BF16), kr.astype(BF16)) * dec_ref[h]
            o = _mm(a.astype(BF16), v)
            qi = qr * inner_ref[h]
            ks = kr * sdec_ref[h]
            if sample:
                o_scr[rs, col(1, h)] = o
                qi_scr[rs, hs] = qi
                ks_scr[rs, hs] = ks
            else:
                st = sb_scr[h]
                o = o + _mm(qi.astype(BF16), st.astype(BF16))
                sb_scr[h] = cdec[h] * st + _mm_tn(ks.astype(BF16), v)
                gate = proj_scr[rs, col(7, h)]
                o_scr[rs, col(1, h)] = _groupnorm(o, nb_ref[...]) * (gate * _sigmoid(gate))

        if sample:
            for slot, group in enumerate(KEPT_GROUPS):
                keep_ref[:, slot * GROUP_W:(slot + 1) * GROUP_W] = proj_scr[rs, group * GROUP_W:(group + 1) * GROUP_W]

    def kept(group, h):
        slot = KEPT_GROUPS.index(group)
        return slice(slot * GROUP_W + h * D_HEAD, slot * GROUP_W + (h + 1) * D_HEAD)

    def apply_states():
        zeros8 = jnp.zeros((SUBLANES, D_HEAD), F32)
        sel8 = jnp.where(lax.broadcasted_iota(jnp.int32, (SUBLANES, D_HEAD), 0) < 3, 1.0, 0.0)

        def pair_readout(lhs_ref, st_ref, j, rows, group, h):
            lhs = jnp.concatenate([lhs_ref[rows, head(h)], lhs_ref[rows, head(h + 1)]], axis=0).astype(BF16)
            w = jnp.concatenate([st_ref[j, h], st_ref[j, h + 1]], axis=1).astype(BF16)
            oo = _mm(lhs, w)
            os_out_ref[rows, col(group, h)] = os_in_ref[rows, col(group, h)] + oo[:SUBLANES, :D_HEAD]
            os_out_ref[rows, col(group, h + 1)] = os_in_ref[rows, col(group, h + 1)] + oo[SUBLANES:, D_HEAD:]

        for j in range(smp_seq_per_step):
            rows = slice(j * SUBLANES, (j + 1) * SUBLANES)
            for h in range(0, N_HEADS, 2):
                pair_readout(qe_ref, sa_in_ref, j, rows, 0, h)
                pair_readout(qi_ref, sb_in_ref, j, rows, 1, h)
            for h in range(N_HEADS):
                hs = head(h)
                lhs = jnp.concatenate([kh_ref[rows, hs], ex_ref[rows, hs]], axis=0).astype(BF16)
                v = keep_in_ref[rows, kept(2, h)]
                rhs = jnp.concatenate([jnp.concatenate([v, zeros8], axis=1),
                                       jnp.concatenate([zeros8, sel8], axis=1)], axis=0).astype(BF16)
                upd = _mm_tn(lhs, rhs)
                sas_out_ref[j, h] = sa_in_ref[j, h] * upd[:, D_HEAD:] + upd[:, :D_HEAD]
                v = keep_in_ref[rows, kept(6, h)].astype(BF16)
                sbs_out_ref[j, h] = smp_cdec[h] * sb_in_ref[j, h] + _mm_tn(ks_ref[rows, hs].astype(BF16), v)

    chunk_rows = [slice(c * C, (c + 1) * C) for c in range(chunks)]
    if not sample:
        apply_states()
        for wide, narrow in zip(wide_refs, narrow_refs):
            narrow[...] = wide[...].astype(BF16)
    for rs in chunk_rows:
        project(rs)
    for rs in chunk_rows:
        score(rs)
    if not sample:
        x1_ref[...] = x_ref[...] + _mm(o_scr[...].astype(BF16), wout_ref[...])

        @pl.when(step == pl.num_programs(1) - 1)
        def _():
            for h in range(N_HEADS):
                sa_out_ref[0, h] = sa_scr[h].T
                sb_out_ref[0, h] = sb_scr[h]


def _full(shape):
    return pl.BlockSpec(shape, lambda *_: (0,) * len(shape))


def _row_block_spec(shape, n_steps, inner_steps):
    _, n_rows, width = shape
    share = next(k for k in (1, 2, 4, 8) if n_steps % k == 0 and n_rows % ((n_steps // k) * BF16_ROWS) == 0)
    return pl.BlockSpec((1, n_rows // (n_steps // share), width),
                        lambda b, i: (0, (b * inner_steps + i) // share, 0))


def _mix_scratch(tile):
    return [pltpu.VMEM((tile, D_MODEL), BF16), pltpu.VMEM((tile, IN_WIDTH), F32)]


def _smp_score_call(x2, weights, *, seq_len, tile):
    n_rows = x2.shape[0]
    assert n_rows % tile == 0 and tile % seq_len == 0
    nlev, cdec, consts = _chunk_consts(tile, seq_len)
    cos, sin = _rope_tables(np.tile(PAST_LEN + np.arange(seq_len), tile // seq_len))
    rows = lambda width: pl.BlockSpec((tile, width), lambda n: (n, 0))
    widths = [2 * GROUP_W] + [GROUP_W] * 5 + [len(KEPT_GROUPS) * GROUP_W]
    args = [x2, cos, sin, *weights, *consts]
    w_in, w_out = weights[1], weights[5]
    narrow_shapes = [w.shape[1:] for w in (w_in, w_out)]
    *outs, w_in_bf, w_out_bf = pl.pallas_call(
        functools.partial(_mix_kernel, sample=True, tile=tile, chunks=1, seq_rows=seq_len, nlev=nlev, cdec=cdec,
                          smp_cdec=None, smp_seq_per_step=0),
        grid=(n_rows // tile,),
        in_specs=[rows(D_MODEL)] + [_full(a.shape) for a in args[1:]],
        out_specs=[rows(w) for w in widths] + [_full(shape) for shape in narrow_shapes],
        out_shape=[jax.ShapeDtypeStruct((n_rows, w), F32) for w in widths]
        + [jax.ShapeDtypeStruct(shape, BF16) for shape in narrow_shapes],
        scratch_shapes=_mix_scratch(tile) + [pltpu.VMEM((tile, GROUP_W), F32)] * 2,
        compiler_params=pltpu.CompilerParams(dimension_semantics=("arbitrary",),
                                             vmem_limit_bytes=VMEM_LIMITS["score_sample"]),
        name="score_sample",
    )(*args)
    return cdec, outs, w_in_bf, w_out_bf


def _mix_call(x2, weights, to_narrow, smp, smp_states, smp_cdec, *, n_seq, seq_len, tile, chunks, smp_len):
    n_rows = n_seq * seq_len
    assert seq_len % tile == 0 and tile % chunks == 0
    chunk = tile // chunks
    steps = seq_len // tile
    n_steps = n_seq * steps
    nlev, cdec, consts = _chunk_consts(chunk, min(seq_len, chunk))
    cos, sin = _rope_tables(np.arange(seq_len))
    n_smp = smp_states[0].shape[0]
    assert n_smp % n_steps == 0
    seq_per_step = n_smp // n_steps
    smp_rows = seq_per_step * smp_len
    assert smp_rows % SUBLANES == 0

    row_map = lambda b, i: (b * steps + i, 0)
    pos_map = lambda b, i: (i, 0)
    state_spec = pl.BlockSpec((1, N_HEADS, D_HEAD, D_HEAD), lambda b, i: (b, 0, 0, 0))
    smp_state_spec = pl.BlockSpec((seq_per_step, N_HEADS, D_HEAD, D_HEAD), lambda b, i: (b * steps + i, 0, 0, 0))
    smp_spec = lambda a: pl.BlockSpec((smp_rows, a.shape[1]), row_map)
    narrow_specs = [_row_block_spec(w.shape, n_steps, steps) for w in to_narrow]
    smp_scores, *smp_factors, smp_keep = smp
    side_in = [*smp_factors, smp_keep, smp_scores]
    in_specs = ([pl.BlockSpec((tile, D_MODEL), row_map), pl.BlockSpec((tile, D_HEAD), pos_map),
                 pl.BlockSpec((tile, D_HEAD), pos_map)] + [_full(w.shape) for w in weights]
                + [_full(c.shape) for c in consts] + narrow_specs + [smp_spec(a) for a in side_in]
                + [smp_state_spec, smp_state_spec])
    args = [x2, cos, sin, *weights, *consts, *to_narrow, *side_in, *smp_states]
    state_shape = lambda n: jax.ShapeDtypeStruct((n, N_HEADS, D_HEAD, D_HEAD), F32)
    return pl.pallas_call(
        functools.partial(_mix_kernel, sample=False, tile=tile, chunks=chunks, seq_rows=min(seq_len, chunk),
                          nlev=nlev, cdec=cdec, smp_cdec=smp_cdec, smp_seq_per_step=seq_per_step),
        grid=(n_seq, steps),
        in_specs=in_specs,
        out_specs=[pl.BlockSpec((tile, D_MODEL), row_map), state_spec, state_spec] + narrow_specs
        + [smp_spec(smp_scores), smp_state_spec, smp_state_spec],
        out_shape=[jax.ShapeDtypeStruct((n_rows, D_MODEL), F32), state_shape(n_seq), state_shape(n_seq)]
        + [jax.ShapeDtypeStruct(w.shape, BF16) for w in to_narrow]
        + [jax.ShapeDtypeStruct(smp_scores.shape, F32), state_shape(n_smp), state_shape(n_smp)],
        scratch_shapes=_mix_scratch(tile) + [pltpu.VMEM((tile, 2 * GROUP_W), F32)]
        + [pltpu.VMEM((tile, GROUP_W), F32)] * 2 + [pltpu.VMEM((N_HEADS, D_HEAD, D_HEAD), F32)] * 2,
        compiler_params=pltpu.CompilerParams(dimension_semantics=("arbitrary", "arbitrary"),
                                             vmem_limit_bytes=VMEM_LIMITS["mix_prompt"]),
        name="mix_prompt",
    )(*args)


def _ffn_kernel(*refs, sample, groups, rows):
    x_ref, w2_ref, wup_ref, cw_ref, cb_ref, wdown_ref, wf_ref = refs[:7]
    refs = refs[7:]
    if sample:
        (hist_ref, os_ref, keep_ref, na_ref, nb_ref, wout_ref, y_ref, hist_out_ref, h_scr, act_scr, mixed_scr,
         xin) = refs
        for h in range(N_HEADS):
            for group, norm, w_ref in ((0, _rmsnorm, na_ref), (1, _groupnorm, nb_ref)):
                cols = slice(group * GROUP_W + h * D_HEAD, group * GROUP_W + (h + 1) * D_HEAD)
                slot = KEPT_GROUPS.index(4 * group + 3)
                gate = keep_ref[:, slot * GROUP_W + h * D_HEAD:slot * GROUP_W + (h + 1) * D_HEAD]
                mixed_scr[:, cols] = (norm(os_ref[:, cols], w_ref[...]) * (gate * _sigmoid(gate))).astype(BF16)
        xin[...] = x_ref[...] + _mm(mixed_scr[...], wout_ref[...])
    else:
        y_ref, hist_out_ref, h_scr, act_scr, tail_scr = refs
        xin = x_ref
        step = pl.program_id(1)

        @pl.when(step == 0)
        def _():
            tail_scr[...] = jnp.zeros_like(tail_scr)

    G, L, P = groups, rows, SUBLANES
    parts = 1 if sample else ROW_PARTS
    LP = L // parts
    width = D_HEAD if sample else FF_COLS
    row_id = lax.broadcasted_iota(jnp.int32, (G, P, width), 1)

    def shifted(up, prev2, prev1):
        r1, r2 = pltpu.roll(up, 1, 1), pltpu.roll(up, 2, 1)
        top1 = jnp.where(row_id == 0, prev1, r1[:, :P])
        top2 = jnp.where(row_id == 0, prev2, jnp.where(row_id == 1, prev1, r2[:, :P]))
        if LP == P:
            return top1, top2
        return jnp.concatenate([top1, r1[:, P:]], axis=1), jnp.concatenate([top2, r2[:, P:]], axis=1)

    tails = {}
    for part in range(parts):
        rows_p = slice(part * G * LP, (part + 1) * G * LP)
        h_scr[rows_p, :] = _rmsnorm(xin[rows_p, :], w2_ref[...]).astype(BF16)
        for n in range(0, D_FF, FF_COLS):
            ups = [_mm(h_scr[rows_p, :], wup_ref[0, :, base:base + FF_COLS]).reshape(G, LP, FF_COLS)
                   for base in (n, D_FF + n)]
            for off in range(0, FF_COLS, width):
                conv = []
                for up_full, base in zip(ups, (n, D_FF + n)):
                    cols = slice(base + off, base + off + width)
                    up = up_full[:, :, off:off + width]
                    if sample:
                        prev2, prev1 = hist_ref[:, 0:1, cols], hist_ref[:, 1:2, cols]
                        hist_out_ref[:, :, cols] = up[:, LP - 2:, :]
                    else:
                        if part == 0:
                            prev2, prev1 = tail_scr[:, P - 2:P - 1, cols], tail_scr[:, P - 1:P, cols]
                        else:
                            prev = tails[cols.start]
                            prev2, prev1 = prev[:, P - 2:P - 1, :], prev[:, P - 1:P, :]
                        tails[cols.start] = up[:, LP - P:, :]
                        if part == parts - 1:
                            tail_scr[:, :, cols] = up[:, LP - P:, :]
                    sh1, sh2 = shifted(up, prev2, prev1)
                    conv.append(cb_ref[:, cols] + cw_ref[0:1, cols] * sh2 + cw_ref[1:2, cols] * sh1
                                + cw_ref[2:3, cols] * up)
                u, g = (c.reshape(G * LP, width).astype(BF16) for c in conv)
                one = jnp.ones((), BF16)
                act_scr[rows_p, n + off:n + off + width] = (g * (one / (one + jnp.exp(-g)))) * u

        x2 = xin[rows_p, :] + _mm(act_scr[rows_p, :], wdown_ref[0])
        y_ref[rows_p, :] = _rmsnorm(x2, wf_ref[...])

    if not sample:
        @pl.when(step == pl.num_programs(1) - 1)
        def _():
            hist_out_ref[...] = tail_scr[:, P - 2:, :]


def _ffn_call(x, w2, w_up, cw, cb, w_down, wf, smp, *, n_seq, seq_len, tile):
    sample = smp is not None
    n_rows = n_seq * seq_len
    hist_shape = jax.ShapeDtypeStruct((n_seq, CONV_W - 1, FF2), F32)
    if sample:
        hist, scores, keep, na, nb, w_out = smp
        groups, rows = tile // seq_len, seq_len
        grid = (n_rows // tile,)
        row_map = lambda n: (n, 0)
        hist_spec = pl.BlockSpec((groups, CONV_W - 1, FF2), lambda n: (n, 0, 0))
        extra_in = [hist_spec, pl.BlockSpec((tile, scores.shape[1]), row_map),
                    pl.BlockSpec((tile, keep.shape[1]), row_map), _full(na.shape), _full(nb.shape), _full(w_out.shape)]
        extra_args = [hist, scores, keep, na, nb, w_out]
        scratch = [pltpu.VMEM((tile, 2 * GROUP_W), BF16), pltpu.VMEM((tile, D_MODEL), F32)]
    else:
        assert seq_len % tile == 0
        groups, rows = 1, tile
        steps = seq_len // tile
        grid = (n_seq, steps)
        row_map = lambda b, i: (b * steps + i, 0)
        hist_spec = pl.BlockSpec((1, CONV_W - 1, FF2), lambda b, i: (b, 0, 0))
        extra_in, extra_args = [], []
        scratch = [pltpu.VMEM((1, SUBLANES, FF2), F32)]
    in_specs = [pl.BlockSpec((tile, D_MODEL), row_map), _full(w2.shape), _full(w_up.shape), _full(cw.shape),
                _full(cb.shape), _full(w_down.shape), _full(wf.shape)] + extra_in
    args = [x, w2, w_up, cw, cb, w_down, wf] + extra_args
    name = "ffn_sample" if sample else "ffn_prompt"
    return pl.pallas_call(
        functools.partial(_ffn_kernel, sample=sample, groups=groups, rows=rows),
        grid=grid,
        in_specs=in_specs,
        out_specs=[pl.BlockSpec((tile, D_MODEL), row_map), hist_spec],
        out_shape=[jax.ShapeDtypeStruct((n_rows, D_MODEL), F32), hist_shape],
        scratch_shapes=[pltpu.VMEM((tile, D_MODEL), BF16), pltpu.VMEM((tile, D_FF), BF16)] + scratch,
        compiler_params=pltpu.CompilerParams(dimension_semantics=("arbitrary",) * len(grid),
                                             vmem_limit_bytes=VMEM_LIMITS[name]),
        name=name,
    )(*args)


def kernel(x_prompt, x_sample, state_hgrn, state_ret, state_conv, w_norm1, w_in, hgrn_lb, hgrn_norm_w, ret_norm_w,
           w_out, w_norm2, w_ffn_in, conv_w, conv_b, w_ffn_out, w_norm_f):
    assert w_in.shape == (1, D_MODEL, IN_WIDTH) and hgrn_lb.shape == (2, GROUP_W)
    n_seq, seq_len, _ = x_prompt.shape
    n_smp, smp_len, _ = x_sample.shape
    xs = x_sample.reshape(n_smp * smp_len, D_MODEL)

    smp_cdec, smp, w_in_bf, w_out_bf = _smp_score_call(
        xs, (w_norm1, w_in, hgrn_lb, hgrn_norm_w, ret_norm_w, w_out), seq_len=smp_len, tile=256)
    mix_w = (w_norm1, w_in_bf, hgrn_lb, hgrn_norm_w, ret_norm_w, w_out_bf)
    x1, ha_p, rb_p, w_up, w_down, smp_scores, ha_s, rb_s = _mix_call(
        x_prompt.reshape(n_seq * seq_len, D_MODEL), mix_w, (w_ffn_in, w_ffn_out), smp,
        (state_hgrn[0], state_ret[0]), smp_cdec, n_seq=n_seq, seq_len=seq_len, tile=512, chunks=2, smp_len=smp_len)
    ffn_w = (w_norm2, w_up, conv_w[0], conv_b, w_down, w_norm_f.reshape(1, D_MODEL))
    y_p, cv_p = _ffn_call(x1, *ffn_w, None, n_seq=n_seq, seq_len=seq_len, tile=512)
    y_s, cv_s = _ffn_call(xs, *ffn_w, (state_conv[0], smp_scores, smp[-1], hgrn_norm_w, ret_norm_w, w_out_bf),
                          n_seq=n_smp, seq_len=smp_len, tile=256)
    return (y_p.reshape(x_prompt.shape), y_s.reshape(x_sample.shape), ha_p[None], rb_p[None], cv_p[None],
            ha_s[None], rb_s[None], cv_s[None])
```

```python
import functools

import numpy as np
import jax
import jax.numpy as jnp
from jax import lax
from jax.experimental import pallas as pl
from jax.experimental.pallas import tpu as pltpu

F32 = jnp.float32
BF16 = jnp.bfloat16

D_MODEL = 1024
N_HEADS = 4
D_HEAD = 128
GROUP_W = N_HEADS * D_HEAD
IN_WIDTH = 8 * GROUP_W
D_FF = 2816
FF2 = 2 * D_FF
CONV_W = 3
PAST_LEN = 16384
ROPE_BASE = 10000.0
EPS = 1e-6
LOG2E = 1.4426950408889634

SUBLANES = 8
BF16_ROWS = 16
PROJ_COLS = 512
FF_COLS = 256
ROW_PARTS = 2
KEPT_GROUPS = (2, 3, 6, 7)
MIB = 1024 * 1024
VMEM_LIMITS = {"score_sample": 54 * MIB, "mix_prompt": 52 * MIB, "ffn_prompt": 32 * MIB, "ffn_sample": 44 * MIB}


def _mm(a, b):
    return jnp.dot(a, b, preferred_element_type=F32)


def _mm_nt(a, b):
    return lax.dot_general(a, b, (((1,), (1,)), ((), ())), preferred_element_type=F32)


def _mm_tn(a, b):
    return lax.dot_general(a, b, (((0,), (0,)), ((), ())), preferred_element_type=F32)


def _sigmoid(x):
    return 1.0 / (1.0 + jnp.exp(-x))


def _rmsnorm(x, w):
    return x * lax.rsqrt(jnp.mean(x * x, axis=-1, keepdims=True) + EPS) * w


def _groupnorm(x, w):
    xc = x - jnp.mean(x, axis=-1, keepdims=True)
    return xc * lax.rsqrt(jnp.mean(xc * xc, axis=-1, keepdims=True) + EPS) * w


def _chunk_consts(chunk, seq_len):
    nlev = int(np.log2(seq_len))
    assert 1 << nlev == seq_len and chunk % seq_len == 0
    r = np.arange(chunk)
    rr, cc = r[:, None], r[None, :]
    same_seq = (rr // seq_len) == (cc // seq_len)
    cum = (same_seq & (cc <= rr)).astype(np.float32)
    x = rr ^ cc
    bit_len = np.where(x > 0, np.floor(np.log2(np.maximum(x, 1))).astype(np.int64) + 1, 0)
    level = np.where(same_seq & (cc <= rr), bit_len, -1).astype(np.int32)

    pos = r % seq_len
    log_gamma = np.log1p(-np.exp2(-5.0 - np.arange(N_HEADS, dtype=np.float64)))[:, None, None]
    rel = (pos[:, None] - pos[None, :]).astype(np.float64)[None]
    causal = (same_seq & (cc <= rr))[None]
    dec = np.where(causal, np.exp(np.where(causal, rel, 0.0) * log_gamma), 0.0)
    ones = np.ones((1, 1, D_HEAD))
    inner = np.exp((pos + 1.0)[None, :, None] * log_gamma) * ones
    sdec = np.exp((seq_len - 1.0 - pos)[None, :, None] * log_gamma) * ones
    cdec = tuple(float(v) for v in np.exp(seq_len * log_gamma[:, 0, 0]))
    consts = (jnp.asarray(cum, BF16), jnp.asarray(level, BF16), jnp.asarray(dec, F32), jnp.asarray(inner, F32),
              jnp.asarray(sdec, F32))
    return nlev, cdec, consts


def _rope_tables(pos):
    half = D_HEAD // 2
    inv = 1.0 / (ROPE_BASE ** (np.arange(half, dtype=np.float64) / half))
    ang = np.asarray(pos, np.float64)[:, None] * inv[None, :]
    cos, sin = np.cos(ang), np.sin(ang)
    return (jnp.asarray(np.concatenate([cos, cos], axis=-1), F32),
            jnp.asarray(np.concatenate([-sin, sin], axis=-1), F32))


def _mix_kernel(*refs, sample, tile, chunks, seq_rows, nlev, cdec, smp_cdec, smp_seq_per_step):
    (x_ref, cos_ref, sin_ref, w1_ref, win_ref, lbp_ref, na_ref, nb_ref, wout_ref,
     cum_ref, level_ref, dec_ref, inner_ref, sdec_ref) = refs[:14]
    refs = refs[14:]
    if sample:
        o_scr, qe_scr, kh_scr, ex_scr, qi_scr, ks_scr, keep_ref, win_bf_ref, wout_bf_ref = refs[:9]
        h_scr, proj_scr, d_scr, k_scr = refs[9:]

        @pl.when(pl.program_id(0) == 0)
        def _():
            for n in range(0, IN_WIDTH, PROJ_COLS):
                win_bf_ref[:, n:n + PROJ_COLS] = win_ref[0, :, n:n + PROJ_COLS].astype(BF16)
            wout_bf_ref[...] = wout_ref[0].astype(BF16)

        win_ref = win_bf_ref
    else:
        wide_refs = refs[:2]
        qe_ref, kh_ref, ex_ref, qi_ref, ks_ref, keep_in_ref, os_in_ref, sa_in_ref, sb_in_ref = refs[2:11]
        x1_ref, sa_out_ref, sb_out_ref = refs[11:14]
        narrow_refs = refs[14:16]
        os_out_ref, sas_out_ref, sbs_out_ref = refs[16:19]
        h_scr, proj_scr, o_scr, d_scr, k_scr, sa_scr, sb_scr = refs[19:]
        step = pl.program_id(1)

        @pl.when(step == 0)
        def _():
            sa_scr[...] = jnp.zeros_like(sa_scr)
            sb_scr[...] = jnp.zeros_like(sb_scr)

    C = tile // chunks

    def col(group, h):
        return slice(group * GROUP_W + h * D_HEAD, group * GROUP_W + (h + 1) * D_HEAD)

    def head(h):
        return slice(h * D_HEAD, (h + 1) * D_HEAD)

    def block_rows(x, m, row):
        x3 = x.reshape(C // m, m, D_HEAD)
        return jnp.broadcast_to(x3[:, row:row + 1, :], x3.shape).reshape(C, D_HEAD)

    def upper_rows(x, m):
        return x.reshape(C // m, 2, m // 2, x.shape[-1])[:, 1].reshape(C // 2, x.shape[-1])

    def put_upper_rows(x, xu, m):
        x4 = x.reshape(C // m, 2, m // 2, x.shape[-1])
        xu4 = xu.reshape(C // m, 1, m // 2, x.shape[-1])
        return jnp.concatenate([x4[:, 0:1], xu4], axis=1).reshape(C, x.shape[-1])

    def project(rs):
        h_scr[rs, :] = _rmsnorm(x_ref[rs, :], w1_ref[...]).astype(BF16)
        for n in range(0, IN_WIDTH, PROJ_COLS):
            proj_scr[rs, n:n + PROJ_COLS] = _mm(h_scr[rs, :], win_ref[:, n:n + PROJ_COLS])

    def score(rs):
        lb0, lb1 = lbp_ref[0:1, :], lbp_ref[1:2, :]
        lb_max = jnp.maximum(lb0, lb1)
        e0, e1 = jnp.exp(lb0 - lb_max), jnp.exp(lb1 - lb_max)
        lb = e0 / (e0 + e1)

        row_id = lax.broadcasted_iota(jnp.int32, (C, D_HEAD), 0)

        f = lb + (1.0 - lb) * _sigmoid(proj_scr[rs, GROUP_W:2 * GROUP_W])
        k_scr[rs, :] = 1.0 - f
        g = jnp.log(f)
        g_hi = g.astype(BF16)
        g_lo = (g - g_hi.astype(F32)).astype(BF16)
        d_scr[rs, :] = _mm(cum_ref[...], g_hi) + _mm(cum_ref[...], g_lo)

        signs = [jnp.where((row_id & (1 << (lev - 1))) != 0, LOG2E, -LOG2E) for lev in range(3, nlev + 1)]

        for h in range(N_HEADS):
            hs = head(h)
            q = proj_scr[rs, col(0, h)]
            k = k_scr[rs, hs]
            v = proj_scr[rs, col(2, h)].astype(BF16)
            b = d_scr[rs, hs]
            a = jnp.where(level_ref[...] == 0, _mm_nt(q.astype(BF16), k.astype(BF16)).astype(BF16), 0.0)
            for lev in range(1, nlev + 1):
                m = 1 << lev
                upper = (row_id & (m // 2)) != 0
                if lev == 1:
                    z = jnp.where(upper, q * (1.0 - k), k)
                elif lev == 2:
                    fh = 1.0 - k
                    pos4 = row_id & 3
                    decay = jnp.where(pos4 == 0, pltpu.roll(fh, C - 1, 0),
                                      jnp.where(pos4 == 1, 1.0, jnp.where(pos4 == 2, fh, fh * pltpu.roll(fh, 1, 0))))
                    z = jnp.where(upper, q, k) * decay
                else:
                    z = jnp.where(upper, q, k) * jnp.exp2((b - block_rows(b, m, m // 2 - 1)) * signs[lev - 3])
                if m < 2 * BF16_ROWS:
                    z = z.astype(BF16)
                    a = jnp.where(level_ref[...] == lev, _mm_nt(z, z).astype(BF16), a)
                else:
                    zq = upper_rows(z, m).astype(BF16)
                    zk = z.astype(BF16)
                    if m > D_HEAD:
                        p = [_mm_nt(zq[i * (m // 2):(i + 1) * (m // 2)], zk[i * m:i * m + m // 2])
                             for i in range(C // m)]
                        width = m // 2
                    else:
                        p = [_mm_nt(zq[i * (D_HEAD // 2):(i + 1) * (D_HEAD // 2)], zk[i * D_HEAD:(i + 1) * D_HEAD])
                             for i in range(C // D_HEAD)]
                        width = D_HEAD
                    full = jnp.concatenate([jnp.concatenate([pi] * (C // width), axis=1) for pi in p], axis=0)
                    lvl_u = upper_rows(level_ref[...], m)
                    a = put_upper_rows(a, jnp.where(lvl_u == lev, full.astype(BF16), upper_rows(a, m)), m)
            o = _mm(a, v)
            eb = jnp.exp(b)
            qe = q * eb
            kh = k * jnp.exp(block_rows(b, seq_rows, seq_rows - 1) - b)
            if sample:
                o_scr[rs, col(0, h)] = o
                qe_scr[rs, hs] = qe
                kh_scr[rs, hs] = kh
                e_all = block_rows(eb, seq_rows, seq_rows - 1)
                e_hi = e_all.astype(BF16).astype(F32)
                e_mid = (e_all - e_hi).astype(BF16).astype(F32)
                e_lo = e_all - e_hi - e_mid
                pos = row_id & (seq_rows - 1)
                ex_scr[rs, hs] = jnp.where(pos == 0, e_hi, jnp.where(pos == 1, e_mid, jnp.where(pos == 2, e_lo, 0.0)))
            else:
                st = sa_scr[h]
                o = o + _mm_nt(qe.astype(BF16), st.astype(BF16))
                sa_scr[h] = st * eb[C - 1:C, :] + _mm_tn(v, kh.astype(BF16))
                gate = proj_scr[rs, col(3, h)]
                o_scr[rs, col(0, h)] = _rmsnorm(o, na_ref[...]) * (gate * _sigmoid(gate))

        cos, sin = cos_ref[rs, :], sin_ref[rs, :]
        for h in range(N_HEADS):
            hs = head(h)
            q = proj_scr[rs, col(4, h)]
            k = proj_scr[rs, col(5, h)]
            v = proj_scr[rs, col(6, h)].astype(BF16)
            qr = q * cos + pltpu.roll(q, D_HEAD // 2, 1) * sin
            kr = (k * cos + pltpu.roll(k, D_HEAD // 2, 1) * sin) * (D_HEAD ** -0.5)
            a = _mm_nt(qr.astype(BF16), kr.astype(BF16)) * dec_ref[h]
            o = _mm(a.astype(BF16), v)
            qi = qr * inner_ref[h]
            ks = kr * sdec_ref[h]
            if sample:
                o_scr[rs, col(1, h)] = o
                qi_scr[rs, hs] = qi
                ks_scr[rs, hs] = ks
            else:
                st = sb_scr[h]
                o = o + _mm(qi.astype(BF16), st.astype(BF16))
                sb_scr[h] = cdec[h] * st + _mm_tn(ks.astype(BF16), v)
                gate = proj_scr[rs, col(7, h)]
                o_scr[rs, col(1, h)] = _groupnorm(o, nb_ref[...]) * (gate * _sigmoid(gate))

        if sample:
            for slot, group in enumerate(KEPT_GROUPS):
                keep_ref[:, slot * GROUP_W:(slot + 1) * GROUP_W] = proj_scr[rs, group * GROUP_W:(group + 1) * GROUP_W]

    def kept(group, h):
        slot = KEPT_GROUPS.index(group)
        return slice(slot * GROUP_W + h * D_HEAD, slot * GROUP_W + (h + 1) * D_HEAD)

    def apply_states():
        zeros8 = jnp.zeros((SUBLANES, D_HEAD), F32)
        sel8 = jnp.where(lax.broadcasted_iota(jnp.int32, (SUBLANES, D_HEAD), 0) < 3, 1.0, 0.0)

        def pair_readout(lhs_ref, st_ref, j, rows, group, h):
            lhs = jnp.concatenate([lhs_ref[rows, head(h)], lhs_ref[rows, head(h + 1)]], axis=0).astype(BF16)
            w = jnp.concatenate([st_ref[j, h], st_ref[j, h + 1]], axis=1).astype(BF16)
            oo = _mm(lhs, w)
            os_out_ref[rows, col(group, h)] = os_in_ref[rows, col(group, h)] + oo[:SUBLANES, :D_HEAD]
            os_out_ref[rows, col(group, h + 1)] = os_in_ref[rows, col(group, h + 1)] + oo[SUBLANES:, D_HEAD:]

        for j in range(smp_seq_per_step):
            rows = slice(j * SUBLANES, (j + 1) * SUBLANES)
            for h in range(0, N_HEADS, 2):
                pair_readout(qe_ref, sa_in_ref, j, rows, 0, h)
                pair_readout(qi_ref, sb_in_ref, j, rows, 1, h)
            for h in range(N_HEADS):
                hs = head(h)
                lhs = jnp.concatenate([kh_ref[rows, hs], ex_ref[rows, hs]], axis=0).astype(BF16)
                v = keep_in_ref[rows, kept(2, h)]
                rhs = jnp.concatenate([jnp.concatenate([v, zeros8], axis=1),
                                       jnp.concatenate([zeros8, sel8], axis=1)], axis=0).astype(BF16)
                upd = _mm_tn(lhs, rhs)
                sas_out_ref[j, h] = sa_in_ref[j, h] * upd[:, D_HEAD:] + upd[:, :D_HEAD]
                v = keep_in_ref[rows, kept(6, h)].astype(BF16)
                sbs_out_ref[j, h] = smp_cdec[h] * sb_in_ref[j, h] + _mm_tn(ks_ref[rows, hs].astype(BF16), v)

    chunk_rows = [slice(c * C, (c + 1) * C) for c in range(chunks)]
    if not sample:
        apply_states()
        for wide, narrow in zip(wide_refs, narrow_refs):
            narrow[...] = wide[...].astype(BF16)
    for rs in chunk_rows:
        project(rs)
    for rs in chunk_rows:
        score(rs)
    if not sample:
        x1_ref[...] = x_ref[...] + _mm(o_scr[...].astype(BF16), wout_ref[...])

        @pl.when(step == pl.num_programs(1) - 1)
        def _():
            for h in range(N_HEADS):
                sa_out_ref[0, h] = sa_scr[h].T
                sb_out_ref[0, h] = sb_scr[h]


def _full(shape):
    return pl.BlockSpec(shape, lambda *_: (0,) * len(shape))


def _row_block_spec(shape, n_steps, inner_steps):
    _, n_rows, width = shape
    share = next(k for k in (1, 2, 4, 8) if n_steps % k == 0 and n_rows % ((n_steps // k) * BF16_ROWS) == 0)
    return pl.BlockSpec((1, n_rows // (n_steps // share), width),
                        lambda b, i: (0, (b * inner_steps + i) // share, 0))


def _mix_scratch(tile):
    return [pltpu.VMEM((tile, D_MODEL), BF16), pltpu.VMEM((tile, IN_WIDTH), F32)]


def _smp_score_call(x2, weights, *, seq_len, tile):
    n_rows = x2.shape[0]
    assert n_rows % tile == 0 and tile % seq_len == 0
    nlev, cdec, consts = _chunk_consts(tile, seq_len)
    cos, sin = _rope_tables(np.tile(PAST_LEN + np.arange(seq_len), tile // seq_len))
    rows = lambda width: pl.BlockSpec((tile, width), lambda n: (n, 0))
    widths = [2 * GROUP_W] + [GROUP_W] * 5 + [len(KEPT_GROUPS) * GROUP_W]
    args = [x2, cos, sin, *weights, *consts]
    w_in, w_out = weights[1], weights[5]
    narrow_shapes = [w.shape[1:] for w in (w_in, w_out)]
    *outs, w_in_bf, w_out_bf = pl.pallas_call(
        functools.partial(_mix_kernel, sample=True, tile=tile, chunks=1, seq_rows=seq_len, nlev=nlev, cdec=cdec,
                          smp_cdec=None, smp_seq_per_step=0),
        grid=(n_rows // tile,),
        in_specs=[rows(D_MODEL)] + [_full(a.shape) for a in args[1:]],
        out_specs=[rows(w) for w in widths] + [_full(shape) for shape in narrow_shapes],
        out_shape=[jax.ShapeDtypeStruct((n_rows, w), F32) for w in widths]
        + [jax.ShapeDtypeStruct(shape, BF16) for shape in narrow_shapes],
        scratch_shapes=_mix_scratch(tile) + [pltpu.VMEM((tile, GROUP_W), F32)] * 2,
        compiler_params=pltpu.CompilerParams(dimension_semantics=("arbitrary",),
                                             vmem_limit_bytes=VMEM_LIMITS["score_sample"]),
        name="score_sample",
    )(*args)
    return cdec, outs, w_in_bf, w_out_bf


def _mix_call(x2, weights, to_narrow, smp, smp_states, smp_cdec, *, n_seq, seq_len, tile, chunks, smp_len):
    n_rows = n_seq * seq_len
    assert seq_len % tile == 0 and tile % chunks == 0
    chunk = tile // chunks
    steps = seq_len // tile
    n_steps = n_seq * steps
    nlev, cdec, consts = _chunk_consts(chunk, min(seq_len, chunk))
    cos, sin = _rope_tables(np.arange(seq_len))
    n_smp = smp_states[0].shape[0]
    assert n_smp % n_steps == 0
    seq_per_step = n_smp // n_steps
    smp_rows = seq_per_step * smp_len
    assert smp_rows % SUBLANES == 0

    row_map = lambda b, i: (b * steps + i, 0)
    pos_map = lambda b, i: (i, 0)
    state_spec = pl.BlockSpec((1, N_HEADS, D_HEAD, D_HEAD), lambda b, i: (b, 0, 0, 0))
    smp_state_spec = pl.BlockSpec((seq_per_step, N_HEADS, D_HEAD, D_HEAD), lambda b, i: (b * steps + i, 0, 0, 0))
    smp_spec = lambda a: pl.BlockSpec((smp_rows, a.shape[1]), row_map)
    narrow_specs = [_row_block_spec(w.shape, n_steps, steps) for w in to_narrow]
    smp_scores, *smp_factors, smp_keep = smp
    side_in = [*smp_factors, smp_keep, smp_scores]
    in_specs = ([pl.BlockSpec((tile, D_MODEL), row_map), pl.BlockSpec((tile, D_HEAD), pos_map),
                 pl.BlockSpec((tile, D_HEAD), pos_map)] + [_full(w.shape) for w in weights]
                + [_full(c.shape) for c in consts] + narrow_specs + [smp_spec(a) for a in side_in]
                + [smp_state_spec, smp_state_spec])
    args = [x2, cos, sin, *weights, *consts, *to_narrow, *side_in, *smp_states]
    state_shape = lambda n: jax.ShapeDtypeStruct((n, N_HEADS, D_HEAD, D_HEAD), F32)
    return pl.pallas_call(
        functools.partial(_mix_kernel, sample=False, tile=tile, chunks=chunks, seq_rows=min(seq_len, chunk),
                          nlev=nlev, cdec=cdec, smp_cdec=smp_cdec, smp_seq_per_step=seq_per_step),
        grid=(n_seq, steps),
        in_specs=in_specs,
        out_specs=[pl.BlockSpec((tile, D_MODEL), row_map), state_spec, state_spec] + narrow_specs
        + [smp_spec(smp_scores), smp_state_spec, smp_state_spec],
        out_shape=[jax.ShapeDtypeStruct((n_rows, D_MODEL), F32), state_shape(n_seq), state_shape(n_seq)]
        + [jax.ShapeDtypeStruct(w.shape, BF16) for w in to_narrow]
        + [jax.ShapeDtypeStruct(smp_scores.shape, F32), state_shape(n_smp), state_shape(n_smp)],
        scratch_shapes=_mix_scratch(tile) + [pltpu.VMEM((tile, 2 * GROUP_W), F32)]
        + [pltpu.VMEM((tile, GROUP_W), F32)] * 2 + [pltpu.VMEM((N_HEADS, D_HEAD, D_HEAD), F32)] * 2,
        compiler_params=pltpu.CompilerParams(dimension_semantics=("arbitrary", "arbitrary"),
                                             vmem_limit_bytes=VMEM_LIMITS["mix_prompt"]),
        name="mix_prompt",
    )(*args)


def _ffn_kernel(*refs, sample, groups, rows):
    x_ref, w2_ref, wup_ref, cw_ref, cb_ref, wdown_ref, wf_ref = refs[:7]
    refs = refs[7:]
    if sample:
        (hist_ref, os_ref, keep_ref, na_ref, nb_ref, wout_ref, y_ref, hist_out_ref, h_scr, act_scr, mixed_scr,
         xin) = refs
        for h in range(N_HEADS):
            for group, norm, w_ref in ((0, _rmsnorm, na_ref), (1, _groupnorm, nb_ref)):
                cols = slice(group * GROUP_W + h * D_HEAD, group * GROUP_W + (h + 1) * D_HEAD)
                slot = KEPT_GROUPS.index(4 * group + 3)
                gate = keep_ref[:, slot * GROUP_W + h * D_HEAD:slot * GROUP_W + (h + 1) * D_HEAD]
                mixed_scr[:, cols] = (norm(os_ref[:, cols], w_ref[...]) * (gate * _sigmoid(gate))).astype(BF16)
        xin[...] = x_ref[...] + _mm(mixed_scr[...], wout_ref[...])
    else:
        y_ref, hist_out_ref, h_scr, act_scr, tail_scr = refs
        xin = x_ref
        step = pl.program_id(1)

        @pl.when(step == 0)
        def _():
            tail_scr[...] = jnp.zeros_like(tail_scr)

    G, L, P = groups, rows, SUBLANES
    parts = 1 if sample else ROW_PARTS
    LP = L // parts
    row_id = lax.broadcasted_iota(jnp.int32, (G, P, FF_COLS), 1)

    def shifted(up, prev2, prev1):
        r1, r2 = pltpu.roll(up, 1, 1), pltpu.roll(up, 2, 1)
        top1 = jnp.where(row_id == 0, prev1, r1[:, :P])
        top2 = jnp.where(row_id == 0, prev2, jnp.where(row_id == 1, prev1, r2[:, :P]))
        if LP == P:
            return top1, top2
        return jnp.concatenate([top1, r1[:, P:]], axis=1), jnp.concatenate([top2, r2[:, P:]], axis=1)

    tails = {}
    for part in range(parts):
        rows_p = slice(part * G * LP, (part + 1) * G * LP)
        h_scr[rows_p, :] = _rmsnorm(xin[rows_p, :], w2_ref[...]).astype(BF16)
        for n in range(0, D_FF, FF_COLS):
            conv = []
            for cols in (slice(n, n + FF_COLS), slice(D_FF + n, D_FF + n + FF_COLS)):
                up = _mm(h_scr[rows_p, :], wup_ref[0, :, cols]).reshape(G, LP, FF_COLS)
                if sample:
                    prev2, prev1 = hist_ref[:, 0:1, cols], hist_ref[:, 1:2, cols]
                    hist_out_ref[:, :, cols] = up[:, LP - 2:, :]
                else:
                    if part == 0:
                        prev2, prev1 = tail_scr[:, P - 2:P - 1, cols], tail_scr[:, P - 1:P, cols]
                    else:
                        prev = tails[cols.start]
                        prev2, prev1 = prev[:, P - 2:P - 1, :], prev[:, P - 1:P, :]
                    tails[cols.start] = up[:, LP - P:, :]
                    if part == parts - 1:
                        tail_scr[:, :, cols] = up[:, LP - P:, :]
                sh1, sh2 = shifted(up, prev2, prev1)
                conv.append(cb_ref[:, cols] + cw_ref[0:1, cols] * sh2 + cw_ref[1:2, cols] * sh1 + cw_ref[2:3, cols] * up)
            u, g = (c.reshape(G * LP, FF_COLS).astype(BF16) for c in conv)
            one = jnp.ones((), BF16)
            act_scr[rows_p, n:n + FF_COLS] = (g * (one / (one + jnp.exp(-g)))) * u

        x2 = xin[rows_p, :] + _mm(act_scr[rows_p, :], wdown_ref[0])
        y_ref[rows_p, :] = _rmsnorm(x2, wf_ref[...])

    if not sample:
        @pl.when(step == pl.num_programs(1) - 1)
        def _():
            hist_out_ref[...] = tail_scr[:, P - 2:, :]


def _ffn_call(x, w2, w_up, cw, cb, w_down, wf, smp, *, n_seq, seq_len, tile):
    sample = smp is not None
    n_rows = n_seq * seq_len
    hist_shape = jax.ShapeDtypeStruct((n_seq, CONV_W - 1, FF2), F32)
    if sample:
        hist, scores, keep, na, nb, w_out = smp
        groups, rows = tile // seq_len, seq_len
        grid = (n_rows // tile,)
        row_map = lambda n: (n, 0)
        hist_spec = pl.BlockSpec((groups, CONV_W - 1, FF2), lambda n: (n, 0, 0))
        extra_in = [hist_spec, pl.BlockSpec((tile, scores.shape[1]), row_map),
                    pl.BlockSpec((tile, keep.shape[1]), row_map), _full(na.shape), _full(nb.shape), _full(w_out.shape)]
        extra_args = [hist, scores, keep, na, nb, w_out]
        scratch = [pltpu.VMEM((tile, 2 * GROUP_W), BF16), pltpu.VMEM((tile, D_MODEL), F32)]
    else:
        assert seq_len % tile == 0
        groups, rows = 1, tile
        steps = seq_len // tile
        grid = (n_seq, steps)
        row_map = lambda b, i: (b * steps + i, 0)
        hist_spec = pl.BlockSpec((1, CONV_W - 1, FF2), lambda b, i: (b, 0, 0))
        extra_in, extra_args = [], []
        scratch = [pltpu.VMEM((1, SUBLANES, FF2), F32)]
    in_specs = [pl.BlockSpec((tile, D_MODEL), row_map), _full(w2.shape), _full(w_up.shape), _full(cw.shape),
                _full(cb.shape), _full(w_down.shape), _full(wf.shape)] + extra_in
    args = [x, w2, w_up, cw, cb, w_down, wf] + extra_args
    name = "ffn_sample" if sample else "ffn_prompt"
    return pl.pallas_call(
        functools.partial(_ffn_kernel, sample=sample, groups=groups, rows=rows),
        grid=grid,
        in_specs=in_specs,
        out_specs=[pl.BlockSpec((tile, D_MODEL), row_map), hist_spec],
        out_shape=[jax.ShapeDtypeStruct((n_rows, D_MODEL), F32), hist_shape],
        scratch_shapes=[pltpu.VMEM((tile, D_MODEL), BF16), pltpu.VMEM((tile, D_FF), BF16)] + scratch,
        compiler_params=pltpu.CompilerParams(dimension_semantics=("arbitrary",) * len(grid),
                                             vmem_limit_bytes=VMEM_LIMITS[name]),
        name=name,
    )(*args)


def kernel(x_prompt, x_sample, state_hgrn, state_ret, state_conv, w_norm1, w_in, hgrn_lb, hgrn_norm_w, ret_norm_w,
           w_out, w_norm2, w_ffn_in, conv_w, conv_b, w_ffn_out, w_norm_f):
    assert w_in.shape == (1, D_MODEL, IN_WIDTH) and hgrn_lb.shape == (2, GROUP_W)
    n_seq, seq_len, _ = x_prompt.shape
    n_smp, smp_len, _ = x_sample.shape
    xs = x_sample.reshape(n_smp * smp_len, D_MODEL)

    smp_cdec, smp, w_in_bf, w_out_bf = _smp_score_call(
        xs, (w_norm1, w_in, hgrn_lb, hgrn_norm_w, ret_norm_w, w_out), seq_len=smp_len, tile=256)
    mix_w = (w_norm1, w_in_bf, hgrn_lb, hgrn_norm_w, ret_norm_w, w_out_bf)
    x1, ha_p, rb_p, w_up, w_down, smp_scores, ha_s, rb_s = _mix_call(
        x_prompt.reshape(n_seq * seq_len, D_MODEL), mix_w, (w_ffn_in, w_ffn_out), smp,
        (state_hgrn[0], state_ret[0]), smp_cdec, n_seq=n_seq, seq_len=seq_len, tile=512, chunks=2, smp_len=smp_len)
    ffn_w = (w_norm2, w_up, conv_w[0], conv_b, w_down, w_norm_f.reshape(1, D_MODEL))
    y_p, cv_p = _ffn_call(x1, *ffn_w, None, n_seq=n_seq, seq_len=seq_len, tile=512)
    y_s, cv_s = _ffn_call(xs, *ffn_w, (state_conv[0], smp_scores, smp[-1], hgrn_norm_w, ret_norm_w, w_out_bf),
                          n_seq=n_smp, seq_len=smp_len, tile=256)
    return (y_p.reshape(x_prompt.shape), y_s.reshape(x_sample.shape), ha_p[None], rb_p[None], cv_p[None],
            ha_s[None], rb_s[None], cv_s[None])
```

```python
import functools

import numpy as np
import jax
import jax.numpy as jnp
from jax import lax
from jax.experimental import pallas as pl
from jax.experimental.pallas import tpu as pltpu

F32 = jnp.float32
BF16 = jnp.bfloat16

D_MODEL = 1024
N_HEADS = 4
D_HEAD = 128
GROUP_W = N_HEADS * D_HEAD
IN_WIDTH = 8 * GROUP_W
D_FF = 2816
FF2 = 2 * D_FF
CONV_W = 3
PAST_LEN = 16384
ROPE_BASE = 10000.0
EPS = 1e-6
LOG2E = 1.4426950408889634

SUBLANES = 8
BF16_ROWS = 16
PROJ_COLS = 512
FF_COLS = 256
ROW_PARTS = 2
KEPT_GROUPS = (2, 3, 6, 7)
MIB = 1024 * 1024
VMEM_LIMITS = {"score_sample": 51 * MIB, "mix_prompt": 49 * MIB, "ffn_prompt": 30 * MIB, "ffn_sample": 40 * MIB}


def _mm(a, b):
    return jnp.dot(a, b, preferred_element_type=F32)


def _mm_nt(a, b):
    return lax.dot_general(a, b, (((1,), (1,)), ((), ())), preferred_element_type=F32)


def _mm_tn(a, b):
    return lax.dot_general(a, b, (((0,), (0,)), ((), ())), preferred_element_type=F32)


def _sigmoid(x):
    return 1.0 / (1.0 + jnp.exp(-x))


def _rmsnorm(x, w):
    return x * lax.rsqrt(jnp.mean(x * x, axis=-1, keepdims=True) + EPS) * w


def _groupnorm(x, w):
    xc = x - jnp.mean(x, axis=-1, keepdims=True)
    return xc * lax.rsqrt(jnp.mean(xc * xc, axis=-1, keepdims=True) + EPS) * w


def _chunk_consts(chunk, seq_len):
    nlev = int(np.log2(seq_len))
    assert 1 << nlev == seq_len and chunk % seq_len == 0
    r = np.arange(chunk)
    rr, cc = r[:, None], r[None, :]
    same_seq = (rr // seq_len) == (cc // seq_len)
    cum = (same_seq & (cc <= rr)).astype(np.float32)
    x = rr ^ cc
    bit_len = np.where(x > 0, np.floor(np.log2(np.maximum(x, 1))).astype(np.int64) + 1, 0)
    level = np.where(same_seq & (cc <= rr), bit_len, -1).astype(np.int32)

    pos = r % seq_len
    log_gamma = np.log1p(-np.exp2(-5.0 - np.arange(N_HEADS, dtype=np.float64)))[:, None, None]
    rel = (pos[:, None] - pos[None, :]).astype(np.float64)[None]
    causal = (same_seq & (cc <= rr))[None]
    dec = np.where(causal, np.exp(np.where(causal, rel, 0.0) * log_gamma), 0.0)
    ones = np.ones((1, 1, D_HEAD))
    inner = np.exp((pos + 1.0)[None, :, None] * log_gamma) * ones
    sdec = np.exp((seq_len - 1.0 - pos)[None, :, None] * log_gamma) * ones
    cdec = tuple(float(v) for v in np.exp(seq_len * log_gamma[:, 0, 0]))
    consts = (jnp.asarray(cum, BF16), jnp.asarray(level, BF16), jnp.asarray(dec, F32), jnp.asarray(inner, F32),
              jnp.asarray(sdec, F32))
    return nlev, cdec, consts


def _rope_tables(pos):
    half = D_HEAD // 2
    inv = 1.0 / (ROPE_BASE ** (np.arange(half, dtype=np.float64) / half))
    ang = np.asarray(pos, np.float64)[:, None] * inv[None, :]
    cos, sin = np.cos(ang), np.sin(ang)
    return (jnp.asarray(np.concatenate([cos, cos], axis=-1), F32),
            jnp.asarray(np.concatenate([-sin, sin], axis=-1), F32))


def _mix_kernel(*refs, sample, tile, chunks, seq_rows, nlev, cdec, smp_cdec, smp_seq_per_step):
    (x_ref, cos_ref, sin_ref, w1_ref, win_ref, lbp_ref, na_ref, nb_ref, wout_ref,
     cum_ref, level_ref, dec_ref, inner_ref, sdec_ref) = refs[:14]
    refs = refs[14:]
    if sample:
        o_scr, qe_scr, kh_scr, ex_scr, qi_scr, ks_scr, keep_ref, win_bf_ref, wout_bf_ref = refs[:9]
        h_scr, proj_scr, d_scr, k_scr = refs[9:]

        @pl.when(pl.program_id(0) == 0)
        def _():
            for n in range(0, IN_WIDTH, PROJ_COLS):
                win_bf_ref[:, n:n + PROJ_COLS] = win_ref[0, :, n:n + PROJ_COLS].astype(BF16)
            wout_bf_ref[...] = wout_ref[0].astype(BF16)

        win_ref = win_bf_ref
    else:
        wide_refs = refs[:2]
        qe_ref, kh_ref, ex_ref, qi_ref, ks_ref, keep_in_ref, os_in_ref, sa_in_ref, sb_in_ref = refs[2:11]
        x1_ref, sa_out_ref, sb_out_ref = refs[11:14]
        narrow_refs = refs[14:16]
        os_out_ref, sas_out_ref, sbs_out_ref = refs[16:19]
        h_scr, proj_scr, o_scr, d_scr, k_scr, sa_scr, sb_scr = refs[19:]
        step = pl.program_id(1)

        @pl.when(step == 0)
        def _():
            sa_scr[...] = jnp.zeros_like(sa_scr)
            sb_scr[...] = jnp.zeros_like(sb_scr)

    C = tile // chunks

    def col(group, h):
        return slice(group * GROUP_W + h * D_HEAD, group * GROUP_W + (h + 1) * D_HEAD)

    def head(h):
        return slice(h * D_HEAD, (h + 1) * D_HEAD)

    def block_rows(x, m, row):
        x3 = x.reshape(C // m, m, D_HEAD)
        return jnp.broadcast_to(x3[:, row:row + 1, :], x3.shape).reshape(C, D_HEAD)

    def upper_rows(x, m):
        return x.reshape(C // m, 2, m // 2, x.shape[-1])[:, 1].reshape(C // 2, x.shape[-1])

    def put_upper_rows(x, xu, m):
        x4 = x.reshape(C // m, 2, m // 2, x.shape[-1])
        xu4 = xu.reshape(C // m, 1, m // 2, x.shape[-1])
        return jnp.concatenate([x4[:, 0:1], xu4], axis=1).reshape(C, x.shape[-1])

    def project(rs):
        h_scr[rs, :] = _rmsnorm(x_ref[rs, :], w1_ref[...]).astype(BF16)
        for n in range(0, IN_WIDTH, PROJ_COLS):
            proj_scr[rs, n:n + PROJ_COLS] = _mm(h_scr[rs, :], win_ref[:, n:n + PROJ_COLS])

    def score(rs):
        lb0, lb1 = lbp_ref[0:1, :], lbp_ref[1:2, :]
        lb_max = jnp.maximum(lb0, lb1)
        e0, e1 = jnp.exp(lb0 - lb_max), jnp.exp(lb1 - lb_max)
        lb = e0 / (e0 + e1)

        row_id = lax.broadcasted_iota(jnp.int32, (C, D_HEAD), 0)

        f = lb + (1.0 - lb) * _sigmoid(proj_scr[rs, GROUP_W:2 * GROUP_W])
        k_scr[rs, :] = 1.0 - f
        g = jnp.log(f)
        g_hi = g.astype(BF16)
        g_lo = (g - g_hi.astype(F32)).astype(BF16)
        d_scr[rs, :] = _mm(cum_ref[...], g_hi) + _mm(cum_ref[...], g_lo)

        signs = [jnp.where((row_id & (1 << (lev - 1))) != 0, LOG2E, -LOG2E) for lev in range(3, nlev + 1)]

        for h in range(N_HEADS):
            hs = head(h)
            q = proj_scr[rs, col(0, h)]
            k = k_scr[rs, hs]
            v = proj_scr[rs, col(2, h)].astype(BF16)
            b = d_scr[rs, hs]
            a = jnp.where(level_ref[...] == 0, _mm_nt(q.astype(BF16), k.astype(BF16)).astype(BF16), 0.0)
            for lev in range(1, nlev + 1):
                m = 1 << lev
                upper = (row_id & (m // 2)) != 0
                if lev == 1:
                    z = jnp.where(upper, q * (1.0 - k), k)
                elif lev == 2:
                    fh = 1.0 - k
                    pos4 = row_id & 3
                    decay = jnp.where(pos4 == 0, pltpu.roll(fh, C - 1, 0),
                                      jnp.where(pos4 == 1, 1.0, jnp.where(pos4 == 2, fh, fh * pltpu.roll(fh, 1, 0))))
                    z = jnp.where(upper, q, k) * decay
                else:
                    z = jnp.where(upper, q, k) * jnp.exp2((b - block_rows(b, m, m // 2 - 1)) * signs[lev - 3])
                if m < 2 * BF16_ROWS:
                    z = z.astype(BF16)
                    a = jnp.where(level_ref[...] == lev, _mm_nt(z, z).astype(BF16), a)
                else:
                    zq = upper_rows(z, m).astype(BF16)
                    zk = z.astype(BF16)
                    if m > D_HEAD:
                        p = [_mm_nt(zq[i * (m // 2):(i + 1) * (m // 2)], zk[i * m:i * m + m // 2])
                             for i in range(C // m)]
                        width = m // 2
                    else:
                        p = [_mm_nt(zq[i * (D_HEAD // 2):(i + 1) * (D_HEAD // 2)], zk[i * D_HEAD:(i + 1) * D_HEAD])
                             for i in range(C // D_HEAD)]
                        width = D_HEAD
                    full = jnp.concatenate([jnp.concatenate([pi] * (C // width), axis=1) for pi in p], axis=0)
                    lvl_u = upper_rows(level_ref[...], m)
                    a = put_upper_rows(a, jnp.where(lvl_u == lev, full.astype(BF16), upper_rows(a, m)), m)
            o = _mm(a, v)
            eb = jnp.exp(b)
            qe = q * eb
            kh = k * jnp.exp(block_rows(b, seq_rows, seq_rows - 1) - b)
            if sample:
                o_scr[rs, col(0, h)] = o
                qe_scr[rs, hs] = qe
                kh_scr[rs, hs] = kh
                e_all = block_rows(eb, seq_rows, seq_rows - 1)
                e_hi = e_all.astype(BF16).astype(F32)
                e_mid = (e_all - e_hi).astype(BF16).astype(F32)
                e_lo = e_all - e_hi - e_mid
                pos = row_id & (seq_rows - 1)
                ex_scr[rs, hs] = jnp.where(pos == 0, e_hi, jnp.where(pos == 1, e_mid, jnp.where(pos == 2, e_lo, 0.0)))
            else:
                st = sa_scr[h]
                o = o + _mm_nt(qe.astype(BF16), st.astype(BF16))
                sa_scr[h] = st * eb[C - 1:C, :] + _mm_tn(v, kh.astype(BF16))
                gate = proj_scr[rs, col(3, h)]
                o_scr[rs, col(0, h)] = _rmsnorm(o, na_ref[...]) * (gate * _sigmoid(gate))

        cos, sin = cos_ref[rs, :], sin_ref[rs, :]
        for h in range(N_HEADS):
            hs = head(h)
            q = proj_scr[rs, col(4, h)]
            k = proj_scr[rs, col(5, h)]
            v = proj_scr[rs, col(6, h)].astype(BF16)
            qr = q * cos + pltpu.roll(q, D_HEAD // 2, 1) * sin
            kr = (k * cos + pltpu.roll(k, D_HEAD // 2, 1) * sin) * (D_HEAD ** -0.5)
            a = _mm_nt(qr.astype(BF16), kr.astype(BF16)) * dec_ref[h]
            o = _mm(a.astype(BF16), v)
            qi = qr * inner_ref[h]
            ks = kr * sdec_ref[h]
            if sample:
                o_scr[rs, col(1, h)] = o
                qi_scr[rs, hs] = qi
                ks_scr[rs, hs] = ks
            else:
                st = sb_scr[h]
                o = o + _mm(qi.astype(BF16), st.astype(BF16))
                sb_scr[h] = cdec[h] * st + _mm_tn(ks.astype(BF16), v)
                gate = proj_scr[rs, col(7, h)]
                o_scr[rs, col(1, h)] = _groupnorm(o, nb_ref[...]) * (gate * _sigmoid(gate))

        if sample:
            for slot, group in enumerate(KEPT_GROUPS):
                keep_ref[:, slot * GROUP_W:(slot + 1) * GROUP_W] = proj_scr[rs, group * GROUP_W:(group + 1) * GROUP_W]

    def kept(group, h):
        slot = KEPT_GROUPS.index(group)
        return slice(slot * GROUP_W + h * D_HEAD, slot * GROUP_W + (h + 1) * D_HEAD)

    def apply_states():
        zeros8 = jnp.zeros((SUBLANES, D_HEAD), F32)
        sel8 = jnp.where(lax.broadcasted_iota(jnp.int32, (SUBLANES, D_HEAD), 0) < 3, 1.0, 0.0)

        def pair_readout(lhs_ref, st_ref, j, rows, group, h):
            lhs = jnp.concatenate([lhs_ref[rows, head(h)], lhs_ref[rows, head(h + 1)]], axis=0).astype(BF16)
            w = jnp.concatenate([st_ref[j, h], st_ref[j, h + 1]], axis=1).astype(BF16)
            oo = _mm(lhs, w)
            os_out_ref[rows, col(group, h)] = os_in_ref[rows, col(group, h)] + oo[:SUBLANES, :D_HEAD]
            os_out_ref[rows, col(group, h + 1)] = os_in_ref[rows, col(group, h + 1)] + oo[SUBLANES:, D_HEAD:]

        for j in range(smp_seq_per_step):
            rows = slice(j * SUBLANES, (j + 1) * SUBLANES)
            for h in range(0, N_HEADS, 2):
                pair_readout(qe_ref, sa_in_ref, j, rows, 0, h)
                pair_readout(qi_ref, sb_in_ref, j, rows, 1, h)
            for h in range(N_HEADS):
                hs = head(h)
                lhs = jnp.concatenate([kh_ref[rows, hs], ex_ref[rows, hs]], axis=0).astype(BF16)
                v = keep_in_ref[rows, kept(2, h)]
                rhs = jnp.concatenate([jnp.concatenate([v, zeros8], axis=1),
                                       jnp.concatenate([zeros8, sel8], axis=1)], axis=0).astype(BF16)
                upd = _mm_tn(lhs, rhs)
                sas_out_ref[j, h] = sa_in_ref[j, h] * upd[:, D_HEAD:] + upd[:, :D_HEAD]
                v = keep_in_ref[rows, kept(6, h)].astype(BF16)
                sbs_out_ref[j, h] = smp_cdec[h] * sb_in_ref[j, h] + _mm_tn(ks_ref[rows, hs].astype(BF16), v)

    chunk_rows = [slice(c * C, (c + 1) * C) for c in range(chunks)]
    if not sample:
        apply_states()
        for wide, narrow in zip(wide_refs, narrow_refs):
            narrow[...] = wide[...].astype(BF16)
    for rs in chunk_rows:
        project(rs)
    for rs in chunk_rows:
        score(rs)
    if not sample:
        x1_ref[...] = x_ref[...] + _mm(o_scr[...].astype(BF16), wout_ref[...])

        @pl.when(step == pl.num_programs(1) - 1)
        def _():
            for h in range(N_HEADS):
                sa_out_ref[0, h] = sa_scr[h].T
                sb_out_ref[0, h] = sb_scr[h]


def _full(shape):
    return pl.BlockSpec(shape, lambda *_: (0,) * len(shape))


def _row_block_spec(shape, n_steps, inner_steps):
    _, n_rows, width = shape
    share = next(k for k in (1, 2, 4, 8) if n_steps % k == 0 and n_rows % ((n_steps // k) * BF16_ROWS) == 0)
    return pl.BlockSpec((1, n_rows // (n_steps // share), width),
                        lambda b, i: (0, (b * inner_steps + i) // share, 0))


def _mix_scratch(tile):
    return [pltpu.VMEM((tile, D_MODEL), BF16), pltpu.VMEM((tile, IN_WIDTH), F32)]


def _smp_score_call(x2, weights, *, seq_len, tile):
    n_rows = x2.shape[0]
    assert n_rows % tile == 0 and tile % seq_len == 0
    nlev, cdec, consts = _chunk_consts(tile, seq_len)
    cos, sin = _rope_tables(np.tile(PAST_LEN + np.arange(seq_len), tile // seq_len))
    rows = lambda width: pl.BlockSpec((tile, width), lambda n: (n, 0))
    widths = [2 * GROUP_W] + [GROUP_W] * 5 + [len(KEPT_GROUPS) * GROUP_W]
    args = [x2, cos, sin, *weights, *consts]
    w_in, w_out = weights[1], weights[5]
    narrow_shapes = [w.shape[1:] for w in (w_in, w_out)]
    *outs, w_in_bf, w_out_bf = pl.pallas_call(
        functools.partial(_mix_kernel, sample=True, tile=tile, chunks=1, seq_rows=seq_len, nlev=nlev, cdec=cdec,
                          smp_cdec=None, smp_seq_per_step=0),
        grid=(n_rows // tile,),
        in_specs=[rows(D_MODEL)] + [_full(a.shape) for a in args[1:]],
        out_specs=[rows(w) for w in widths] + [_full(shape) for shape in narrow_shapes],
        out_shape=[jax.ShapeDtypeStruct((n_rows, w), F32) for w in widths]
        + [jax.ShapeDtypeStruct(shape, BF16) for shape in narrow_shapes],
        scratch_shapes=_mix_scratch(tile) + [pltpu.VMEM((tile, GROUP_W), F32)] * 2,
        compiler_params=pltpu.CompilerParams(dimension_semantics=("arbitrary",),
                                             vmem_limit_bytes=VMEM_LIMITS["score_sample"]),
        name="score_sample",
    )(*args)
    return cdec, outs, w_in_bf, w_out_bf


def _mix_call(x2, weights, to_narrow, smp, smp_states, smp_cdec, *, n_seq, seq_len, tile, chunks, smp_len):
    n_rows = n_seq * seq_len
    assert seq_len % tile == 0 and tile % chunks == 0
    chunk = tile // chunks
    steps = seq_len // tile
    n_steps = n_seq * steps
    nlev, cdec, consts = _chunk_consts(chunk, min(seq_len, chunk))
    cos, sin = _rope_tables(np.arange(seq_len))
    n_smp = smp_states[0].shape[0]
    assert n_smp % n_steps == 0
    seq_per_step = n_smp // n_steps
    smp_rows = seq_per_step * smp_len
    assert smp_rows % SUBLANES == 0

    row_map = lambda b, i: (b * steps + i, 0)
    pos_map = lambda b, i: (i, 0)
    state_spec = pl.BlockSpec((1, N_HEADS, D_HEAD, D_HEAD), lambda b, i: (b, 0, 0, 0))
    smp_state_spec = pl.BlockSpec((seq_per_step, N_HEADS, D_HEAD, D_HEAD), lambda b, i: (b * steps + i, 0, 0, 0))
    smp_spec = lambda a: pl.BlockSpec((smp_rows, a.shape[1]), row_map)
    narrow_specs = [_row_block_spec(w.shape, n_steps, steps) for w in to_narrow]
    smp_scores, *smp_factors, smp_keep = smp
    side_in = [*smp_factors, smp_keep, smp_scores]
    in_specs = ([pl.BlockSpec((tile, D_MODEL), row_map), pl.BlockSpec((tile, D_HEAD), pos_map),
                 pl.BlockSpec((tile, D_HEAD), pos_map)] + [_full(w.shape) for w in weights]
                + [_full(c.shape) for c in consts] + narrow_specs + [smp_spec(a) for a in side_in]
                + [smp_state_spec, smp_state_spec])
    args = [x2, cos, sin, *weights, *consts, *to_narrow, *side_in, *smp_states]
    state_shape = lambda n: jax.ShapeDtypeStruct((n, N_HEADS, D_HEAD, D_HEAD), F32)
    return pl.pallas_call(
        functools.partial(_mix_kernel, sample=False, tile=tile, chunks=chunks, seq_rows=min(seq_len, chunk),
                          nlev=nlev, cdec=cdec, smp_cdec=smp_cdec, smp_seq_per_step=seq_per_step),
        grid=(n_seq, steps),
        in_specs=in_specs,
        out_specs=[pl.BlockSpec((tile, D_MODEL), row_map), state_spec, state_spec] + narrow_specs
        + [smp_spec(smp_scores), smp_state_spec, smp_state_spec],
        out_shape=[jax.ShapeDtypeStruct((n_rows, D_MODEL), F32), state_shape(n_seq), state_shape(n_seq)]
        + [jax.ShapeDtypeStruct(w.shape, BF16) for w in to_narrow]
        + [jax.ShapeDtypeStruct(smp_scores.shape, F32), state_shape(n_smp), state_shape(n_smp)],
        scratch_shapes=_mix_scratch(tile) + [pltpu.VMEM((tile, 2 * GROUP_W), F32)]
        + [pltpu.VMEM((tile, GROUP_W), F32)] * 2 + [pltpu.VMEM((N_HEADS, D_HEAD, D_HEAD), F32)] * 2,
        compiler_params=pltpu.CompilerParams(dimension_semantics=("arbitrary", "arbitrary"),
                                             vmem_limit_bytes=VMEM_LIMITS["mix_prompt"]),
        name="mix_prompt",
    )(*args)


def _ffn_kernel(*refs, sample, groups, rows):
    x_ref, w2_ref, wup_ref, cw_ref, cb_ref, wdown_ref, wf_ref = refs[:7]
    refs = refs[7:]
    if sample:
        (hist_ref, os_ref, keep_ref, na_ref, nb_ref, wout_ref, y_ref, hist_out_ref, h_scr, act_scr, mixed_scr,
         xin) = refs
        for h in range(N_HEADS):
            for group, norm, w_ref in ((0, _rmsnorm, na_ref), (1, _groupnorm, nb_ref)):
                cols = slice(group * GROUP_W + h * D_HEAD, group * GROUP_W + (h + 1) * D_HEAD)
                slot = KEPT_GROUPS.index(4 * group + 3)
                gate = keep_ref[:, slot * GROUP_W + h * D_HEAD:slot * GROUP_W + (h + 1) * D_HEAD]
                mixed_scr[:, cols] = (norm(os_ref[:, cols], w_ref[...]) * (gate * _sigmoid(gate))).astype(BF16)
        xin[...] = x_ref[...] + _mm(mixed_scr[...], wout_ref[...])
    else:
        y_ref, hist_out_ref, h_scr, act_scr, tail_scr = refs
        xin = x_ref
        step = pl.program_id(1)

        @pl.when(step == 0)
        def _():
            tail_scr[...] = jnp.zeros_like(tail_scr)

    G, L, P = groups, rows, SUBLANES
    parts = 1 if sample else ROW_PARTS
    LP = L // parts
    row_id = lax.broadcasted_iota(jnp.int32, (G, P, FF_COLS), 1)

    def shifted(up, prev2, prev1):
        r1, r2 = pltpu.roll(up, 1, 1), pltpu.roll(up, 2, 1)
        top1 = jnp.where(row_id == 0, prev1, r1[:, :P])
        top2 = jnp.where(row_id == 0, prev2, jnp.where(row_id == 1, prev1, r2[:, :P]))
        if LP == P:
            return top1, top2
        return jnp.concatenate([top1, r1[:, P:]], axis=1), jnp.concatenate([top2, r2[:, P:]], axis=1)

    tails = {}
    for part in range(parts):
        rows_p = slice(part * G * LP, (part + 1) * G * LP)
        h_scr[rows_p, :] = _rmsnorm(xin[rows_p, :], w2_ref[...]).astype(BF16)
        for n in range(0, D_FF, FF_COLS):
            conv = []
            for cols in (slice(n, n + FF_COLS), slice(D_FF + n, D_FF + n + FF_COLS)):
                up = _mm(h_scr[rows_p, :], wup_ref[0, :, cols]).reshape(G, LP, FF_COLS)
                if sample:
                    prev2, prev1 = hist_ref[:, 0:1, cols], hist_ref[:, 1:2, cols]
                    hist_out_ref[:, :, cols] = up[:, LP - 2:, :]
                else:
                    if part == 0:
                        prev2, prev1 = tail_scr[:, P - 2:P - 1, cols], tail_scr[:, P - 1:P, cols]
                    else:
                        prev = tails[cols.start]
                        prev2, prev1 = prev[:, P - 2:P - 1, :], prev[:, P - 1:P, :]
                    tails[cols.start] = up[:, LP - P:, :]
                    if part == parts - 1:
                        tail_scr[:, :, cols] = up[:, LP - P:, :]
                sh1, sh2 = shifted(up, prev2, prev1)
                conv.append(cb_ref[:, cols] + cw_ref[0:1, cols] * sh2 + cw_ref[1:2, cols] * sh1 + cw_ref[2:3, cols] * up)
            u, g = (c.reshape(G * LP, FF_COLS).astype(BF16) for c in conv)
            one = jnp.ones((), BF16)
            act_scr[rows_p, n:n + FF_COLS] = (g * (one / (one + jnp.exp(-g)))) * u

        x2 = xin[rows_p, :] + _mm(act_scr[rows_p, :], wdown_ref[0])
        y_ref[rows_p, :] = _rmsnorm(x2, wf_ref[...])

    if not sample:
        @pl.when(step == pl.num_programs(1) - 1)
        def _():
            hist_out_ref[...] = tail_scr[:, P - 2:, :]


def _ffn_call(x, w2, w_up, cw, cb, w_down, wf, smp, *, n_seq, seq_len, tile):
    sample = smp is not None
    n_rows = n_seq * seq_len
    hist_shape = jax.ShapeDtypeStruct((n_seq, CONV_W - 1, FF2), F32)
    if sample:
        hist, scores, keep, na, nb, w_out = smp
        groups, rows = tile // seq_len, seq_len
        grid = (n_rows // tile,)
        row_map = lambda n: (n, 0)
        hist_spec = pl.BlockSpec((groups, CONV_W - 1, FF2), lambda n: (n, 0, 0))
        extra_in = [hist_spec, pl.BlockSpec((tile, scores.shape[1]), row_map),
                    pl.BlockSpec((tile, keep.shape[1]), row_map), _full(na.shape), _full(nb.shape), _full(w_out.shape)]
        extra_args = [hist, scores, keep, na, nb, w_out]
        scratch = [pltpu.VMEM((tile, 2 * GROUP_W), BF16), pltpu.VMEM((tile, D_MODEL), F32)]
    else:
        assert seq_len % tile == 0
        groups, rows = 1, tile
        steps = seq_len // tile
        grid = (n_seq, steps)
        row_map = lambda b, i: (b * steps + i, 0)
        hist_spec = pl.BlockSpec((1, CONV_W - 1, FF2), lambda b, i: (b, 0, 0))
        extra_in, extra_args = [], []
        scratch = [pltpu.VMEM((1, SUBLANES, FF2), F32)]
    in_specs = [pl.BlockSpec((tile, D_MODEL), row_map), _full(w2.shape), _full(w_up.shape), _full(cw.shape),
                _full(cb.shape), _full(w_down.shape), _full(wf.shape)] + extra_in
    args = [x, w2, w_up, cw, cb, w_down, wf] + extra_args
    name = "ffn_sample" if sample else "ffn_prompt"
    return pl.pallas_call(
        functools.partial(_ffn_kernel, sample=sample, groups=groups, rows=rows),
        grid=grid,
        in_specs=in_specs,
        out_specs=[pl.BlockSpec((tile, D_MODEL), row_map), hist_spec],
        out_shape=[jax.ShapeDtypeStruct((n_rows, D_MODEL), F32), hist_shape],
        scratch_shapes=[pltpu.VMEM((tile, D_MODEL), BF16), pltpu.VMEM((tile, D_FF), BF16)] + scratch,
        compiler_params=pltpu.CompilerParams(dimension_semantics=("arbitrary",) * len(grid),
                                             vmem_limit_bytes=VMEM_LIMITS[name]),
        name=name,
    )(*args)


def kernel(x_prompt, x_sample, state_hgrn, state_ret, state_conv, w_norm1, w_in, hgrn_lb, hgrn_norm_w, ret_norm_w,
           w_out, w_norm2, w_ffn_in, conv_w, conv_b, w_ffn_out, w_norm_f):
    assert w_in.shape == (1, D_MODEL, IN_WIDTH) and hgrn_lb.shape == (2, GROUP_W)
    n_seq, seq_len, _ = x_prompt.shape
    n_smp, smp_len, _ = x_sample.shape
    xs = x_sample.reshape(n_smp * smp_len, D_MODEL)

    smp_cdec, smp, w_in_bf, w_out_bf = _smp_score_call(
        xs, (w_norm1, w_in, hgrn_lb, hgrn_norm_w, ret_norm_w, w_out), seq_len=smp_len, tile=256)
    mix_w = (w_norm1, w_in_bf, hgrn_lb, hgrn_norm_w, ret_norm_w, w_out_bf)
    x1, ha_p, rb_p, w_up, w_down, smp_scores, ha_s, rb_s = _mix_call(
        x_prompt.reshape(n_seq * seq_len, D_MODEL), mix_w, (w_ffn_in, w_ffn_out), smp,
        (state_hgrn[0], state_ret[0]), smp_cdec, n_seq=n_seq, seq_len=seq_len, tile=512, chunks=2, smp_len=smp_len)
    ffn_w = (w_norm2, w_up, conv_w[0], conv_b, w_down, w_norm_f.reshape(1, D_MODEL))
    y_p, cv_p = _ffn_call(x1, *ffn_w, None, n_seq=n_seq, seq_len=seq_len, tile=512)
    y_s, cv_s = _ffn_call(xs, *ffn_w, (state_conv[0], smp_scores, smp[-1], hgrn_norm_w, ret_norm_w, w_out_bf),
                          n_seq=n_smp, seq_len=smp_len, tile=256)
    return (y_p.reshape(x_prompt.shape), y_s.reshape(x_sample.shape), ha_p[None], rb_p[None], cv_p[None],
            ha_s[None], rb_s[None], cv_s[None])
```

```python
import functools

import numpy as np
import jax
import jax.numpy as jnp
from jax import lax
from jax.experimental import pallas as pl
from jax.experimental.pallas import tpu as pltpu

F32 = jnp.float32
BF16 = jnp.bfloat16

D_MODEL = 1024
N_HEADS = 4
D_HEAD = 128
GROUP_W = N_HEADS * D_HEAD
IN_WIDTH = 8 * GROUP_W
D_FF = 2816
FF2 = 2 * D_FF
CONV_W = 3
PAST_LEN = 16384
ROPE_BASE = 10000.0
EPS = 1e-6
LOG2E = 1.4426950408889634

SUBLANES = 8
BF16_ROWS = 16
PROJ_COLS = 512
FF_COLS = 256
ROW_PARTS = 2
KEPT_GROUPS = (2, 3, 6, 7)
MIB = 1024 * 1024
VMEM_LIMITS = {"score_sample": 54 * MIB, "mix_prompt": 52 * MIB, "ffn_prompt": 32 * MIB, "ffn_sample": 44 * MIB}


def _mm(a, b):
    return jnp.dot(a, b, preferred_element_type=F32)


def _mm_nt(a, b):
    return lax.dot_general(a, b, (((1,), (1,)), ((), ())), preferred_element_type=F32)


def _mm_tn(a, b):
    return lax.dot_general(a, b, (((0,), (0,)), ((), ())), preferred_element_type=F32)


def _sigmoid(x):
    return 1.0 / (1.0 + jnp.exp(-x))


def _rmsnorm(x, w):
    return x * lax.rsqrt(jnp.mean(x * x, axis=-1, keepdims=True) + EPS) * w


def _groupnorm(x, w):
    xc = x - jnp.mean(x, axis=-1, keepdims=True)
    return xc * lax.rsqrt(jnp.mean(xc * xc, axis=-1, keepdims=True) + EPS) * w


def _chunk_consts(chunk, seq_len):
    nlev = int(np.log2(seq_len))
    assert 1 << nlev == seq_len and chunk % seq_len == 0
    r = np.arange(chunk)
    rr, cc = r[:, None], r[None, :]
    same_seq = (rr // seq_len) == (cc // seq_len)
    cum = (same_seq & (cc <= rr)).astype(np.float32)
    x = rr ^ cc
    bit_len = np.where(x > 0, np.floor(np.log2(np.maximum(x, 1))).astype(np.int64) + 1, 0)
    level = np.where(same_seq & (cc <= rr), bit_len, -1).astype(np.int32)

    pos = r % seq_len
    log_gamma = np.log1p(-np.exp2(-5.0 - np.arange(N_HEADS, dtype=np.float64)))[:, None, None]
    rel = (pos[:, None] - pos[None, :]).astype(np.float64)[None]
    causal = (same_seq & (cc <= rr))[None]
    dec = np.where(causal, np.exp(np.where(causal, rel, 0.0) * log_gamma), 0.0)
    ones = np.ones((1, 1, D_HEAD))
    inner = np.exp((pos + 1.0)[None, :, None] * log_gamma) * ones
    sdec = np.exp((seq_len - 1.0 - pos)[None, :, None] * log_gamma) * ones
    cdec = tuple(float(v) for v in np.exp(seq_len * log_gamma[:, 0, 0]))
    consts = (jnp.asarray(cum, BF16), jnp.asarray(level, BF16), jnp.asarray(dec, F32), jnp.asarray(inner, F32),
              jnp.asarray(sdec, F32))
    return nlev, cdec, consts


def _rope_tables(pos):
    half = D_HEAD // 2
    inv = 1.0 / (ROPE_BASE ** (np.arange(half, dtype=np.float64) / half))
    ang = np.asarray(pos, np.float64)[:, None] * inv[None, :]
    cos, sin = np.cos(ang), np.sin(ang)
    return (jnp.asarray(np.concatenate([cos, cos], axis=-1), F32),
            jnp.asarray(np.concatenate([-sin, sin], axis=-1), F32))


def _mix_kernel(*refs, sample, tile, chunks, seq_rows, nlev, cdec, smp_cdec, smp_seq_per_step):
    (x_ref, cos_ref, sin_ref, w1_ref, win_ref, lbp_ref, na_ref, nb_ref, wout_ref,
     cum_ref, level_ref, dec_ref, inner_ref, sdec_ref) = refs[:14]
    refs = refs[14:]
    if sample:
        o_scr, qe_scr, kh_scr, ex_scr, qi_scr, ks_scr, keep_ref, win_bf_ref, wout_bf_ref = refs[:9]
        h_scr, proj_scr, d_scr, k_scr = refs[9:]

        @pl.when(pl.program_id(0) == 0)
        def _():
            for n in range(0, IN_WIDTH, PROJ_COLS):
                win_bf_ref[:, n:n + PROJ_COLS] = win_ref[0, :, n:n + PROJ_COLS].astype(BF16)
            wout_bf_ref[...] = wout_ref[0].astype(BF16)

        win_ref = win_bf_ref
    else:
        wide_refs = refs[:2]
        qe_ref, kh_ref, ex_ref, qi_ref, ks_ref, keep_in_ref, os_in_ref, sa_in_ref, sb_in_ref = refs[2:11]
        x1_ref, sa_out_ref, sb_out_ref = refs[11:14]
        narrow_refs = refs[14:16]
        os_out_ref, sas_out_ref, sbs_out_ref = refs[16:19]
        h_scr, proj_scr, o_scr, d_scr, k_scr, sa_scr, sb_scr = refs[19:]
        step = pl.program_id(1)

        @pl.when(step == 0)
        def _():
            sa_scr[...] = jnp.zeros_like(sa_scr)
            sb_scr[...] = jnp.zeros_like(sb_scr)

    C = tile // chunks

    def col(group, h):
        return slice(group * GROUP_W + h * D_HEAD, group * GROUP_W + (h + 1) * D_HEAD)

    def head(h):
        return slice(h * D_HEAD, (h + 1) * D_HEAD)

    def block_rows(x, m, row):
        x3 = x.reshape(C // m, m, D_HEAD)
        return jnp.broadcast_to(x3[:, row:row + 1, :], x3.shape).reshape(C, D_HEAD)

    def upper_rows(x, m):
        return x.reshape(C // m, 2, m // 2, x.shape[-1])[:, 1].reshape(C // 2, x.shape[-1])

    def put_upper_rows(x, xu, m):
        x4 = x.reshape(C // m, 2, m // 2, x.shape[-1])
        xu4 = xu.reshape(C // m, 1, m // 2, x.shape[-1])
        return jnp.concatenate([x4[:, 0:1], xu4], axis=1).reshape(C, x.shape[-1])

    def project(rs):
        h_scr[rs, :] = _rmsnorm(x_ref[rs, :], w1_ref[...]).astype(BF16)
        for n in range(0, IN_WIDTH, PROJ_COLS):
            proj_scr[rs, n:n + PROJ_COLS] = _mm(h_scr[rs, :], win_ref[:, n:n + PROJ_COLS])

    def score(rs):
        lb0, lb1 = lbp_ref[0:1, :], lbp_ref[1:2, :]
        lb_max = jnp.maximum(lb0, lb1)
        e0, e1 = jnp.exp(lb0 - lb_max), jnp.exp(lb1 - lb_max)
        lb = e0 / (e0 + e1)

        row_id = lax.broadcasted_iota(jnp.int32, (C, D_HEAD), 0)

        f = lb + (1.0 - lb) * _sigmoid(proj_scr[rs, GROUP_W:2 * GROUP_W])
        k_scr[rs, :] = 1.0 - f
        g = jnp.log(f)
        g_hi = g.astype(BF16)
        g_lo = (g - g_hi.astype(F32)).astype(BF16)
        d_scr[rs, :] = _mm(cum_ref[...], g_hi) + _mm(cum_ref[...], g_lo)

        signs = [jnp.where((row_id & (1 << (lev - 1))) != 0, LOG2E, -LOG2E) for lev in range(3, nlev + 1)]

        for h in range(N_HEADS):
            hs = head(h)
            q = proj_scr[rs, col(0, h)]
            k = k_scr[rs, hs]
            v = proj_scr[rs, col(2, h)].astype(BF16)
            b = d_scr[rs, hs]
            a = jnp.where(level_ref[...] == 0, _mm_nt(q.astype(BF16), k.astype(BF16)).astype(BF16), 0.0)
            for lev in range(1, nlev + 1):
                m = 1 << lev
                upper = (row_id & (m // 2)) != 0
                if lev == 1:
                    z = jnp.where(upper, q * (1.0 - k), k)
                elif lev == 2:
                    fh = 1.0 - k
                    pos4 = row_id & 3
                    decay = jnp.where(pos4 == 0, pltpu.roll(fh, C - 1, 0),
                                      jnp.where(pos4 == 1, 1.0, jnp.where(pos4 == 2, fh, fh * pltpu.roll(fh, 1, 0))))
                    z = jnp.where(upper, q, k) * decay
                else:
                    z = jnp.where(upper, q, k) * jnp.exp2((b - block_rows(b, m, m // 2 - 1)) * signs[lev - 3])
                if m < 2 * BF16_ROWS:
                    z = z.astype(BF16)
                    a = jnp.where(level_ref[...] == lev, _mm_nt(z, z).astype(BF16), a)
                else:
                    zq = upper_rows(z, m).astype(BF16)
                    zk = z.astype(BF16)
                    if m > D_HEAD:
                        p = [_mm_nt(zq[i * (m // 2):(i + 1) * (m // 2)], zk[i * m:i * m + m // 2])
                             for i in range(C // m)]
                        width = m // 2
                    else:
                        p = [_mm_nt(zq[i * (D_HEAD // 2):(i + 1) * (D_HEAD // 2)], zk[i * D_HEAD:(i + 1) * D_HEAD])
                             for i in range(C // D_HEAD)]
                        width = D_HEAD
                    full = jnp.concatenate([jnp.concatenate([pi] * (C // width), axis=1) for pi in p], axis=0)
                    lvl_u = upper_rows(level_ref[...], m)
                    a = put_upper_rows(a, jnp.where(lvl_u == lev, full.astype(BF16), upper_rows(a, m)), m)
            o = _mm(a, v)
            eb = jnp.exp(b)
            qe = q * eb
            kh = k * jnp.exp(block_rows(b, seq_rows, seq_rows - 1) - b)
            if sample:
                o_scr[rs, col(0, h)] = o
                qe_scr[rs, hs] = qe
                kh_scr[rs, hs] = kh
                e_all = block_rows(eb, seq_rows, seq_rows - 1)
                e_hi = e_all.astype(BF16).astype(F32)
                e_mid = (e_all - e_hi).astype(BF16).astype(F32)
                e_lo = e_all - e_hi - e_mid
                pos = row_id & (seq_rows - 1)
                ex_scr[rs, hs] = jnp.where(pos == 0, e_hi, jnp.where(pos == 1, e_mid, jnp.where(pos == 2, e_lo, 0.0)))
            else:
                st = sa_scr[h]
                o = o + _mm_nt(qe.astype(BF16), st.astype(BF16))
                sa_scr[h] = st * eb[C - 1:C, :] + _mm_tn(v, kh.astype(BF16))
                gate = proj_scr[rs, col(3, h)]
                o_scr[rs, col(0, h)] = _rmsnorm(o, na_ref[...]) * (gate * _sigmoid(gate))

        cos, sin = cos_ref[rs, :], sin_ref[rs, :]
        for h in range(N_HEADS):
            hs = head(h)
            q = proj_scr[rs, col(4, h)]
            k = proj_scr[rs, col(5, h)]
            v = proj_scr[rs, col(6, h)].astype(BF16)
            qr = q * cos + pltpu.roll(q, D_HEAD // 2, 1) * sin
            kr = (k * cos + pltpu.roll(k, D_HEAD // 2, 1) * sin) * (D_HEAD ** -0.5)
            qi = qr * inner_ref[h]
            ks = kr * sdec_ref[h]
            a = (_mm_nt(qi.astype(BF16), ks.astype(BF16)) * (1.0 / cdec[h])).astype(BF16)
            o = _mm(jnp.where(level_ref[...] >= 0, a, 0.0), v)
            if sample:
                o_scr[rs, col(1, h)] = o
                qi_scr[rs, hs] = qi
                ks_scr[rs, hs] = ks
            else:
                st = sb_scr[h]
                o = o + _mm(qi.astype(BF16), st.astype(BF16))
                sb_scr[h] = cdec[h] * st + _mm_tn(ks.astype(BF16), v)
                gate = proj_scr[rs, col(7, h)]
                o_scr[rs, col(1, h)] = _groupnorm(o, nb_ref[...]) * (gate * _sigmoid(gate))

        if sample:
            for slot, group in enumerate(KEPT_GROUPS):
                keep_ref[:, slot * GROUP_W:(slot + 1) * GROUP_W] = proj_scr[rs, group * GROUP_W:(group + 1) * GROUP_W]

    def kept(group, h):
        slot = KEPT_GROUPS.index(group)
        return slice(slot * GROUP_W + h * D_HEAD, slot * GROUP_W + (h + 1) * D_HEAD)

    def apply_states():
        zeros8 = jnp.zeros((SUBLANES, D_HEAD), F32)
        sel8 = jnp.where(lax.broadcasted_iota(jnp.int32, (SUBLANES, D_HEAD), 0) < 3, 1.0, 0.0)

        def pair_readout(lhs_ref, st_ref, j, rows, group, h):
            lhs = jnp.concatenate([lhs_ref[rows, head(h)], lhs_ref[rows, head(h + 1)]], axis=0).astype(BF16)
            w = jnp.concatenate([st_ref[j, h], st_ref[j, h + 1]], axis=1).astype(BF16)
            oo = _mm(lhs, w)
            os_out_ref[rows, col(group, h)] = os_in_ref[rows, col(group, h)] + oo[:SUBLANES, :D_HEAD]
            os_out_ref[rows, col(group, h + 1)] = os_in_ref[rows, col(group, h + 1)] + oo[SUBLANES:, D_HEAD:]

        for j in range(smp_seq_per_step):
            rows = slice(j * SUBLANES, (j + 1) * SUBLANES)
            for h in range(0, N_HEADS, 2):
                pair_readout(qe_ref, sa_in_ref, j, rows, 0, h)
                pair_readout(qi_ref, sb_in_ref, j, rows, 1, h)
            for h in range(N_HEADS):
                hs = head(h)
                lhs = jnp.concatenate([kh_ref[rows, hs], ex_ref[rows, hs]], axis=0).astype(BF16)
                v = keep_in_ref[rows, kept(2, h)]
                rhs = jnp.concatenate([jnp.concatenate([v, zeros8], axis=1),
                                       jnp.concatenate([zeros8, sel8], axis=1)], axis=0).astype(BF16)
                upd = _mm_tn(lhs, rhs)
                sas_out_ref[j, h] = sa_in_ref[j, h] * upd[:, D_HEAD:] + upd[:, :D_HEAD]
                v = keep_in_ref[rows, kept(6, h)].astype(BF16)
                sbs_out_ref[j, h] = smp_cdec[h] * sb_in_ref[j, h] + _mm_tn(ks_ref[rows, hs].astype(BF16), v)

    chunk_rows = [slice(c * C, (c + 1) * C) for c in range(chunks)]
    if not sample:
        apply_states()
        for wide, narrow in zip(wide_refs, narrow_refs):
            narrow[...] = wide[...].astype(BF16)
    for rs in chunk_rows:
        project(rs)
    for rs in chunk_rows:
        score(rs)
    if not sample:
        x1_ref[...] = x_ref[...] + _mm(o_scr[...].astype(BF16), wout_ref[...])

        @pl.when(step == pl.num_programs(1) - 1)
        def _():
            for h in range(N_HEADS):
                sa_out_ref[0, h] = sa_scr[h].T
                sb_out_ref[0, h] = sb_scr[h]


def _full(shape):
    return pl.BlockSpec(shape, lambda *_: (0,) * len(shape))


def _row_block_spec(shape, n_steps, inner_steps):
    _, n_rows, width = shape
    share = next(k for k in (1, 2, 4, 8) if n_steps % k == 0 and n_rows % ((n_steps // k) * BF16_ROWS) == 0)
    return pl.BlockSpec((1, n_rows // (n_steps // share), width),
                        lambda b, i: (0, (b * inner_steps + i) // share, 0))


def _mix_scratch(tile):
    return [pltpu.VMEM((tile, D_MODEL), BF16), pltpu.VMEM((tile, IN_WIDTH), F32)]


def _smp_score_call(x2, weights, *, seq_len, tile):
    n_rows = x2.shape[0]
    assert n_rows % tile == 0 and tile % seq_len == 0
    nlev, cdec, consts = _chunk_consts(tile, seq_len)
    cos, sin = _rope_tables(np.tile(PAST_LEN + np.arange(seq_len), tile // seq_len))
    rows = lambda width: pl.BlockSpec((tile, width), lambda n: (n, 0))
    widths = [2 * GROUP_W] + [GROUP_W] * 5 + [len(KEPT_GROUPS) * GROUP_W]
    args = [x2, cos, sin, *weights, *consts]
    w_in, w_out = weights[1], weights[5]
    narrow_shapes = [w.shape[1:] for w in (w_in, w_out)]
    *outs, w_in_bf, w_out_bf = pl.pallas_call(
        functools.partial(_mix_kernel, sample=True, tile=tile, chunks=1, seq_rows=seq_len, nlev=nlev, cdec=cdec,
                          smp_cdec=None, smp_seq_per_step=0),
        grid=(n_rows // tile,),
        in_specs=[rows(D_MODEL)] + [_full(a.shape) for a in args[1:]],
        out_specs=[rows(w) for w in widths] + [_full(shape) for shape in narrow_shapes],
        out_shape=[jax.ShapeDtypeStruct((n_rows, w), F32) for w in widths]
        + [jax.ShapeDtypeStruct(shape, BF16) for shape in narrow_shapes],
        scratch_shapes=_mix_scratch(tile) + [pltpu.VMEM((tile, GROUP_W), F32)] * 2,
        compiler_params=pltpu.CompilerParams(dimension_semantics=("arbitrary",),
                                             vmem_limit_bytes=VMEM_LIMITS["score_sample"]),
        name="score_sample",
    )(*args)
    return cdec, outs, w_in_bf, w_out_bf


def _mix_call(x2, weights, to_narrow, smp, smp_states, smp_cdec, *, n_seq, seq_len, tile, chunks, smp_len):
    n_rows = n_seq * seq_len
    assert seq_len % tile == 0 and tile % chunks == 0
    chunk = tile // chunks
    steps = seq_len // tile
    n_steps = n_seq * steps
    nlev, cdec, consts = _chunk_consts(chunk, min(seq_len, chunk))
    cos, sin = _rope_tables(np.arange(seq_len))
    n_smp = smp_states[0].shape[0]
    assert n_smp % n_steps == 0
    seq_per_step = n_smp // n_steps
    smp_rows = seq_per_step * smp_len
    assert smp_rows % SUBLANES == 0

    row_map = lambda b, i: (b * steps + i, 0)
    pos_map = lambda b, i: (i, 0)
    state_spec = pl.BlockSpec((1, N_HEADS, D_HEAD, D_HEAD), lambda b, i: (b, 0, 0, 0))
    smp_state_spec = pl.BlockSpec((seq_per_step, N_HEADS, D_HEAD, D_HEAD), lambda b, i: (b * steps + i, 0, 0, 0))
    smp_spec = lambda a: pl.BlockSpec((smp_rows, a.shape[1]), row_map)
    narrow_specs = [_row_block_spec(w.shape, n_steps, steps) for w in to_narrow]
    smp_scores, *smp_factors, smp_keep = smp
    side_in = [*smp_factors, smp_keep, smp_scores]
    in_specs = ([pl.BlockSpec((tile, D_MODEL), row_map), pl.BlockSpec((tile, D_HEAD), pos_map),
                 pl.BlockSpec((tile, D_HEAD), pos_map)] + [_full(w.shape) for w in weights]
                + [_full(c.shape) for c in consts] + narrow_specs + [smp_spec(a) for a in side_in]
                + [smp_state_spec, smp_state_spec])
    args = [x2, cos, sin, *weights, *consts, *to_narrow, *side_in, *smp_states]
    state_shape = lambda n: jax.ShapeDtypeStruct((n, N_HEADS, D_HEAD, D_HEAD), F32)
    return pl.pallas_call(
        functools.partial(_mix_kernel, sample=False, tile=tile, chunks=chunks, seq_rows=min(seq_len, chunk),
                          nlev=nlev, cdec=cdec, smp_cdec=smp_cdec, smp_seq_per_step=seq_per_step),
        grid=(n_seq, steps),
        in_specs=in_specs,
        out_specs=[pl.BlockSpec((tile, D_MODEL), row_map), state_spec, state_spec] + narrow_specs
        + [smp_spec(smp_scores), smp_state_spec, smp_state_spec],
        out_shape=[jax.ShapeDtypeStruct((n_rows, D_MODEL), F32), state_shape(n_seq), state_shape(n_seq)]
        + [jax.ShapeDtypeStruct(w.shape, BF16) for w in to_narrow]
        + [jax.ShapeDtypeStruct(smp_scores.shape, F32), state_shape(n_smp), state_shape(n_smp)],
        scratch_shapes=_mix_scratch(tile) + [pltpu.VMEM((tile, 2 * GROUP_W), F32)]
        + [pltpu.VMEM((tile, GROUP_W), F32)] * 2 + [pltpu.VMEM((N_HEADS, D_HEAD, D_HEAD), F32)] * 2,
        compiler_params=pltpu.CompilerParams(dimension_semantics=("arbitrary", "arbitrary"),
                                             vmem_limit_bytes=VMEM_LIMITS["mix_prompt"]),
        name="mix_prompt",
    )(*args)


def _ffn_kernel(*refs, sample, groups, rows):
    x_ref, w2_ref, wup_ref, cw_ref, cb_ref, wdown_ref, wf_ref = refs[:7]
    refs = refs[7:]
    if sample:
        (hist_ref, os_ref, keep_ref, na_ref, nb_ref, wout_ref, y_ref, hist_out_ref, h_scr, act_scr, mixed_scr,
         xin) = refs
        for h in range(N_HEADS):
            for group, norm, w_ref in ((0, _rmsnorm, na_ref), (1, _groupnorm, nb_ref)):
                cols = slice(group * GROUP_W + h * D_HEAD, group * GROUP_W + (h + 1) * D_HEAD)
                slot = KEPT_GROUPS.index(4 * group + 3)
                gate = keep_ref[:, slot * GROUP_W + h * D_HEAD:slot * GROUP_W + (h + 1) * D_HEAD]
                mixed_scr[:, cols] = (norm(os_ref[:, cols], w_ref[...]) * (gate * _sigmoid(gate))).astype(BF16)
        xin[...] = x_ref[...] + _mm(mixed_scr[...], wout_ref[...])
    else:
        y_ref, hist_out_ref, h_scr, act_scr, tail_scr = refs
        xin = x_ref
        step = pl.program_id(1)

        @pl.when(step == 0)
        def _():
            tail_scr[...] = jnp.zeros_like(tail_scr)

    G, L, P = groups, rows, SUBLANES
    parts = 1 if sample else ROW_PARTS
    LP = L // parts
    row_id = lax.broadcasted_iota(jnp.int32, (G, P, FF_COLS), 1)

    def shifted(up, prev2, prev1):
        r1, r2 = pltpu.roll(up, 1, 1), pltpu.roll(up, 2, 1)
        top1 = jnp.where(row_id == 0, prev1, r1[:, :P])
        top2 = jnp.where(row_id == 0, prev2, jnp.where(row_id == 1, prev1, r2[:, :P]))
        if LP == P:
            return top1, top2
        return jnp.concatenate([top1, r1[:, P:]], axis=1), jnp.concatenate([top2, r2[:, P:]], axis=1)

    tails = {}
    for part in range(parts):
        rows_p = slice(part * G * LP, (part + 1) * G * LP)
        h_scr[rows_p, :] = _rmsnorm(xin[rows_p, :], w2_ref[...]).astype(BF16)
        for n in range(0, D_FF, FF_COLS):
            conv = []
            for cols in (slice(n, n + FF_COLS), slice(D_FF + n, D_FF + n + FF_COLS)):
                up = _mm(h_scr[rows_p, :], wup_ref[0, :, cols]).reshape(G, LP, FF_COLS)
                if sample:
                    prev2, prev1 = hist_ref[:, 0:1, cols], hist_ref[:, 1:2, cols]
                    hist_out_ref[:, :, cols] = up[:, LP - 2:, :]
                else:
                    if part == 0:
                        prev2, prev1 = tail_scr[:, P - 2:P - 1, cols], tail_scr[:, P - 1:P, cols]
                    else:
                        prev = tails[cols.start]
                        prev2, prev1 = prev[:, P - 2:P - 1, :], prev[:, P - 1:P, :]
                    tails[cols.start] = up[:, LP - P:, :]
                    if part == parts - 1:
                        tail_scr[:, :, cols] = up[:, LP - P:, :]
                sh1, sh2 = shifted(up, prev2, prev1)
                conv.append(cb_ref[:, cols] + cw_ref[0:1, cols] * sh2 + cw_ref[1:2, cols] * sh1 + cw_ref[2:3, cols] * up)
            u, g = (c.reshape(G * LP, FF_COLS).astype(BF16) for c in conv)
            one = jnp.ones((), BF16)
            act_scr[rows_p, n:n + FF_COLS] = (g * (one / (one + jnp.exp(-g)))) * u

        x2 = xin[rows_p, :] + _mm(act_scr[rows_p, :], wdown_ref[0])
        y_ref[rows_p, :] = _rmsnorm(x2, wf_ref[...])

    if not sample:
        @pl.when(step == pl.num_programs(1) - 1)
        def _():
            hist_out_ref[...] = tail_scr[:, P - 2:, :]


def _ffn_call(x, w2, w_up, cw, cb, w_down, wf, smp, *, n_seq, seq_len, tile):
    sample = smp is not None
    n_rows = n_seq * seq_len
    hist_shape = jax.ShapeDtypeStruct((n_seq, CONV_W - 1, FF2), F32)
    if sample:
        hist, scores, keep, na, nb, w_out = smp
        groups, rows = tile // seq_len, seq_len
        grid = (n_rows // tile,)
        row_map = lambda n: (n, 0)
        hist_spec = pl.BlockSpec((groups, CONV_W - 1, FF2), lambda n: (n, 0, 0))
        extra_in = [hist_spec, pl.BlockSpec((tile, scores.shape[1]), row_map),
                    pl.BlockSpec((tile, keep.shape[1]), row_map), _full(na.shape), _full(nb.shape), _full(w_out.shape)]
        extra_args = [hist, scores, keep, na, nb, w_out]
        scratch = [pltpu.VMEM((tile, 2 * GROUP_W), BF16), pltpu.VMEM((tile, D_MODEL), F32)]
    else:
        assert seq_len % tile == 0
        groups, rows = 1, tile
        steps = seq_len // tile
        grid = (n_seq, steps)
        row_map = lambda b, i: (b * steps + i, 0)
        hist_spec = pl.BlockSpec((1, CONV_W - 1, FF2), lambda b, i: (b, 0, 0))
        extra_in, extra_args = [], []
        scratch = [pltpu.VMEM((1, SUBLANES, FF2), F32)]
    in_specs = [pl.BlockSpec((tile, D_MODEL), row_map), _full(w2.shape), _full(w_up.shape), _full(cw.shape),
                _full(cb.shape), _full(w_down.shape), _full(wf.shape)] + extra_in
    args = [x, w2, w_up, cw, cb, w_down, wf] + extra_args
    name = "ffn_sample" if sample else "ffn_prompt"
    return pl.pallas_call(
        functools.partial(_ffn_kernel, sample=sample, groups=groups, rows=rows),
        grid=grid,
        in_specs=in_specs,
        out_specs=[pl.BlockSpec((tile, D_MODEL), row_map), hist_spec],
        out_shape=[jax.ShapeDtypeStruct((n_rows, D_MODEL), F32), hist_shape],
        scratch_shapes=[pltpu.VMEM((tile, D_MODEL), BF16), pltpu.VMEM((tile, D_FF), BF16)] + scratch,
        compiler_params=pltpu.CompilerParams(dimension_semantics=("arbitrary",) * len(grid),
                                             vmem_limit_bytes=VMEM_LIMITS[name]),
        name=name,
    )(*args)


def kernel(x_prompt, x_sample, state_hgrn, state_ret, state_conv, w_norm1, w_in, hgrn_lb, hgrn_norm_w, ret_norm_w,
           w_out, w_norm2, w_ffn_in, conv_w, conv_b, w_ffn_out, w_norm_f):
    assert w_in.shape == (1, D_MODEL, IN_WIDTH) and hgrn_lb.shape == (2, GROUP_W)
    n_seq, seq_len, _ = x_prompt.shape
    n_smp, smp_len, _ = x_sample.shape
    xs = x_sample.reshape(n_smp * smp_len, D_MODEL)

    smp_cdec, smp, w_in_bf, w_out_bf = _smp_score_call(
        xs, (w_norm1, w_in, hgrn_lb, hgrn_norm_w, ret_norm_w, w_out), seq_len=smp_len, tile=256)
    mix_w = (w_norm1, w_in_bf, hgrn_lb, hgrn_norm_w, ret_norm_w, w_out_bf)
    x1, ha_p, rb_p, w_up, w_down, smp_scores, ha_s, rb_s = _mix_call(
        x_prompt.reshape(n_seq * seq_len, D_MODEL), mix_w, (w_ffn_in, w_ffn_out), smp,
        (state_hgrn[0], state_ret[0]), smp_cdec, n_seq=n_seq, seq_len=seq_len, tile=512, chunks=2, smp_len=smp_len)
    ffn_w = (w_norm2, w_up, conv_w[0], conv_b, w_down, w_norm_f.reshape(1, D_MODEL))
    y_p, cv_p = _ffn_call(x1, *ffn_w, None, n_seq=n_seq, seq_len=seq_len, tile=512)
    y_s, cv_s = _ffn_call(xs, *ffn_w, (state_conv[0], smp_scores, smp[-1], hgrn_norm_w, ret_norm_w, w_out_bf),
                          n_seq=n_smp, seq_len=smp_len, tile=256)
    return (y_p.reshape(x_prompt.shape), y_s.reshape(x_sample.shape), ha_p[None], rb_p[None], cv_p[None],
            ha_s[None], rb_s[None], cv_s[None])
```

```python
import functools

import numpy as np
import jax
import jax.numpy as jnp
from jax import lax
from jax.experimental import pallas as pl
from jax.experimental.pallas import tpu as pltpu

F32 = jnp.float32
BF16 = jnp.bfloat16

D_MODEL = 1024
N_HEADS = 4
D_HEAD = 128
GROUP_W = N_HEADS * D_HEAD
IN_WIDTH = 8 * GROUP_W
D_FF = 2816
FF2 = 2 * D_FF
CONV_W = 3
PAST_LEN = 16384
ROPE_BASE = 10000.0
EPS = 1e-6
LOG2E = 1.4426950408889634

SUBLANES = 8
BF16_ROWS = 16
PROJ_COLS = 512
FF_COLS = 256
ROW_PARTS = 2
KEPT_GROUPS = (2, 3, 6, 7)
MIB = 1024 * 1024
VMEM_LIMITS = {"score_sample": 54 * MIB, "mix_prompt": 52 * MIB, "ffn_prompt": 32 * MIB, "ffn_sample": 44 * MIB}


def _mm(a, b):
    return jnp.dot(a, b, preferred_element_type=F32)


def _mm_nt(a, b):
    return lax.dot_general(a, b, (((1,), (1,)), ((), ())), preferred_element_type=F32)


def _mm_tn(a, b):
    return lax.dot_general(a, b, (((0,), (0,)), ((), ())), preferred_element_type=F32)


def _sigmoid(x):
    return 1.0 / (1.0 + jnp.exp(-x))


def _rmsnorm(x, w):
    return x * lax.rsqrt(jnp.mean(x * x, axis=-1, keepdims=True) + EPS) * w


def _groupnorm(x, w):
    xc = x - jnp.mean(x, axis=-1, keepdims=True)
    return xc * lax.rsqrt(jnp.mean(xc * xc, axis=-1, keepdims=True) + EPS) * w


def _chunk_consts(chunk, seq_len):
    nlev = int(np.log2(seq_len))
    assert 1 << nlev == seq_len and chunk % seq_len == 0
    r = np.arange(chunk)
    rr, cc = r[:, None], r[None, :]
    same_seq = (rr // seq_len) == (cc // seq_len)
    cum = (same_seq & (cc <= rr)).astype(np.float32)
    x = rr ^ cc
    bit_len = np.where(x > 0, np.floor(np.log2(np.maximum(x, 1))).astype(np.int64) + 1, 0)
    level = np.where(same_seq & (cc <= rr), bit_len, -1).astype(np.int32)

    pos = r % seq_len
    log_gamma = np.log1p(-np.exp2(-5.0 - np.arange(N_HEADS, dtype=np.float64)))[:, None, None]
    ones = np.ones((1, 1, D_HEAD))
    inner = np.exp((pos + 1.0)[None, :, None] * log_gamma) * ones
    sdec = np.exp((seq_len - 1.0 - pos)[None, :, None] * log_gamma) * ones
    cdec = tuple(float(v) for v in np.exp(seq_len * log_gamma[:, 0, 0]))
    consts = (jnp.asarray(cum, BF16), jnp.asarray(level, BF16), jnp.asarray(inner, F32), jnp.asarray(sdec, F32))
    return nlev, cdec, consts


def _rope_tables(pos):
    half = D_HEAD // 2
    inv = 1.0 / (ROPE_BASE ** (np.arange(half, dtype=np.float64) / half))
    ang = np.asarray(pos, np.float64)[:, None] * inv[None, :]
    cos, sin = np.cos(ang), np.sin(ang)
    return (jnp.asarray(np.concatenate([cos, cos], axis=-1), F32),
            jnp.asarray(np.concatenate([-sin, sin], axis=-1), F32))


def _mix_kernel(*refs, sample, tile, chunks, seq_rows, nlev, cdec, smp_cdec, smp_seq_per_step):
    (x_ref, cos_ref, sin_ref, w1_ref, win_ref, lbp_ref, na_ref, nb_ref, wout_ref,
     cum_ref, level_ref, inner_ref, sdec_ref) = refs[:13]
    refs = refs[13:]
    if sample:
        o_scr, qe_scr, kh_scr, ex_scr, qi_scr, ks_scr, keep_ref, win_bf_ref, wout_bf_ref = refs[:9]
        h_scr, proj_scr, d_scr, k_scr = refs[9:]

        @pl.when(pl.program_id(0) == 0)
        def _():
            for n in range(0, IN_WIDTH, PROJ_COLS):
                win_bf_ref[:, n:n + PROJ_COLS] = win_ref[0, :, n:n + PROJ_COLS].astype(BF16)
            wout_bf_ref[...] = wout_ref[0].astype(BF16)

        win_ref = win_bf_ref
    else:
        wide_refs = refs[:2]
        qe_ref, kh_ref, ex_ref, qi_ref, ks_ref, keep_in_ref, os_in_ref, sa_in_ref, sb_in_ref = refs[2:11]
        x1_ref, sa_out_ref, sb_out_ref = refs[11:14]
        narrow_refs = refs[14:16]
        os_out_ref, sas_out_ref, sbs_out_ref = refs[16:19]
        h_scr, proj_scr, o_scr, d_scr, k_scr, sa_scr, sb_scr = refs[19:]
        step = pl.program_id(1)

        @pl.when(step == 0)
        def _():
            sa_scr[...] = jnp.zeros_like(sa_scr)
            sb_scr[...] = jnp.zeros_like(sb_scr)

    C = tile // chunks

    def col(group, h):
        return slice(group * GROUP_W + h * D_HEAD, group * GROUP_W + (h + 1) * D_HEAD)

    def head(h):
        return slice(h * D_HEAD, (h + 1) * D_HEAD)

    def block_rows(x, m, row):
        x3 = x.reshape(C // m, m, D_HEAD)
        return jnp.broadcast_to(x3[:, row:row + 1, :], x3.shape).reshape(C, D_HEAD)

    def upper_rows(x, m):
        return x.reshape(C // m, 2, m // 2, x.shape[-1])[:, 1].reshape(C // 2, x.shape[-1])

    def put_upper_rows(x, xu, m):
        x4 = x.reshape(C // m, 2, m // 2, x.shape[-1])
        xu4 = xu.reshape(C // m, 1, m // 2, x.shape[-1])
        return jnp.concatenate([x4[:, 0:1], xu4], axis=1).reshape(C, x.shape[-1])

    def project(rs):
        h_scr[rs, :] = _rmsnorm(x_ref[rs, :], w1_ref[...]).astype(BF16)
        for n in range(0, IN_WIDTH, PROJ_COLS):
            proj_scr[rs, n:n + PROJ_COLS] = _mm(h_scr[rs, :], win_ref[:, n:n + PROJ_COLS])

    def score(rs):
        lb0, lb1 = lbp_ref[0:1, :], lbp_ref[1:2, :]
        lb_max = jnp.maximum(lb0, lb1)
        e0, e1 = jnp.exp(lb0 - lb_max), jnp.exp(lb1 - lb_max)
        lb = e0 / (e0 + e1)

        row_id = lax.broadcasted_iota(jnp.int32, (C, D_HEAD), 0)

        f = lb + (1.0 - lb) * _sigmoid(proj_scr[rs, GROUP_W:2 * GROUP_W])
        k_scr[rs, :] = 1.0 - f
        g = jnp.log(f)
        g_hi = g.astype(BF16)
        g_lo = (g - g_hi.astype(F32)).astype(BF16)
        d_scr[rs, :] = _mm(cum_ref[...], g_hi) + _mm(cum_ref[...], g_lo)

        signs = [jnp.where((row_id & (1 << (lev - 1))) != 0, LOG2E, -LOG2E) for lev in range(3, nlev + 1)]

        for h in range(N_HEADS):
            hs = head(h)
            q = proj_scr[rs, col(0, h)]
            k = k_scr[rs, hs]
            v = proj_scr[rs, col(2, h)].astype(BF16)
            b = d_scr[rs, hs]
            a = jnp.where(level_ref[...] == 0, _mm_nt(q.astype(BF16), k.astype(BF16)).astype(BF16), 0.0)
            for lev in range(1, nlev + 1):
                m = 1 << lev
                upper = (row_id & (m // 2)) != 0
                if lev == 1:
                    z = jnp.where(upper, q * (1.0 - k), k)
                elif lev == 2:
                    fh = 1.0 - k
                    pos4 = row_id & 3
                    decay = jnp.where(pos4 == 0, pltpu.roll(fh, C - 1, 0),
                                      jnp.where(pos4 == 1, 1.0, jnp.where(pos4 == 2, fh, fh * pltpu.roll(fh, 1, 0))))
                    z = jnp.where(upper, q, k) * decay
                else:
                    z = jnp.where(upper, q, k) * jnp.exp2((b - block_rows(b, m, m // 2 - 1)) * signs[lev - 3])
                if m < 2 * BF16_ROWS:
                    z = z.astype(BF16)
                    a = jnp.where(level_ref[...] == lev, _mm_nt(z, z).astype(BF16), a)
                else:
                    zq = upper_rows(z, m).astype(BF16)
                    zk = z.astype(BF16)
                    if m > D_HEAD:
                        p = [_mm_nt(zq[i * (m // 2):(i + 1) * (m // 2)], zk[i * m:i * m + m // 2])
                             for i in range(C // m)]
                        width = m // 2
                    else:
                        p = [_mm_nt(zq[i * (D_HEAD // 2):(i + 1) * (D_HEAD // 2)], zk[i * D_HEAD:(i + 1) * D_HEAD])
                             for i in range(C // D_HEAD)]
                        width = D_HEAD
                    full = jnp.concatenate([jnp.concatenate([pi] * (C // width), axis=1) for pi in p], axis=0)
                    lvl_u = upper_rows(level_ref[...], m)
                    a = put_upper_rows(a, jnp.where(lvl_u == lev, full.astype(BF16), upper_rows(a, m)), m)
            o = _mm(a, v)
            eb = jnp.exp(b)
            qe = q * eb
            kh = k * jnp.exp(block_rows(b, seq_rows, seq_rows - 1) - b)
            if sample:
                o_scr[rs, col(0, h)] = o
                qe_scr[rs, hs] = qe
                kh_scr[rs, hs] = kh
                e_all = block_rows(eb, seq_rows, seq_rows - 1)
                e_hi = e_all.astype(BF16).astype(F32)
                e_mid = (e_all - e_hi).astype(BF16).astype(F32)
                e_lo = e_all - e_hi - e_mid
                pos = row_id & (seq_rows - 1)
                ex_scr[rs, hs] = jnp.where(pos == 0, e_hi, jnp.where(pos == 1, e_mid, jnp.where(pos == 2, e_lo, 0.0)))
            else:
                st = sa_scr[h]
                o = o + _mm_nt(qe.astype(BF16), st.astype(BF16))
                sa_scr[h] = st * eb[C - 1:C, :] + _mm_tn(v, kh.astype(BF16))
                gate = proj_scr[rs, col(3, h)]
                o_scr[rs, col(0, h)] = _rmsnorm(o, na_ref[...]) * (gate * _sigmoid(gate))

        cos, sin = cos_ref[rs, :], sin_ref[rs, :]
        for h in range(N_HEADS):
            hs = head(h)
            q = proj_scr[rs, col(4, h)]
            k = proj_scr[rs, col(5, h)]
            v = proj_scr[rs, col(6, h)].astype(BF16)
            qr = q * cos + pltpu.roll(q, D_HEAD // 2, 1) * sin
            kr = (k * cos + pltpu.roll(k, D_HEAD // 2, 1) * sin) * (D_HEAD ** -0.5)
            qi = qr * inner_ref[h]
            ks = kr * sdec_ref[h]
            a = (_mm_nt(qi.astype(BF16), ks.astype(BF16)) * (1.0 / cdec[h])).astype(BF16)
            o = _mm(jnp.where(level_ref[...] >= 0, a, 0.0), v)
            if sample:
                o_scr[rs, col(1, h)] = o
                qi_scr[rs, hs] = qi
                ks_scr[rs, hs] = ks
            else:
                st = sb_scr[h]
                o = o + _mm(qi.astype(BF16), st.astype(BF16))
                sb_scr[h] = cdec[h] * st + _mm_tn(ks.astype(BF16), v)
                gate = proj_scr[rs, col(7, h)]
                o_scr[rs, col(1, h)] = _groupnorm(o, nb_ref[...]) * (gate * _sigmoid(gate))

        if sample:
            for slot, group in enumerate(KEPT_GROUPS):
                keep_ref[:, slot * GROUP_W:(slot + 1) * GROUP_W] = proj_scr[rs, group * GROUP_W:(group + 1) * GROUP_W]

    def kept(group, h):
        slot = KEPT_GROUPS.index(group)
        return slice(slot * GROUP_W + h * D_HEAD, slot * GROUP_W + (h + 1) * D_HEAD)

    def apply_states():
        zeros8 = jnp.zeros((SUBLANES, D_HEAD), F32)
        sel8 = jnp.where(lax.broadcasted_iota(jnp.int32, (SUBLANES, D_HEAD), 0) < 3, 1.0, 0.0)

        def pair_readout(lhs_ref, st_ref, j, rows, group, h):
            lhs = jnp.concatenate([lhs_ref[rows, head(h)], lhs_ref[rows, head(h + 1)]], axis=0).astype(BF16)
            w = jnp.concatenate([st_ref[j, h], st_ref[j, h + 1]], axis=1).astype(BF16)
            oo = _mm(lhs, w)
            os_out_ref[rows, col(group, h)] = os_in_ref[rows, col(group, h)] + oo[:SUBLANES, :D_HEAD]
            os_out_ref[rows, col(group, h + 1)] = os_in_ref[rows, col(group, h + 1)] + oo[SUBLANES:, D_HEAD:]

        for j in range(smp_seq_per_step):
            rows = slice(j * SUBLANES, (j + 1) * SUBLANES)
            for h in range(0, N_HEADS, 2):
                pair_readout(qe_ref, sa_in_ref, j, rows, 0, h)
                pair_readout(qi_ref, sb_in_ref, j, rows, 1, h)
            for h in range(N_HEADS):
                hs = head(h)
                lhs = jnp.concatenate([kh_ref[rows, hs], ex_ref[rows, hs]], axis=0).astype(BF16)
                v = keep_in_ref[rows, kept(2, h)]
                rhs = jnp.concatenate([jnp.concatenate([v, zeros8], axis=1),
                                       jnp.concatenate([zeros8, sel8], axis=1)], axis=0).astype(BF16)
                upd = _mm_tn(lhs, rhs)
                sas_out_ref[j, h] = sa_in_ref[j, h] * upd[:, D_HEAD:] + upd[:, :D_HEAD]
                v = keep_in_ref[rows, kept(6, h)].astype(BF16)
                sbs_out_ref[j, h] = smp_cdec[h] * sb_in_ref[j, h] + _mm_tn(ks_ref[rows, hs].astype(BF16), v)

    chunk_rows = [slice(c * C, (c + 1) * C) for c in range(chunks)]
    if not sample:
        apply_states()
        for wide, narrow in zip(wide_refs, narrow_refs):
            narrow[...] = wide[...].astype(BF16)
    for rs in chunk_rows:
        project(rs)
    for rs in chunk_rows:
        score(rs)
    if not sample:
        x1_ref[...] = x_ref[...] + _mm(o_scr[...].astype(BF16), wout_ref[...])

        @pl.when(step == pl.num_programs(1) - 1)
        def _():
            for h in range(N_HEADS):
                sa_out_ref[0, h] = sa_scr[h].T
                sb_out_ref[0, h] = sb_scr[h]


def _full(shape):
    return pl.BlockSpec(shape, lambda *_: (0,) * len(shape))


def _row_block_spec(shape, n_steps, inner_steps):
    _, n_rows, width = shape
    share = next(k for k in (1, 2, 4, 8) if n_steps % k == 0 and n_rows % ((n_steps // k) * BF16_ROWS) == 0)
    return pl.BlockSpec((1, n_rows // (n_steps // share), width),
                        lambda b, i: (0, (b * inner_steps + i) // share, 0))


def _mix_scratch(tile):
    return [pltpu.VMEM((tile, D_MODEL), BF16), pltpu.VMEM((tile, IN_WIDTH), F32)]


def _smp_score_call(x2, weights, *, seq_len, tile):
    n_rows = x2.shape[0]
    assert n_rows % tile == 0 and tile % seq_len == 0
    nlev, cdec, consts = _chunk_consts(tile, seq_len)
    cos, sin = _rope_tables(np.tile(PAST_LEN + np.arange(seq_len), tile // seq_len))
    rows = lambda width: pl.BlockSpec((tile, width), lambda n: (n, 0))
    widths = [2 * GROUP_W] + [GROUP_W] * 5 + [len(KEPT_GROUPS) * GROUP_W]
    args = [x2, cos, sin, *weights, *consts]
    w_in, w_out = weights[1], weights[5]
    narrow_shapes = [w.shape[1:] for w in (w_in, w_out)]
    *outs, w_in_bf, w_out_bf = pl.pallas_call(
        functools.partial(_mix_kernel, sample=True, tile=tile, chunks=1, seq_rows=seq_len, nlev=nlev, cdec=cdec,
                          smp_cdec=None, smp_seq_per_step=0),
        grid=(n_rows // tile,),
        in_specs=[rows(D_MODEL)] + [_full(a.shape) for a in args[1:]],
        out_specs=[rows(w) for w in widths] + [_full(shape) for shape in narrow_shapes],
        out_shape=[jax.ShapeDtypeStruct((n_rows, w), F32) for w in widths]
        + [jax.ShapeDtypeStruct(shape, BF16) for shape in narrow_shapes],
        scratch_shapes=_mix_scratch(tile) + [pltpu.VMEM((tile, GROUP_W), F32)] * 2,
        compiler_params=pltpu.CompilerParams(dimension_semantics=("arbitrary",),
                                             vmem_limit_bytes=VMEM_LIMITS["score_sample"]),
        name="score_sample",
    )(*args)
    return cdec, outs, w_in_bf, w_out_bf


def _mix_call(x2, weights, to_narrow, smp, smp_states, smp_cdec, *, n_seq, seq_len, tile, chunks, smp_len):
    n_rows = n_seq * seq_len
    assert seq_len % tile == 0 and tile % chunks == 0
    chunk = tile // chunks
    steps = seq_len // tile
    n_steps = n_seq * steps
    nlev, cdec, consts = _chunk_consts(chunk, min(seq_len, chunk))
    cos, sin = _rope_tables(np.arange(seq_len))
    n_smp = smp_states[0].shape[0]
    assert n_smp % n_steps == 0
    seq_per_step = n_smp // n_steps
    smp_rows = seq_per_step * smp_len
    assert smp_rows % SUBLANES == 0

    row_map = lambda b, i: (b * steps + i, 0)
    pos_map = lambda b, i: (i, 0)
    state_spec = pl.BlockSpec((1, N_HEADS, D_HEAD, D_HEAD), lambda b, i: (b, 0, 0, 0))
    smp_state_spec = pl.BlockSpec((seq_per_step, N_HEADS, D_HEAD, D_HEAD), lambda b, i: (b * steps + i, 0, 0, 0))
    smp_spec = lambda a: pl.BlockSpec((smp_rows, a.shape[1]), row_map)
    narrow_specs = [_row_block_spec(w.shape, n_steps, steps) for w in to_narrow]
    smp_scores, *smp_factors, smp_keep = smp
    side_in = [*smp_factors, smp_keep, smp_scores]
    in_specs = ([pl.BlockSpec((tile, D_MODEL), row_map), pl.BlockSpec((tile, D_HEAD), pos_map),
                 pl.BlockSpec((tile, D_HEAD), pos_map)] + [_full(w.shape) for w in weights]
                + [_full(c.shape) for c in consts] + narrow_specs + [smp_spec(a) for a in side_in]
                + [smp_state_spec, smp_state_spec])
    args = [x2, cos, sin, *weights, *consts, *to_narrow, *side_in, *smp_states]
    state_shape = lambda n: jax.ShapeDtypeStruct((n, N_HEADS, D_HEAD, D_HEAD), F32)
    return pl.pallas_call(
        functools.partial(_mix_kernel, sample=False, tile=tile, chunks=chunks, seq_rows=min(seq_len, chunk),
                          nlev=nlev, cdec=cdec, smp_cdec=smp_cdec, smp_seq_per_step=seq_per_step),
        grid=(n_seq, steps),
        in_specs=in_specs,
        out_specs=[pl.BlockSpec((tile, D_MODEL), row_map), state_spec, state_spec] + narrow_specs
        + [smp_spec(smp_scores), smp_state_spec, smp_state_spec],
        out_shape=[jax.ShapeDtypeStruct((n_rows, D_MODEL), F32), state_shape(n_seq), state_shape(n_seq)]
        + [jax.ShapeDtypeStruct(w.shape, BF16) for w in to_narrow]
        + [jax.ShapeDtypeStruct(smp_scores.shape, F32), state_shape(n_smp), state_shape(n_smp)],
        scratch_shapes=_mix_scratch(tile) + [pltpu.VMEM((tile, 2 * GROUP_W), F32)]
        + [pltpu.VMEM((tile, GROUP_W), F32)] * 2 + [pltpu.VMEM((N_HEADS, D_HEAD, D_HEAD), F32)] * 2,
        compiler_params=pltpu.CompilerParams(dimension_semantics=("arbitrary", "arbitrary"),
                                             vmem_limit_bytes=VMEM_LIMITS["mix_prompt"]),
        name="mix_prompt",
    )(*args)


def _ffn_kernel(*refs, sample, groups, rows):
    x_ref, w2_ref, wup_ref, cw_ref, cb_ref, wdown_ref, wf_ref = refs[:7]
    refs = refs[7:]
    if sample:
        (hist_ref, os_ref, keep_ref, na_ref, nb_ref, wout_ref, y_ref, hist_out_ref, h_scr, act_scr, mixed_scr,
         xin) = refs
        for h in range(N_HEADS):
            for group, norm, w_ref in ((0, _rmsnorm, na_ref), (1, _groupnorm, nb_ref)):
                cols = slice(group * GROUP_W + h * D_HEAD, group * GROUP_W + (h + 1) * D_HEAD)
                slot = KEPT_GROUPS.index(4 * group + 3)
                gate = keep_ref[:, slot * GROUP_W + h * D_HEAD:slot * GROUP_W + (h + 1) * D_HEAD]
                mixed_scr[:, cols] = (norm(os_ref[:, cols], w_ref[...]) * (gate * _sigmoid(gate))).astype(BF16)
        xin[...] = x_ref[...] + _mm(mixed_scr[...], wout_ref[...])
    else:
        y_ref, hist_out_ref, h_scr, act_scr, tail_scr = refs
        xin = x_ref
        step = pl.program_id(1)

        @pl.when(step == 0)
        def _():
            tail_scr[...] = jnp.zeros_like(tail_scr)

    G, L, P = groups, rows, SUBLANES
    parts = 1 if sample else ROW_PARTS
    LP = L // parts
    row_id = lax.broadcasted_iota(jnp.int32, (G, P, FF_COLS), 1)

    def shifted(up, prev2, prev1):
        r1, r2 = pltpu.roll(up, 1, 1), pltpu.roll(up, 2, 1)
        top1 = jnp.where(row_id == 0, prev1, r1[:, :P])
        top2 = jnp.where(row_id == 0, prev2, jnp.where(row_id == 1, prev1, r2[:, :P]))
        if LP == P:
            return top1, top2
        return jnp.concatenate([top1, r1[:, P:]], axis=1), jnp.concatenate([top2, r2[:, P:]], axis=1)

    tails = {}
    for part in range(parts):
        rows_p = slice(part * G * LP, (part + 1) * G * LP)
        h_scr[rows_p, :] = _rmsnorm(xin[rows_p, :], w2_ref[...]).astype(BF16)
        for n in range(0, D_FF, FF_COLS):
            conv = []
            for cols in (slice(n, n + FF_COLS), slice(D_FF + n, D_FF + n + FF_COLS)):
                up = _mm(h_scr[rows_p, :], wup_ref[0, :, cols]).reshape(G, LP, FF_COLS)
                if sample:
                    prev2, prev1 = hist_ref[:, 0:1, cols], hist_ref[:, 1:2, cols]
                    hist_out_ref[:, :, cols] = up[:, LP - 2:, :]
                else:
                    if part == 0:
                        prev2, prev1 = tail_scr[:, P - 2:P - 1, cols], tail_scr[:, P - 1:P, cols]
                    else:
                        prev = tails[cols.start]
                        prev2, prev1 = prev[:, P - 2:P - 1, :], prev[:, P - 1:P, :]
                    tails[cols.start] = up[:, LP - P:, :]
                    if part == parts - 1:
                        tail_scr[:, :, cols] = up[:, LP - P:, :]
                sh1, sh2 = shifted(up, prev2, prev1)
                conv.append(cb_ref[:, cols] + cw_ref[0:1, cols] * sh2 + cw_ref[1:2, cols] * sh1 + cw_ref[2:3, cols] * up)
            u, g = (c.reshape(G * LP, FF_COLS).astype(BF16) for c in conv)
            one = jnp.ones((), BF16)
            act_scr[rows_p, n:n + FF_COLS] = (g * (one / (one + jnp.exp(-g)))) * u

        x2 = xin[rows_p, :] + _mm(act_scr[rows_p, :], wdown_ref[0])
        y_ref[rows_p, :] = _rmsnorm(x2, wf_ref[...])

    if not sample:
        @pl.when(step == pl.num_programs(1) - 1)
        def _():
            hist_out_ref[...] = tail_scr[:, P - 2:, :]


def _ffn_call(x, w2, w_up, cw, cb, w_down, wf, smp, *, n_seq, seq_len, tile):
    sample = smp is not None
    n_rows = n_seq * seq_len
    hist_shape = jax.ShapeDtypeStruct((n_seq, CONV_W - 1, FF2), F32)
    if sample:
        hist, scores, keep, na, nb, w_out = smp
        groups, rows = tile // seq_len, seq_len
        grid = (n_rows // tile,)
        row_map = lambda n: (n, 0)
        hist_spec = pl.BlockSpec((groups, CONV_W - 1, FF2), lambda n: (n, 0, 0))
        extra_in = [hist_spec, pl.BlockSpec((tile, scores.shape[1]), row_map),
                    pl.BlockSpec((tile, keep.shape[1]), row_map), _full(na.shape), _full(nb.shape), _full(w_out.shape)]
        extra_args = [hist, scores, keep, na, nb, w_out]
        scratch = [pltpu.VMEM((tile, 2 * GROUP_W), BF16), pltpu.VMEM((tile, D_MODEL), F32)]
    else:
        assert seq_len % tile == 0
        groups, rows = 1, tile
        steps = seq_len // tile
        grid = (n_seq, steps)
        row_map = lambda b, i: (b * steps + i, 0)
        hist_spec = pl.BlockSpec((1, CONV_W - 1, FF2), lambda b, i: (b, 0, 0))
        extra_in, extra_args = [], []
        scratch = [pltpu.VMEM((1, SUBLANES, FF2), F32)]
    in_specs = [pl.BlockSpec((tile, D_MODEL), row_map), _full(w2.shape), _full(w_up.shape), _full(cw.shape),
                _full(cb.shape), _full(w_down.shape), _full(wf.shape)] + extra_in
    args = [x, w2, w_up, cw, cb, w_down, wf] + extra_args
    name = "ffn_sample" if sample else "ffn_prompt"
    return pl.pallas_call(
        functools.partial(_ffn_kernel, sample=sample, groups=groups, rows=rows),
        grid=grid,
        in_specs=in_specs,
        out_specs=[pl.BlockSpec((tile, D_MODEL), row_map), hist_spec],
        out_shape=[jax.ShapeDtypeStruct((n_rows, D_MODEL), F32), hist_shape],
        scratch_shapes=[pltpu.VMEM((tile, D_MODEL), BF16), pltpu.VMEM((tile, D_FF), BF16)] + scratch,
        compiler_params=pltpu.CompilerParams(dimension_semantics=("arbitrary",) * len(grid),
                                             vmem_limit_bytes=VMEM_LIMITS[name]),
        name=name,
    )(*args)


def kernel(x_prompt, x_sample, state_hgrn, state_ret, state_conv, w_norm1, w_in, hgrn_lb, hgrn_norm_w, ret_norm_w,
           w_out, w_norm2, w_ffn_in, conv_w, conv_b, w_ffn_out, w_norm_f):
    assert w_in.shape == (1, D_MODEL, IN_WIDTH) and hgrn_lb.shape == (2, GROUP_W)
    n_seq, seq_len, _ = x_prompt.shape
    n_smp, smp_len, _ = x_sample.shape
    xs = x_sample.reshape(n_smp * smp_len, D_MODEL)

    smp_cdec, smp, w_in_bf, w_out_bf = _smp_score_call(
        xs, (w_norm1, w_in, hgrn_lb, hgrn_norm_w, ret_norm_w, w_out), seq_len=smp_len, tile=256)
    mix_w = (w_norm1, w_in_bf, hgrn_lb, hgrn_norm_w, ret_norm_w, w_out_bf)
    x1, ha_p, rb_p, w_up, w_down, smp_scores, ha_s, rb_s = _mix_call(
        x_prompt.reshape(n_seq * seq_len, D_MODEL), mix_w, (w_ffn_in, w_ffn_out), smp,
        (state_hgrn[0], state_ret[0]), smp_cdec, n_seq=n_seq, seq_len=seq_len, tile=512, chunks=2, smp_len=smp_len)
    ffn_w = (w_norm2, w_up, conv_w[0], conv_b, w_down, w_norm_f.reshape(1, D_MODEL))
    y_p, cv_p = _ffn_call(x1, *ffn_w, None, n_seq=n_seq, seq_len=seq_len, tile=512)
    y_s, cv_s = _ffn_call(xs, *ffn_w, (state_conv[0], smp_scores, smp[-1], hgrn_norm_w, ret_norm_w, w_out_bf),
                          n_seq=n_smp, seq_len=smp_len, tile=256)
    return (y_p.reshape(x_prompt.shape), y_s.reshape(x_sample.shape), ha_p[None], rb_p[None], cv_p[None],
            ha_s[None], rb_s[None], cv_s[None])
```
